```python
import jax, jax.numpy as jnp
from jax import lax
import numpy as np

D_MODEL = 1024
BATCH = 8
SEQ = 8192
DEPTH = 1
DEC_BATCH = 16
DEC_SEQ = 32
PAST_LEN = 2048

CHUNK = 64
Q_BLOCK = 128
MLA_HEADS = 8
QK_NOPE = 64
QK_ROPE = 32
QK_HEAD = QK_NOPE + QK_ROPE
V_HEAD = 64
Q_LORA = 384
KV_LORA = 256
MLA_WIDTH = MLA_HEADS * V_HEAD
ROPE_THETA = 10000.0
POOL_WINDOWS = (2, 4, 8, 16)
POOL_GROUPS = len(POOL_WINDOWS)
POOL_WIDTH = D_MODEL - MLA_WIDTH
POOL_GROUP_DIM = POOL_WIDTH // POOL_GROUPS
POOL_HIST = max(POOL_WINDOWS) - 1
IN_WIDTH = Q_LORA + KV_LORA + QK_ROPE + POOL_WIDTH
N_EXPERTS = 32
TOP_K = 4
D_FF_EXPERT = 1024
SWIGLU_LIMIT = 7.0
SWIGLU_ALPHA = 1.702
MOE_BLOCK = 256
N_MOD = 6
EPS = 1e-6

kernel_name = "hymba_mla_pool_moe_stream_step"


def rmsnorm(x, g):
    xf = x.astype(jnp.float32)
    y = xf * lax.rsqrt(jnp.mean(xf * xf, axis=-1, keepdims=True) + EPS)
    return (y * g.astype(jnp.float32)).astype(x.dtype)


def rope(x, pos):
    half = QK_ROPE // 2
    inv = ROPE_THETA ** (-jnp.arange(half, dtype=jnp.float32) / half)
    ang = pos.astype(jnp.float32)[:, None] * inv[None, :]
    cos, sin = jnp.cos(ang), jnp.sin(ang)
    if x.ndim == 4:
        cos, sin = cos[:, None, :], sin[:, None, :]
    x1 = x[..., :half].astype(jnp.float32)
    x2 = x[..., half:].astype(jnp.float32)
    return jnp.concatenate([x1 * cos - x2 * sin, x1 * sin + x2 * cos], axis=-1).astype(x.dtype)


def attend(q_nope, q_pe, k_nope, k_pe, v, mask):
    s = (jnp.einsum('bqhd,bkhd->bhqk', q_nope, k_nope)
         + jnp.einsum('bqhr,bkr->bhqk', q_pe, k_pe)).astype(jnp.float32) * (QK_HEAD ** -0.5)
    if mask is not None:
        s = jnp.where(mask[None, None], s, -jnp.inf)
    p = jax.nn.softmax(s, axis=-1).astype(v.dtype)
    return jnp.einsum('bhqk,bkhd->bqhd', p, v)


def mla_prompt(q_nope, q_pe, kv, k_pe, w_uk, w_uv):
    B, S = kv.shape[0], kv.shape[1]
    k_nope = jnp.einsum('bsl,lhd->bshd', kv, w_uk)
    v = jnp.einsum('bsl,lhd->bshd', kv, w_uv)
    nb = S // Q_BLOCK
    k_chunk = jnp.arange(S) // CHUNK
    qn = q_nope.reshape(B, nb, Q_BLOCK, MLA_HEADS, QK_NOPE).transpose(1, 0, 2, 3, 4)
    qp = q_pe.reshape(B, nb, Q_BLOCK, MLA_HEADS, QK_ROPE).transpose(1, 0, 2, 3, 4)

    def block(args):
        i, qn_b, qp_b = args
        q_chunk = (i * Q_BLOCK + jnp.arange(Q_BLOCK)) // CHUNK
        mask = k_chunk[None, :] <= q_chunk[:, None]
        return attend(qn_b, qp_b, k_nope, k_pe, v, mask)

    out = lax.map(block, (jnp.arange(nb), qn, qp))
    return out.transpose(1, 0, 2, 3, 4).reshape(B, S, MLA_WIDTH)


def mla_sample(q_nope, q_pe, kv_all, kpe_all, w_uk, w_uv):
    B, T = q_nope.shape[0], q_nope.shape[1]
    k_nope = jnp.einsum('bsl,lhd->bshd', kv_all, w_uk)
    v = jnp.einsum('bsl,lhd->bshd', kv_all, w_uv)
    return attend(q_nope, q_pe, k_nope, kpe_all, v, None).reshape(B, T, MLA_WIDTH)


def pool_mix(u_full, pos, w_pool, pool_scale):
    B, L, C = u_full.shape
    T = L - POOL_HIST
    uf = u_full.astype(jnp.float32)
    cs = jnp.concatenate([jnp.zeros((B, 1, C), jnp.float32), jnp.cumsum(uf, axis=1)], axis=1)
    end = cs[:, POOL_HIST + 1:]
    means = []
    for g, w in enumerate(POOL_WINDOWS):
        sl = slice(g * POOL_GROUP_DIM, (g + 1) * POOL_GROUP_DIM)
        start = cs[:, POOL_HIST + 1 - w:POOL_HIST + 1 - w + T, sl]
        cnt = jnp.minimum(pos + 1, w).astype(jnp.float32)[None, :, None]
        means.append((end[..., sl] - start) / cnt)
    d = jnp.concatenate(means, axis=-1) - uf[:, POOL_HIST:]
    d = d.reshape(B, T, POOL_GROUPS, POOL_GROUP_DIM).astype(u_full.dtype)
    y = jnp.einsum('btgc,gcd->btgd', d, w_pool).reshape(B, T, POOL_WIDTH)
    return y * pool_scale


def token_mixer(h, pos, hist_kv, hist_kpe, hist_pool, w_in, g_q_a, w_q_b, g_kv_a, w_uk, w_uv,
                w_pool, pool_scale, w_o):
    B, T, _ = h.shape
    z = h @ w_in
    qa, kv_a, kpe_a, u = jnp.split(z, [Q_LORA, Q_LORA + KV_LORA, Q_LORA + KV_LORA + QK_ROPE], axis=-1)
    q = jnp.einsum('btl,lhd->bthd', rmsnorm(qa, g_q_a), w_q_b)
    q_nope = q[..., :QK_NOPE]
    q_pe = rope(q[..., QK_NOPE:], pos)
    kv_new = rmsnorm(kv_a, g_kv_a)
    kpe_new = rope(kpe_a, pos)
    if hist_kv is None:
        attn = mla_prompt(q_nope, q_pe, kv_new, kpe_new, w_uk, w_uv)
        hist_pool = jnp.zeros((B, POOL_HIST, POOL_WIDTH), u.dtype)
    else:
        kv_all = jnp.concatenate([hist_kv.astype(kv_new.dtype), kv_new], axis=1)
        kpe_all = jnp.concatenate([hist_kpe.astype(kpe_new.dtype), kpe_new], axis=1)
        attn = mla_sample(q_nope, q_pe, kv_all, kpe_all, w_uk, w_uv)
    u_full = jnp.concatenate([hist_pool.astype(u.dtype), u], axis=1)
    pooled = pool_mix(u_full, pos, w_pool, pool_scale)
    out = jnp.concatenate([attn, pooled.astype(attn.dtype)], axis=-1) @ w_o
    return out, kv_new, kpe_new, u_full[:, -POOL_HIST:]


def moe(h, w_router, b_router, w_gu, b_gu, w_down, b_down):
    B, T, D = h.shape
    xt = h.reshape(-1, D)
    N = xt.shape[0]
    logits = (xt @ w_router).astype(jnp.float32) + b_router.astype(jnp.float32)
    top_val, top_idx = lax.top_k(logits, TOP_K)
    gates = jax.nn.softmax(top_val, axis=-1)
    NK = N * TOP_K
    flat_e = top_idx.reshape(-1)
    flat_tok = jnp.repeat(jnp.arange(N, dtype=jnp.int32), TOP_K)
    flat_gate = gates.reshape(-1)
    order = jnp.argsort(flat_e)
    se = flat_e[order]
    counts = jnp.bincount(flat_e, length=N_EXPERTS)
    padded = (counts + MOE_BLOCK - 1) // MOE_BLOCK * MOE_BLOCK
    pad_end = jnp.cumsum(padded)
    pad_start = pad_end - padded
    start = jnp.cumsum(counts) - counts
    dest = pad_start[se] + jnp.arange(NK) - start[se]
    n_blocks = -(-NK // MOE_BLOCK) + N_EXPERTS
    P = n_blocks * MOE_BLOCK
    row_tok = jnp.zeros((P,), jnp.int32).at[dest].set(flat_tok[order])
    row_gate = jnp.zeros((P,), jnp.float32).at[dest].set(flat_gate[order])
    block_e = jnp.minimum(jnp.searchsorted(pad_end, jnp.arange(n_blocks) * MOE_BLOCK, side='right'),
                          N_EXPERTS - 1)

    def expert_block(args):
        tok, gate, e = args
        xb = xt[tok]
        gu = xb @ w_gu[e] + b_gu[e]
        g, u = jnp.split(gu, 2, axis=-1)
        g = jnp.minimum(g, SWIGLU_LIMIT)
        u = jnp.clip(u, -SWIGLU_LIMIT, SWIGLU_LIMIT)
        a = (u + 1) * (g * jax.nn.sigmoid(SWIGLU_ALPHA * g))
        return (a @ w_down[e] + b_down[e]) * gate[:, None].astype(xb.dtype)

    rows = lax.map(expert_block, (row_tok.reshape(n_blocks, MOE_BLOCK),
                                  row_gate.reshape(n_blocks, MOE_BLOCK), block_e))
    y = jax.ops.segment_sum(rows.reshape(P, D), row_tok, num_segments=N)
    return y.reshape(B, T, D)


def layer(x, c, pos, hist_kv, hist_kpe, hist_pool, w_ada, b_ada, g_pre_mix, w_in, g_q_a, w_q_b,
          g_kv_a, w_uk, w_uv, w_pool, pool_scale, w_o, g_post_mix, g_pre_ffn, w_router, b_router,
          w_gu, b_gu, w_down, b_down, g_post_ffn):
    mod = jax.nn.silu(c) @ w_ada + b_ada
    shift_m, scale_m, gate_m, shift_f, scale_f, gate_f = [m[:, None, :] for m in jnp.split(mod, N_MOD, axis=-1)]
    h = rmsnorm(x, g_pre_mix) * (1 + scale_m) + shift_m
    mix, kv_new, kpe_new, pool_new = token_mixer(h, pos, hist_kv, hist_kpe, hist_pool, w_in, g_q_a,
                                                 w_q_b, g_kv_a, w_uk, w_uv, w_pool, pool_scale, w_o)
    x = x + gate_m * rmsnorm(mix, g_post_mix)
    h = rmsnorm(x, g_pre_ffn) * (1 + scale_f) + shift_f
    x = x + gate_f * rmsnorm(moe(h, w_router, b_router, w_gu, b_gu, w_down, b_down), g_post_ffn)
    return x, kv_new, kpe_new, pool_new


def setup_inputs(seed: int = 0) -> dict:
    key = jax.random.key(seed)
    ks = jax.random.split(key, 32)
    f32 = jnp.float32
    L, D = DEPTH, D_MODEL

    def nrm(k, shape, scale):
        return jax.random.normal(k, shape, f32) * scale

    def gain(k, shape):
        return 1.0 + 0.05 * jax.random.normal(k, shape, f32)

    return {
        "x_prompt": nrm(ks[0], (BATCH, SEQ, D), 1.0),
        "x_sample": nrm(ks[1], (DEC_BATCH, DEC_SEQ, D), 1.0),
        "c_prompt": nrm(ks[2], (BATCH, D), 1.0),
        "c_sample": nrm(ks[3], (DEC_BATCH, D), 1.0),
        "cache_kv_latent": nrm(ks[4], (L, DEC_BATCH, PAST_LEN, KV_LORA), 1.0),
        "cache_k_rope": nrm(ks[5], (L, DEC_BATCH, PAST_LEN, QK_ROPE), 1.0),
        "state_pool": nrm(ks[6], (L, DEC_BATCH, POOL_HIST, POOL_WIDTH), 1.0),
        "w_ada": nrm(ks[7], (L, D, N_MOD * D), 0.5 * D ** -0.5),
        "b_ada": nrm(ks[8], (L, N_MOD * D), 0.02),
        "g_pre_mix": gain(ks[9], (L, D)),
        "w_in": nrm(ks[10], (L, D, IN_WIDTH), D ** -0.5),
        "g_q_a": gain(ks[11], (L, Q_LORA)),
        "w_q_b": nrm(ks[12], (L, Q_LORA, MLA_HEADS, QK_HEAD), Q_LORA ** -0.5),
        "g_kv_a": gain(ks[13], (L, KV_LORA)),
        "w_uk": nrm(ks[14], (L, KV_LORA, MLA_HEADS, QK_NOPE), KV_LORA ** -0.5),
        "w_uv": nrm(ks[15], (L, KV_LORA, MLA_HEADS, V_HEAD), KV_LORA ** -0.5),
        "w_pool": nrm(ks[16], (L, POOL_GROUPS, POOL_GROUP_DIM, POOL_GROUP_DIM), POOL_GROUP_DIM ** -0.5),
        "pool_scale": gain(ks[17], (L, POOL_WIDTH)),
        "w_o": nrm(ks[18], (L, D, D), D ** -0.5),
        "g_post_mix": gain(ks[19], (L, D)),
        "g_pre_ffn": gain(ks[20], (L, D)),
        "w_router": nrm(ks[21], (L, D, N_EXPERTS), D ** -0.5),
        "b_router": nrm(ks[22], (L, N_EXPERTS), 0.01),
        "w_gu": nrm(ks[23], (L, N_EXPERTS, D, 2 * D_FF_EXPERT), D ** -0.5),
        "b_gu": nrm(ks[24], (L, N_EXPERTS, 2 * D_FF_EXPERT), 0.02),
        "w_down": nrm(ks[25], (L, N_EXPERTS, D_FF_EXPERT, D), D_FF_EXPERT ** -0.5),
        "b_down": nrm(ks[26], (L, N_EXPERTS, D), 0.02),
        "g_post_ffn": gain(ks[27], (L, D)),
    }


def reference(x_prompt, x_sample, c_prompt, c_sample, cache_kv_latent, cache_k_rope, state_pool,
              w_ada, b_ada, g_pre_mix, w_in, g_q_a, w_q_b, g_kv_a, w_uk, w_uv, w_pool, pool_scale,
              w_o, g_post_mix, g_pre_ffn, w_router, b_router, w_gu, b_gu, w_down, b_down, g_post_ffn):
    pos_p = jnp.arange(x_prompt.shape[1], dtype=jnp.int32)
    pos_s = PAST_LEN + jnp.arange(x_sample.shape[1], dtype=jnp.int32)
    yp, ys = x_prompt, x_sample
    kv_p, kpe_p, pool_p, kv_s, kpe_s, pool_s = [], [], [], [], [], []
    for l in range(DEPTH):
        w = (w_ada[l], b_ada[l], g_pre_mix[l], w_in[l], g_q_a[l], w_q_b[l], g_kv_a[l], w_uk[l], w_uv[l],
             w_pool[l], pool_scale[l], w_o[l], g_post_mix[l], g_pre_ffn[l], w_router[l], b_router[l],
             w_gu[l], b_gu[l], w_down[l], b_down[l], g_post_ffn[l])
        yp, a, b, c = layer(yp, c_prompt, pos_p, None, None, None, *w)
        kv_p.append(a); kpe_p.append(b); pool_p.append(c)
        ys, a, b, c = layer(ys, c_sample, pos_s, cache_kv_latent[l], cache_k_rope[l], state_pool[l], *w)
        kv_s.append(a); kpe_s.append(b); pool_s.append(c)
    return (yp, ys, jnp.stack(kv_p), jnp.stack(kpe_p), jnp.stack(pool_p),
            jnp.stack(kv_s), jnp.stack(kpe_s), jnp.stack(pool_s))
```

```python
import functools

import jax
import jax.numpy as jnp
from jax import lax
from jax.experimental import pallas as pl
from jax.experimental.pallas import tpu as pltpu

F32 = jnp.float32
BF16 = jnp.bfloat16

D_MODEL = 1024
CHUNK = 64
N_HEADS = 8
QK_NOPE = 64
QK_ROPE = 32
QK_HEAD = QK_NOPE + QK_ROPE
V_HEAD = 64
Q_LORA = 384
KV_LORA = 256
ROPE_THETA = 10000.0
POOL_WINDOWS = (2, 4, 8, 16)
POOL_GROUP_DIM = 128
POOL_WIDTH = POOL_GROUP_DIM * len(POOL_WINDOWS)
POOL_HIST = max(POOL_WINDOWS) - 1
HIST_ROWS = POOL_HIST + 1
N_EXPERTS = 32
TOP_K = 4
D_FF = 1024
SWIGLU_LIMIT = 7.0
SWIGLU_ALPHA = 1.702
N_MOD = 6
EPS = 1e-6

LANES = 128
HEAD_SLABS = N_HEADS * LANES
ONE_LANE = V_HEAD
IN_EXT = Q_LORA + KV_LORA + POOL_WIDTH + 2 * LANES
U_OFF = Q_LORA + KV_LORA
KPE_OFF = U_OFF + POOL_WIDTH
SM_SCALE = QK_HEAD ** -0.5
NEG_INF = float("-inf")

MOE_BLOCK = 512
VMEM_LIMIT = 56 * 1024 * 1024


def _cparams(n_axes, vmem=VMEM_LIMIT):
    return pltpu.CompilerParams(dimension_semantics=("arbitrary",) * n_axes, vmem_limit_bytes=vmem)


def _rms(x, g):
    return x * lax.rsqrt(jnp.mean(x * x, axis=-1, keepdims=True) + EPS) * g


def _ada_kernel(c_ref, w_ref, b_ref, o_ref):
    c = c_ref[...]
    s = (c * jax.nn.sigmoid(c)).astype(BF16)
    o_ref[...] = jnp.dot(s, w_ref[...].astype(BF16), preferred_element_type=F32) + b_ref[...]


def _ada(c, w_ada, b_ada):
    nb = c.shape[0]
    return pl.pallas_call(
        _ada_kernel,
        out_shape=jax.ShapeDtypeStruct((nb, N_MOD * D_MODEL), F32),
        grid=(N_MOD,),
        in_specs=[pl.BlockSpec((nb, D_MODEL), lambda j: (0, 0)),
                  pl.BlockSpec((D_MODEL, D_MODEL), lambda j: (0, j)),
                  pl.BlockSpec((1, D_MODEL), lambda j: (0, j))],
        out_specs=pl.BlockSpec((nb, D_MODEL), lambda j: (0, j)),
        compiler_params=_cparams(1),
        name="ada",
    )(c, w_ada, b_ada.reshape(1, -1))


def _premix_kernel(x_ref, mod_ref, cos_ref, sin_ref, hist_ref, gpre_ref, gqa_ref, gkv_ref, pscale_ref,
                   win_ref, wq_ref, wkv_ref, wpool_ref,
                   q_ref, k_ref, v_ref, kv_ref, kpe_ref, pooled_ref, poolnew_ref,
                   ubuf, *, tm, pos0):
    t = pl.program_id(1)
    x = x_ref[...]
    h = (_rms(x, gpre_ref[...]) * (1.0 + mod_ref[1:2, :]) + mod_ref[0:1, :]).astype(BF16)
    z = jnp.dot(h, win_ref[...], preferred_element_type=F32)
    cosv = cos_ref[...]
    sinv = sin_ref[...]

    qan = _rms(z[:, :Q_LORA], gqa_ref[...]).astype(BF16)
    qq = jnp.dot(qan, wq_ref[...], preferred_element_type=F32)
    for hd in range(N_HEADS):
        a = qq[:, hd * LANES:(hd + 1) * LANES]
        b = qq[:, HEAD_SLABS + hd * LANES:HEAD_SLABS + (hd + 1) * LANES]
        q_ref[:, hd * LANES:(hd + 1) * LANES] = ((a * cosv + b * sinv) * SM_SCALE).astype(BF16)

    kvn = _rms(z[:, Q_LORA:U_OFF], gkv_ref[...])
    kv_ref[...] = kvn
    kk = jnp.dot(kvn.astype(BF16), wkv_ref[...], preferred_element_type=F32)
    kslab = z[:, KPE_OFF:KPE_OFF + LANES] * cosv + z[:, KPE_OFF + LANES:KPE_OFF + 2 * LANES] * sinv
    kpe_ref[...] = kslab
    one_lane = (lax.broadcasted_iota(jnp.int32, (1, LANES), 1) == ONE_LANE).astype(F32)
    for hd in range(N_HEADS):
        sl = slice(hd * LANES, (hd + 1) * LANES)
        k_ref[:, sl] = (kk[:, sl] + kslab).astype(BF16)
        v_ref[:, sl] = (kk[:, HEAD_SLABS + hd * LANES:HEAD_SLABS + (hd + 1) * LANES] + one_lane).astype(BF16)

    @pl.when(t == 0)
    def _():
        ubuf[0:HIST_ROWS, :] = hist_ref[...]

    @pl.when(t > 0)
    def _():
        ubuf[0:HIST_ROWS, :] = ubuf[tm:tm + HIST_ROWS, :]

    ubuf[HIST_ROWS:HIST_ROWS + tm, :] = z[:, U_OFF:KPE_OFF]
    pos = pos0 + t * tm + lax.broadcasted_iota(jnp.int32, (tm, 1), 0)
    for g, w in enumerate(POOL_WINDOWS):
        sl = slice(g * POOL_GROUP_DIM, (g + 1) * POOL_GROUP_DIM)
        u = ubuf[HIST_ROWS:HIST_ROWS + tm, sl]
        acc = u
        for j in range(1, w):
            acc = acc + ubuf[HIST_ROWS - j:HIST_ROWS - j + tm, sl]
        cnt = jnp.minimum(pos + 1, w).astype(F32)
        d = (acc / cnt - u).astype(BF16)
        y = jnp.dot(d, wpool_ref[g], preferred_element_type=F32) * pscale_ref[:, sl]
        pooled_ref[:, sl] = y.astype(BF16)
    poolnew_ref[...] = ubuf[tm:tm + HIST_ROWS, :]


def _premix(x, mod, cos_t, sin_t, hist, w, *, tm, pos0):
    nb, nt, _ = x.shape
    grid = (nb, nt // tm)
    tok = lambda width: pl.BlockSpec((None, tm, width), lambda b, t: (b, t, 0))
    full = lambda a: pl.BlockSpec(a.shape, lambda b, t: (0,) * a.ndim)
    tab = pl.BlockSpec((tm, LANES), lambda b, t: (t, 0))
    perb = lambda rows, width: pl.BlockSpec((None, rows, width), lambda b, t: (b, 0, 0))
    outs = [
        jax.ShapeDtypeStruct((nb, nt, HEAD_SLABS), BF16),
        jax.ShapeDtypeStruct((nb, nt, HEAD_SLABS), BF16),
        jax.ShapeDtypeStruct((nb, nt, HEAD_SLABS), BF16),
        jax.ShapeDtypeStruct((nb, nt, KV_LORA), F32),
        jax.ShapeDtypeStruct((nb, nt, LANES), F32),
        jax.ShapeDtypeStruct((nb, nt, POOL_WIDTH), BF16),
        jax.ShapeDtypeStruct((nb, HIST_ROWS, POOL_WIDTH), F32),
    ]
    return pl.pallas_call(
        functools.partial(_premix_kernel, tm=tm, pos0=pos0),
        out_shape=outs,
        grid=grid,
        in_specs=[tok(D_MODEL), perb(N_MOD, D_MODEL), tab, tab, perb(HIST_ROWS, POOL_WIDTH),
                  full(w["g_pre_mix"]), full(w["g_q_a"]), full(w["g_kv_a"]), full(w["pool_scale"]),
                  full(w["w_in"]), full(w["w_q"]), full(w["w_kv"]), full(w["w_pool"])],
        out_specs=[tok(HEAD_SLABS), tok(HEAD_SLABS), tok(HEAD_SLABS), tok(KV_LORA), tok(LANES),
                   tok(POOL_WIDTH), perb(HIST_ROWS, POOL_WIDTH)],
        scratch_shapes=[pltpu.VMEM((HIST_ROWS + tm, POOL_WIDTH), F32)],
        compiler_params=_cparams(2),
        name="premix",
    )(x, mod, cos_t, sin_t, hist, w["g_pre_mix"], w["g_q_a"], w["g_kv_a"], w["pool_scale"],
      w["w_in"], w["w_q"], w["w_kv"], w["w_pool"])


def _kvproj_kernel(kv_ref, kpe_ref, wkv_ref, k_ref, v_ref):
    kk = jnp.dot(kv_ref[...].astype(BF16), wkv_ref[...], preferred_element_type=F32)
    kslab = kpe_ref[...]
    one_lane = (lax.broadcasted_iota(jnp.int32, (1, LANES), 1) == ONE_LANE).astype(F32)
    for hd in range(N_HEADS):
        sl = slice(hd * LANES, (hd + 1) * LANES)
        k_ref[:, sl] = (kk[:, sl] + kslab).astype(BF16)
        v_ref[:, sl] = (kk[:, HEAD_SLABS + hd * LANES:HEAD_SLABS + (hd + 1) * LANES] + one_lane).astype(BF16)


def _kvproj(kv, kpe_slab, w_kv, *, tm):
    nb, nt, _ = kv.shape
    tok = lambda width: pl.BlockSpec((None, tm, width), lambda b, t: (b, t, 0))
    return pl.pallas_call(
        _kvproj_kernel,
        out_shape=[jax.ShapeDtypeStruct((nb, nt, HEAD_SLABS), BF16)] * 2,
        grid=(nb, nt // tm),
        in_specs=[tok(KV_LORA), tok(LANES), pl.BlockSpec(w_kv.shape, lambda b, t: (0, 0))],
        out_specs=[tok(HEAD_SLABS), tok(HEAD_SLABS)],
        compiler_params=_cparams(2),
        name="kvproj",
    )(kv, kpe_slab, w_kv)


def _attn_kernel(q_ref, k_ref, v_ref, o_ref, m_ref, acc_ref, *, tq, tk, nk, causal, kv_len):
    qi = pl.program_id(1)
    ki = pl.program_id(2)
    last = ((qi + 1) * tq - 1) // tk if causal else nk - 1
    need_len_mask = kv_len < nk * tk

    @pl.when(ki == 0)
    def _():
        m_ref[...] = jnp.full(m_ref.shape, NEG_INF, F32)
        acc_ref[...] = jnp.zeros(acc_ref.shape, F32)

    def step(masked):
        if masked:
            qpos = qi * tq + lax.broadcasted_iota(jnp.int32, (tq, tk), 0)
            kpos = ki * tk + lax.broadcasted_iota(jnp.int32, (tq, tk), 1)
            vis = None
            if causal:
                vis = (kpos // CHUNK) <= (qpos // CHUNK)
            if need_len_mask:
                lm = kpos < kv_len
                vis = lm if vis is None else (vis & lm)
        for hd in range(N_HEADS):
            sl = slice(hd * LANES, (hd + 1) * LANES)
            s = lax.dot_general(q_ref[:, sl], k_ref[:, sl], (((1,), (1,)), ((), ())),
                                preferred_element_type=F32)
            if masked:
                s = jnp.where(vis, s, NEG_INF)
            m_prev = m_ref[hd]
            m_new = jnp.maximum(m_prev, jnp.max(s, axis=1, keepdims=True))
            alpha = jnp.exp(m_prev - m_new)
            p = jnp.exp(s - m_new[:, 0:1]).astype(BF16)
            acc_ref[hd] = alpha * acc_ref[hd] + jnp.dot(p, v_ref[:, sl], preferred_element_type=F32)
            m_ref[hd] = m_new

    if causal or need_len_mask:
        @pl.when(ki < last)
        def _():
            step(False)

        @pl.when(ki == last)
        def _():
            step(True)
    else:
        step(False)

    @pl.when(ki == nk - 1)
    def _():
        for hd in range(N_HEADS):
            acc = acc_ref[hd]
            o_ref[:, hd * LANES:(hd + 1) * LANES] = (acc / acc[:, ONE_LANE:ONE_LANE + 1]).astype(BF16)


def _attention(q, k, v, *, tq, tk, causal, kv_len):
    nb, nq_tot, _ = q.shape
    nk = k.shape[1] // tk
    nq = nq_tot // tq
    if causal:
        kmap = lambda b, i, j: (b, jnp.minimum(j, ((i + 1) * tq - 1) // tk), 0)
    else:
        kmap = lambda b, i, j: (b, j, 0)
    return pl.pallas_call(
        functools.partial(_attn_kernel, tq=tq, tk=tk, nk=nk, causal=causal, kv_len=kv_len),
        out_shape=jax.ShapeDtypeStruct((nb, nq_tot, HEAD_SLABS), BF16),
        grid=(nb, nq, nk),
        in_specs=[pl.BlockSpec((None, tq, HEAD_SLABS), lambda b, i, j: (b, i, 0)),
                  pl.BlockSpec((None, tk, HEAD_SLABS), kmap),
                  pl.BlockSpec((None, tk, HEAD_SLABS), kmap)],
        out_specs=pl.BlockSpec((None, tq, HEAD_SLABS), lambda b, i, j: (b, i, 0)),
        scratch_shapes=[pltpu.VMEM((N_HEADS, tq, LANES), F32), pltpu.VMEM((N_HEADS, tq, LANES), F32)],
        compiler_params=_cparams(3),
        name="attention",
    )(q, k, v)


def _postmix_kernel(attn_ref, pooled_ref, x_ref, mod_ref, cnt0_ref, gpost_ref, gffn_ref,
                    woa_ref, wop_ref, wr_ref, br_ref,
                    x1_ref, h2_ref, ri_ref, rg_ref, cnt_ref,
                    carry, ltri, *, tm):
    first = (pl.program_id(0) == 0) & (pl.program_id(1) == 0)

    @pl.when(first)
    def _():
        carry[...] = cnt0_ref[...]
        r = lax.broadcasted_iota(jnp.int32, (tm, tm), 0)
        c = lax.broadcasted_iota(jnp.int32, (tm, tm), 1)
        ltri[...] = (r > c).astype(BF16)

    mix = (jnp.dot(attn_ref[...], woa_ref[...], preferred_element_type=F32)
           + jnp.dot(pooled_ref[...], wop_ref[...], preferred_element_type=F32))
    x1 = x_ref[...] + mod_ref[2:3, :] * _rms(mix, gpost_ref[...])
    x1_ref[...] = x1
    h2 = _rms(x1, gffn_ref[...]) * (1.0 + mod_ref[4:5, :]) + mod_ref[3:4, :]
    h2_ref[...] = h2

    lane = lax.broadcasted_iota(jnp.int32, (tm, LANES), 1).astype(F32)
    logits = jnp.dot(h2.astype(BF16), wr_ref[...], preferred_element_type=F32) + br_ref[...]
    logits = jnp.where(lane < N_EXPERTS, logits, NEG_INF)
    sel = jnp.zeros((tm, LANES), F32)
    ids, vals, hots = [], [], []
    for _ in range(TOP_K):
        mk = jnp.max(logits, axis=1, keepdims=True)
        idx = jnp.min(jnp.where(logits == mk, lane, float(LANES)), axis=1, keepdims=True)
        hot = lane == idx
        logits = jnp.where(hot, NEG_INF, logits)
        sel = sel + hot.astype(F32)
        ids.append(idx)
        vals.append(mk)
        hots.append(hot)
    ex = [jnp.exp(vk - vals[0]) for vk in vals]
    denom = ex[0] + ex[1] + ex[2] + ex[3]

    before = jnp.dot(ltri[...], sel.astype(BF16), preferred_element_type=F32) + carry[0:1, :]
    ri = jnp.zeros((tm, LANES), F32)
    rg = jnp.zeros((tm, LANES), F32)
    for kk in range(TOP_K):
        rank = jnp.sum(jnp.where(hots[kk], before, 0.0), axis=1, keepdims=True)
        ri = jnp.where(lane == kk, ids[kk], ri)
        ri = jnp.where(lane == TOP_K + kk, rank, ri)
        rg = jnp.where(lane == kk, ex[kk] / denom, rg)
    ri_ref[...] = ri.astype(jnp.int32)
    rg_ref[...] = rg
    carry[0:1, :] = carry[0:1, :] + jnp.sum(sel, axis=0, keepdims=True)
    cnt_ref[...] = carry[...]


def _postmix(attn, pooled, x, mod, cnt0, w, *, tm):
    nb, nt, _ = x.shape
    tok = lambda width: pl.BlockSpec((None, tm, width), lambda b, t: (b, t, 0))
    full = lambda a: pl.BlockSpec(a.shape, lambda b, t: (0,) * a.ndim)
    outs = [
        jax.ShapeDtypeStruct((nb, nt, D_MODEL), F32),
        jax.ShapeDtypeStruct((nb, nt, D_MODEL), F32),
        jax.ShapeDtypeStruct((nb, nt, LANES), jnp.int32),
        jax.ShapeDtypeStruct((nb, nt, LANES), F32),
        jax.ShapeDtypeStruct((8, LANES), F32),
    ]
    return pl.pallas_call(
        functools.partial(_postmix_kernel, tm=tm),
        out_shape=outs,
        grid=(nb, nt // tm),
        in_specs=[tok(HEAD_SLABS), tok(POOL_WIDTH), tok(D_MODEL),
                  pl.BlockSpec((None, N_MOD, D_MODEL), lambda b, t: (b, 0, 0)),
                  full(cnt0), full(w["g_post_mix"]), full(w["g_pre_ffn"]),
                  full(w["w_o_attn"]), full(w["w_o_pool"]), full(w["w_router"]), full(w["b_router"])],
        out_specs=[tok(D_MODEL), tok(D_MODEL), tok(LANES), tok(LANES),
                   pl.BlockSpec((8, LANES), lambda b, t: (0, 0))],
        scratch_shapes=[pltpu.VMEM((8, LANES), F32), pltpu.VMEM((tm, tm), BF16)],
        compiler_params=_cparams(2),
        name="postmix",
    )(attn, pooled, x, mod, cnt0, w["g_post_mix"], w["g_pre_ffn"],
      w["w_o_attn"], w["w_o_pool"], w["w_router"], w["b_router"])


def _dispatch_kernel(dest_ref, h_ref, xs_in_ref, xs_ref, sem, *, tm):
    del xs_in_ref

    def row_copy(i, d):
        return pltpu.make_async_copy(h_ref.at[pl.ds(i, 1), :], xs_ref.at[pl.ds(d, 1), :], sem)

    def issue(i, carry):
        for kk in range(TOP_K):
            row_copy(i, dest_ref[0, 0, i * TOP_K + kk]).start()
        return carry

    lax.fori_loop(0, tm, issue, 0)

    def drain(i, carry):
        for kk in range(TOP_K):
            row_copy(i, 0).wait()
        return carry

    lax.fori_loop(0, tm, drain, 0)


def _dispatch(dest, h, xs, *, tm):
    n = h.shape[0]
    nt = n // tm
    return pl.pallas_call(
        functools.partial(_dispatch_kernel, tm=tm),
        out_shape=jax.ShapeDtypeStruct(xs.shape, xs.dtype),
        grid=(nt,),
        in_specs=[pl.BlockSpec((1, 1, tm * TOP_K), lambda i: (i, 0, 0), memory_space=pltpu.SMEM),
                  pl.BlockSpec((tm, D_MODEL), lambda i: (i, 0)),
                  pl.BlockSpec(memory_space=pl.ANY)],
        out_specs=pl.BlockSpec(memory_space=pl.ANY),
        scratch_shapes=[pltpu.SemaphoreType.DMA(())],
        input_output_aliases={2: 0},
        compiler_params=_cparams(1),
        name="dispatch",
    )(dest.reshape(nt, 1, tm * TOP_K), h, xs)


def _expert_kernel(be_ref, nu_ref, xs_ref, wgu_ref, bgu_ref, wd_ref, bd_ref, ys_ref):
    del be_ref

    @pl.when(pl.program_id(0) < nu_ref[0])
    def _():
        x = xs_ref[...].astype(BF16)
        gu = jnp.dot(x, wgu_ref[...], preferred_element_type=F32) + bgu_ref[...]
        g = jnp.minimum(gu[:, :D_FF], SWIGLU_LIMIT)
        u = jnp.clip(gu[:, D_FF:], -SWIGLU_LIMIT, SWIGLU_LIMIT)
        a = (u + 1.0) * (g * jax.nn.sigmoid(SWIGLU_ALPHA * g))
        ys_ref[...] = jnp.dot(a.astype(BF16), wd_ref[...], preferred_element_type=F32) + bd_ref[...]


def _experts(block_e, n_used, xs, w):
    nblk = xs.shape[0] // MOE_BLOCK
    row = lambda i, be, nu: (jnp.minimum(i, nu[0] - 1), 0)
    per_e = lambda i, be, nu: (be[i], 0, 0)
    return pl.pallas_call(
        _expert_kernel,
        out_shape=jax.ShapeDtypeStruct(xs.shape, F32),
        grid_spec=pltpu.PrefetchScalarGridSpec(
            num_scalar_prefetch=2,
            grid=(nblk,),
            in_specs=[pl.BlockSpec((MOE_BLOCK, D_MODEL), row),
                      pl.BlockSpec((None, D_MODEL, 2 * D_FF), per_e),
                      pl.BlockSpec((None, 1, 2 * D_FF), per_e),
                      pl.BlockSpec((None, D_FF, D_MODEL), per_e),
                      pl.BlockSpec((None, 1, D_MODEL), per_e)],
            out_specs=pl.BlockSpec((MOE_BLOCK, D_MODEL), row),
        ),
        compiler_params=_cparams(1),
        name="experts",
    )(block_e, n_used, xs, w["w_gu"], w["b_gu"], w["w_down"], w["b_down"])


def _combine_kernel(dest_ref, x1_ref, rg_ref, mod_ref, gpost_ref, ys_ref, o_ref, gbuf, sem, *, tm):
    def row_copy(i, kk, d):
        return pltpu.make_async_copy(ys_ref.at[pl.ds(d, 1), :], gbuf.at[kk, pl.ds(i, 1), :], sem)

    def issue(i, carry):
        for kk in range(TOP_K):
            row_copy(i, kk, dest_ref[0, 0, i * TOP_K + kk]).start()
        return carry

    lax.fori_loop(0, tm, issue, 0)

    def drain(i, carry):
        for kk in range(TOP_K):
            row_copy(i, kk, 0).wait()
        return carry

    lax.fori_loop(0, tm, drain, 0)

    rg = rg_ref[...]
    y = rg[:, 0:1] * gbuf[0]
    for kk in range(1, TOP_K):
        y = y + rg[:, kk:kk + 1] * gbuf[kk]
    o_ref[...] = x1_ref[...] + mod_ref[5:6, :] * _rms(y, gpost_ref[...])


def _combine(dest, x1, rg, mod, g_post, ys, *, tm):
    nb, nt, _ = x1.shape
    ntile = nt // tm
    tok = lambda width: pl.BlockSpec((None, tm, width), lambda b, t: (b, t, 0))
    return pl.pallas_call(
        functools.partial(_combine_kernel, tm=tm),
        out_shape=jax.ShapeDtypeStruct(x1.shape, F32),
        grid=(nb, ntile),
        in_specs=[pl.BlockSpec((1, 1, tm * TOP_K), lambda b, t: (b * ntile + t, 0, 0), memory_space=pltpu.SMEM),
                  tok(D_MODEL), tok(LANES),
                  pl.BlockSpec((None, N_MOD, D_MODEL), lambda b, t: (b, 0, 0)),
                  pl.BlockSpec(g_post.shape, lambda b, t: (0, 0)),
                  pl.BlockSpec(memory_space=pl.ANY)],
        out_specs=tok(D_MODEL),
        scratch_shapes=[pltpu.VMEM((TOP_K, tm, D_MODEL), F32), pltpu.SemaphoreType.DMA(())],
        compiler_params=_cparams(2),
        name="combine",
    )(dest.reshape(nb * ntile, 1, tm * TOP_K), x1, rg, mod, g_post, ys)


def _rot_swap(w):
    half = QK_ROPE // 2
    return jnp.concatenate([-w[..., half:], w[..., :half]], axis=-1)


def _prep_weights(w_in, g_q_a, w_q_b, g_kv_a, w_uk, w_uv, w_pool, pool_scale, w_o, g_pre_mix, g_post_mix,
                  g_pre_ffn, w_router, b_router, w_gu, b_gu, w_down, b_down, g_post_ffn):
    row = lambda a: a.reshape(1, -1).astype(F32)
    w_kpe = w_in[:, U_OFF:U_OFF + QK_ROPE]
    zeros = lambda *s: jnp.zeros(s, F32)
    d = D_MODEL
    slab = lambda a: jnp.concatenate([zeros(d, QK_NOPE), a, zeros(d, LANES - QK_HEAD)], axis=1)
    w_in_ext = jnp.concatenate([w_in[:, :U_OFF], w_in[:, U_OFF + QK_ROPE:], slab(w_kpe), slab(_rot_swap(w_kpe))],
                               axis=1)
    pad_q = zeros(Q_LORA, N_HEADS, LANES - QK_HEAD)
    wq_plain = jnp.concatenate([w_q_b, pad_q], axis=2).reshape(Q_LORA, HEAD_SLABS)
    wq_swap = jnp.concatenate([zeros(Q_LORA, N_HEADS, QK_NOPE), _rot_swap(w_q_b[..., QK_NOPE:]), pad_q],
                              axis=2).reshape(Q_LORA, HEAD_SLABS)
    pad_kv = zeros(KV_LORA, N_HEADS, LANES - QK_NOPE)
    wk = jnp.concatenate([w_uk, pad_kv], axis=2).reshape(KV_LORA, HEAD_SLABS)
    wv = jnp.concatenate([w_uv, pad_kv], axis=2).reshape(KV_LORA, HEAD_SLABS)
    mla_w = N_HEADS * V_HEAD
    woa = jnp.concatenate([w_o[:mla_w].reshape(N_HEADS, V_HEAD, d), zeros(N_HEADS, LANES - V_HEAD, d)],
                          axis=1).reshape(HEAD_SLABS, d)
    return {
        "g_pre_mix": row(g_pre_mix), "g_q_a": row(g_q_a), "g_kv_a": row(g_kv_a), "pool_scale": row(pool_scale),
        "w_in": w_in_ext.astype(BF16),
        "w_q": jnp.concatenate([wq_plain, wq_swap], axis=1).astype(BF16),
        "w_kv": jnp.concatenate([wk, wv], axis=1).astype(BF16),
        "w_pool": w_pool.astype(BF16),
        "g_post_mix": row(g_post_mix), "g_pre_ffn": row(g_pre_ffn), "g_post_ffn": row(g_post_ffn),
        "w_o_attn": woa.astype(BF16), "w_o_pool": w_o[mla_w:].astype(BF16),
        "w_router": jnp.pad(w_router, ((0, 0), (0, LANES - N_EXPERTS))).astype(BF16),
        "b_router": jnp.pad(b_router, (0, LANES - N_EXPERTS)).reshape(1, LANES).astype(F32),
        "w_gu": w_gu.astype(BF16), "b_gu": b_gu.reshape(N_EXPERTS, 1, 2 * D_FF).astype(F32),
        "w_down": w_down.astype(BF16), "b_down": b_down.reshape(N_EXPERTS, 1, D_MODEL).astype(F32),
    }


def _rope_tables(pos):
    half = QK_ROPE // 2
    inv = ROPE_THETA ** (-jnp.arange(half, dtype=F32) / half)
    ang = pos.astype(F32)[:, None] * inv[None, :]
    cos, sin = jnp.cos(ang), jnp.sin(ang)
    n = pos.shape[0]
    cos_t = jnp.concatenate([jnp.ones((n, QK_NOPE), F32), cos, cos, jnp.zeros((n, LANES - QK_HEAD), F32)], axis=1)
    sin_t = jnp.concatenate([jnp.zeros((n, QK_NOPE), F32), sin, sin, jnp.zeros((n, LANES - QK_HEAD), F32)], axis=1)
    return cos_t, sin_t


def _tile(n, pref):
    return pref if n % pref == 0 else n


def _mixer_path(x, mod, pos0, hist, cache, w, cnt0):
    nb, nt, _ = x.shape
    tm = _tile(nt, 512)
    cos_t, sin_t = _rope_tables(pos0 + jnp.arange(nt, dtype=jnp.int32))
    q, k, v, kv_new, kslab, pooled, pool_tail = _premix(x, mod, cos_t, sin_t, hist, w, tm=tm, pos0=pos0)
    if cache is None:
        attn = _attention(q, k, v, tq=tm, tk=tm, causal=True, kv_len=nt)
    else:
        ckv, ckpe = cache
        past = ckv.shape[1]
        ckpe_slab = jnp.pad(ckpe, ((0, 0), (0, 0), (QK_NOPE, LANES - QK_HEAD)))
        kc, vc = _kvproj(ckv, ckpe_slab, w["w_kv"], tm=_tile(past, 512))
        kv_len = past + nt
        tk = -(-kv_len // 256) * 256
        padk = lambda a, c: jnp.concatenate([c, a, jnp.zeros((nb, tk - kv_len, HEAD_SLABS), BF16)], axis=1)
        attn = _attention(q, padk(k, kc), padk(v, vc), tq=tm, tk=tk, causal=False, kv_len=kv_len)
    x1, h2, ri, rg, cnt = _postmix(attn, pooled, x, mod, cnt0, w, tm=tm)
    return x1, h2, ri, rg, cnt, kv_new, kslab[..., QK_NOPE:QK_HEAD], pool_tail[:, 1:]


def kernel(x_prompt, x_sample, c_prompt, c_sample, cache_kv_latent, cache_k_rope, state_pool, w_ada, b_ada,
           g_pre_mix, w_in, g_q_a, w_q_b, g_kv_a, w_uk, w_uv, w_pool, pool_scale, w_o, g_post_mix, g_pre_ffn,
           w_router, b_router, w_gu, b_gu, w_down, b_down, g_post_ffn):
    assert w_ada.shape[0] == 1, "single-layer step"
    bp, sp, _ = x_prompt.shape
    bs, ss, _ = x_sample.shape
    past = cache_kv_latent.shape[2]
    w = _prep_weights(w_in[0], g_q_a[0], w_q_b[0], g_kv_a[0], w_uk[0], w_uv[0], w_pool[0], pool_scale[0], w_o[0],
                      g_pre_mix[0], g_post_mix[0], g_pre_ffn[0], w_router[0], b_router[0], w_gu[0], b_gu[0],
                      w_down[0], b_down[0], g_post_ffn[0])

    mod = _ada(jnp.concatenate([c_prompt, c_sample], axis=0), w_ada[0], b_ada[0])
    mod = mod.reshape(bp + bs, N_MOD, D_MODEL)
    mod_p, mod_s = mod[:bp], mod[bp:]

    hist_p = jnp.zeros((bp, HIST_ROWS, POOL_WIDTH), F32)
    hist_s = jnp.pad(state_pool[0], ((0, 0), (1, 0), (0, 0)))
    cnt0 = jnp.zeros((8, LANES), F32)
    x1p, h2p, rip, rgp, cntp, kv_p, kpe_p, pool_p = _mixer_path(x_prompt, mod_p, 0, hist_p, None, w, cnt0)
    x1s, h2s, ris, rgs, cnts, kv_s, kpe_s, pool_s = _mixer_path(
        x_sample, mod_s, past, hist_s, (cache_kv_latent[0], cache_k_rope[0]), w, cntp)

    counts = cnts[0, :N_EXPERTS].astype(jnp.int32)
    padded = (counts + MOE_BLOCK - 1) // MOE_BLOCK * MOE_BLOCK
    pad_end = jnp.cumsum(padded)
    pad_start = pad_end - padded
    n_tok = bp * sp + bs * ss
    n_blocks = -(-(n_tok * TOP_K) // MOE_BLOCK) + N_EXPERTS
    block_e = jnp.minimum(jnp.searchsorted(pad_end, jnp.arange(n_blocks, dtype=jnp.int32) * MOE_BLOCK, side="right"),
                          N_EXPERTS - 1).astype(jnp.int32)
    n_used = (pad_end[-1:] // MOE_BLOCK).astype(jnp.int32)

    def dest_of(ri):
        ids = ri[..., :TOP_K]
        hot = ids[..., None] == jnp.arange(N_EXPERTS, dtype=jnp.int32)
        return ri[..., TOP_K:2 * TOP_K] + jnp.sum(jnp.where(hot, pad_start, 0), axis=-1)

    dest_p = dest_of(rip)
    dest_s = dest_of(ris)
    xs = jnp.zeros((n_blocks * MOE_BLOCK, D_MODEL), F32)
    xs = _dispatch(dest_p, h2p.reshape(bp * sp, D_MODEL), xs, tm=512)
    xs = _dispatch(dest_s, h2s.reshape(bs * ss, D_MODEL), xs, tm=_tile(bs * ss, 512))
    ys = _experts(block_e, n_used, xs, w)
    y_p = _combine(dest_p, x1p, rgp, mod_p, w["g_post_ffn"], ys, tm=_tile(sp, 256))
    y_s = _combine(dest_s, x1s, rgs, mod_s, w["g_post_ffn"], ys, tm=_tile(ss, 256))
    return (y_p, y_s, kv_p[None], kpe_p[None], pool_p[None], kv_s[None], kpe_s[None], pool_s[None])
```

```python
import functools

import jax
import jax.numpy as jnp
from jax import lax
from jax.experimental import pallas as pl
from jax.experimental.pallas import tpu as pltpu

F32 = jnp.float32
BF16 = jnp.bfloat16

D_MODEL = 1024
CHUNK = 64
N_HEADS = 8
QK_NOPE = 64
QK_ROPE = 32
QK_HEAD = QK_NOPE + QK_ROPE
V_HEAD = 64
Q_LORA = 384
KV_LORA = 256
ROPE_THETA = 10000.0
POOL_WINDOWS = (2, 4, 8, 16)
POOL_GROUP_DIM = 128
POOL_WIDTH = POOL_GROUP_DIM * len(POOL_WINDOWS)
POOL_HIST = max(POOL_WINDOWS) - 1
HIST_ROWS = POOL_HIST + 1
N_EXPERTS = 32
TOP_K = 4
D_FF = 1024
SWIGLU_LIMIT = 7.0
SWIGLU_ALPHA = 1.702
N_MOD = 6
EPS = 1e-6

LANES = 128
HEAD_SLABS = N_HEADS * LANES
ONE_LANE = V_HEAD
IN_EXT = Q_LORA + KV_LORA + POOL_WIDTH + 2 * LANES
U_OFF = Q_LORA + KV_LORA
KPE_OFF = U_OFF + POOL_WIDTH
SM_SCALE = QK_HEAD ** -0.5
LOG2_E = 1.4426950408889634
Q_SCALE = SM_SCALE * LOG2_E
NEG_INF = float("-inf")

MOE_BLOCK = 512
VMEM_LIMIT = 56 * 1024 * 1024


def _cparams(n_axes, vmem=VMEM_LIMIT):
    return pltpu.CompilerParams(dimension_semantics=("arbitrary",) * n_axes, vmem_limit_bytes=vmem)


def _rms(x, g):
    return x * lax.rsqrt(jnp.mean(x * x, axis=-1, keepdims=True) + EPS) * g


def _ada_kernel(c_ref, w_ref, b_ref, o_ref):
    c = c_ref[...]
    s = (c * jax.nn.sigmoid(c)).astype(BF16)
    o_ref[...] = jnp.dot(s, w_ref[...].astype(BF16), preferred_element_type=F32) + b_ref[...]


def _ada(c, w_ada, b_ada):
    nb = c.shape[0]
    return pl.pallas_call(
        _ada_kernel,
        out_shape=jax.ShapeDtypeStruct((nb, N_MOD * D_MODEL), F32),
        grid=(N_MOD,),
        in_specs=[pl.BlockSpec((nb, D_MODEL), lambda j: (0, 0)),
                  pl.BlockSpec((D_MODEL, D_MODEL), lambda j: (0, j)),
                  pl.BlockSpec((1, D_MODEL), lambda j: (0, j))],
        out_specs=pl.BlockSpec((nb, D_MODEL), lambda j: (0, j)),
        compiler_params=_cparams(1),
        name="ada",
    )(c, w_ada, b_ada.reshape(1, -1))


def _store_kv(kvb, kslab, wk_ref, wvt_ref, k_ref, vt_ref):
    kk = jnp.dot(kvb, wk_ref[...], preferred_element_type=F32)
    for hd in range(N_HEADS):
        sl = slice(hd * LANES, (hd + 1) * LANES)
        k_ref[:, sl] = (kk[:, sl] + kslab).astype(BF16)
    vt = lax.dot_general(wvt_ref[...], kvb, (((1,), (1,)), ((), ())), preferred_element_type=F32)
    row = lax.broadcasted_iota(jnp.int32, (HEAD_SLABS, 1), 0)
    vt_ref[...] = (vt + (row % LANES == ONE_LANE).astype(F32)).astype(BF16)


def _premix_kernel(x_ref, mod_ref, cos_ref, sin_ref, hist_ref, gpre_ref, gqa_ref, gkv_ref, pscale_ref,
                   win_ref, wq_ref, wk_ref, wvt_ref, wpool_ref,
                   q_ref, k_ref, vt_ref, kv_ref, kpe_ref, pooled_ref, poolnew_ref,
                   ubuf, *, tm, pos0):
    t = pl.program_id(1)
    x = x_ref[...]
    h = (_rms(x, gpre_ref[...]) * (1.0 + mod_ref[1:2, :]) + mod_ref[0:1, :]).astype(BF16)
    z = jnp.dot(h, win_ref[...], preferred_element_type=F32)
    cosv = cos_ref[...]
    sinv = sin_ref[...]

    qan = _rms(z[:, :Q_LORA], gqa_ref[...]).astype(BF16)
    qq = jnp.dot(qan, wq_ref[...], preferred_element_type=F32)
    for hd in range(N_HEADS):
        a = qq[:, hd * LANES:(hd + 1) * LANES]
        b = qq[:, HEAD_SLABS + hd * LANES:HEAD_SLABS + (hd + 1) * LANES]
        q_ref[:, hd * LANES:(hd + 1) * LANES] = ((a * cosv + b * sinv) * Q_SCALE).astype(BF16)

    kvn = _rms(z[:, Q_LORA:U_OFF], gkv_ref[...])
    kv_ref[...] = kvn
    kslab = z[:, KPE_OFF:KPE_OFF + LANES] * cosv + z[:, KPE_OFF + LANES:KPE_OFF + 2 * LANES] * sinv
    kpe_ref[...] = kslab
    _store_kv(kvn.astype(BF16), kslab, wk_ref, wvt_ref, k_ref, vt_ref)

    @pl.when(t == 0)
    def _():
        ubuf[0:HIST_ROWS, :] = hist_ref[...]

    @pl.when(t > 0)
    def _():
        ubuf[0:HIST_ROWS, :] = ubuf[tm:tm + HIST_ROWS, :]

    ubuf[HIST_ROWS:HIST_ROWS + tm, :] = z[:, U_OFF:KPE_OFF]
    pos = pos0 + t * tm + lax.broadcasted_iota(jnp.int32, (tm, 1), 0)
    for g, w in enumerate(POOL_WINDOWS):
        sl = slice(g * POOL_GROUP_DIM, (g + 1) * POOL_GROUP_DIM)
        u = ubuf[HIST_ROWS:HIST_ROWS + tm, sl]
        acc = u
        for j in range(1, w):
            acc = acc + ubuf[HIST_ROWS - j:HIST_ROWS - j + tm, sl]
        cnt = jnp.minimum(pos + 1, w).astype(F32)
        d = (acc / cnt - u).astype(BF16)
        y = jnp.dot(d, wpool_ref[g], preferred_element_type=F32) * pscale_ref[:, sl]
        pooled_ref[:, sl] = y.astype(BF16)
    poolnew_ref[...] = ubuf[tm:tm + HIST_ROWS, :]


def _premix(x, mod, cos_t, sin_t, hist, w, *, tm, pos0):
    nb, nt, _ = x.shape
    grid = (nb, nt // tm)
    tok = lambda width: pl.BlockSpec((None, tm, width), lambda b, t: (b, t, 0))
    full = lambda a: pl.BlockSpec(a.shape, lambda b, t: (0,) * a.ndim)
    tab = pl.BlockSpec((tm, LANES), lambda b, t: (t, 0))
    perb = lambda rows, width: pl.BlockSpec((None, rows, width), lambda b, t: (b, 0, 0))
    outs = [
        jax.ShapeDtypeStruct((nb, nt, HEAD_SLABS), BF16),
        jax.ShapeDtypeStruct((nb, nt, HEAD_SLABS), BF16),
        jax.ShapeDtypeStruct((nb, HEAD_SLABS, nt), BF16),
        jax.ShapeDtypeStruct((nb, nt, KV_LORA), F32),
        jax.ShapeDtypeStruct((nb, nt, LANES), F32),
        jax.ShapeDtypeStruct((nb, nt, POOL_WIDTH), BF16),
        jax.ShapeDtypeStruct((nb, HIST_ROWS, POOL_WIDTH), F32),
    ]
    return pl.pallas_call(
        functools.partial(_premix_kernel, tm=tm, pos0=pos0),
        out_shape=outs,
        grid=grid,
        in_specs=[tok(D_MODEL), perb(N_MOD, D_MODEL), tab, tab, perb(HIST_ROWS, POOL_WIDTH),
                  full(w["g_pre_mix"]), full(w["g_q_a"]), full(w["g_kv_a"]), full(w["pool_scale"]),
                  full(w["w_in"]), full(w["w_q"]), full(w["w_k"]), full(w["w_vt"]), full(w["w_pool"])],
        out_specs=[tok(HEAD_SLABS), tok(HEAD_SLABS),
                   pl.BlockSpec((None, HEAD_SLABS, tm), lambda b, t: (b, 0, t)),
                   tok(KV_LORA), tok(LANES), tok(POOL_WIDTH), perb(HIST_ROWS, POOL_WIDTH)],
        scratch_shapes=[pltpu.VMEM((HIST_ROWS + tm, POOL_WIDTH), F32)],
        compiler_params=_cparams(2),
        name="premix",
    )(x, mod, cos_t, sin_t, hist, w["g_pre_mix"], w["g_q_a"], w["g_kv_a"], w["pool_scale"],
      w["w_in"], w["w_q"], w["w_k"], w["w_vt"], w["w_pool"])


def _kvproj_kernel(kv_ref, kpe_ref, wk_ref, wvt_ref, k_ref, vt_ref):
    _store_kv(kv_ref[...].astype(BF16), kpe_ref[...], wk_ref, wvt_ref, k_ref, vt_ref)


def _kvproj(kv, kpe_slab, w_k, w_vt, *, tm):
    nb, nt, _ = kv.shape
    tok = lambda width: pl.BlockSpec((None, tm, width), lambda b, t: (b, t, 0))
    full = lambda a: pl.BlockSpec(a.shape, lambda b, t: (0,) * a.ndim)
    return pl.pallas_call(
        _kvproj_kernel,
        out_shape=[jax.ShapeDtypeStruct((nb, nt, HEAD_SLABS), BF16),
                   jax.ShapeDtypeStruct((nb, HEAD_SLABS, nt), BF16)],
        grid=(nb, nt // tm),
        in_specs=[tok(KV_LORA), tok(LANES), full(w_k), full(w_vt)],
        out_specs=[tok(HEAD_SLABS), pl.BlockSpec((None, HEAD_SLABS, tm), lambda b, t: (b, 0, t))],
        compiler_params=_cparams(2),
        name="kvproj",
    )(kv, kpe_slab, w_k, w_vt)


def _attn_kernel(q_ref, k_ref, vt_ref, o_ref, m_ref, acc_ref, *, tq, tk, nk, causal, kv_len):
    qi = pl.program_id(1)
    ki = pl.program_id(2)
    last = ((qi + 1) * tq - 1) // tk if causal else nk - 1
    need_len_mask = kv_len < nk * tk

    @pl.when(ki == 0)
    def _():
        m_ref[...] = jnp.full(m_ref.shape, NEG_INF, F32)
        acc_ref[...] = jnp.zeros(acc_ref.shape, F32)

    def step(masked):
        if masked:
            kpos = ki * tk + lax.broadcasted_iota(jnp.int32, (tk, tq), 0)
            qpos = qi * tq + lax.broadcasted_iota(jnp.int32, (tk, tq), 1)
            vis = None
            if causal:
                vis = (kpos // CHUNK) <= (qpos // CHUNK)
            if need_len_mask:
                lm = kpos < kv_len
                vis = lm if vis is None else (vis & lm)
        def scores(hd):
            sl = slice(hd * LANES, (hd + 1) * LANES)
            return lax.dot_general(k_ref[:, sl], q_ref[:, sl], (((1,), (1,)), ((), ())),
                                   preferred_element_type=F32)

        s_next = scores(0)
        for hd in range(N_HEADS):
            sl = slice(hd * LANES, (hd + 1) * LANES)
            s = s_next
            if hd + 1 < N_HEADS:
                s_next = scores(hd + 1)
            if masked:
                s = jnp.where(vis, s, NEG_INF)
            m_prev = m_ref[hd:hd + 1, :]
            m_new = jnp.maximum(m_prev, jnp.max(s, axis=0, keepdims=True))
            alpha = jnp.exp2(m_prev - m_new)
            p = jnp.exp2(s - m_new).astype(BF16)
            acc_ref[hd] = alpha * acc_ref[hd] + jnp.dot(vt_ref[sl, :], p, preferred_element_type=F32)
            m_ref[hd:hd + 1, :] = m_new

    if causal or need_len_mask:
        @pl.when(ki < last)
        def _():
            step(False)

        @pl.when(ki == last)
        def _():
            step(True)
    else:
        step(False)

    @pl.when(ki == nk - 1)
    def _():
        for hd in range(N_HEADS):
            acc = acc_ref[hd]
            out_t = acc / acc[ONE_LANE:ONE_LANE + 1, :]
            o_ref[:, hd * LANES:(hd + 1) * LANES] = out_t.T.astype(BF16)


def _attention(q, k, vt, *, tq, tk, causal, kv_len):
    nb, nq_tot, _ = q.shape
    nk = k.shape[1] // tk
    nq = nq_tot // tq
    if causal:
        last = lambda i: ((i + 1) * tq - 1) // tk
        kmap = lambda b, i, j: (b, jnp.minimum(j, last(i)), 0)
        vmap = lambda b, i, j: (b, 0, jnp.minimum(j, last(i)))
    else:
        kmap = lambda b, i, j: (b, j, 0)
        vmap = lambda b, i, j: (b, 0, j)
    return pl.pallas_call(
        functools.partial(_attn_kernel, tq=tq, tk=tk, nk=nk, causal=causal, kv_len=kv_len),
        out_shape=jax.ShapeDtypeStruct((nb, nq_tot, HEAD_SLABS), BF16),
        grid=(nb, nq, nk),
        in_specs=[pl.BlockSpec((None, tq, HEAD_SLABS), lambda b, i, j: (b, i, 0)),
                  pl.BlockSpec((None, tk, HEAD_SLABS), kmap),
                  pl.BlockSpec((None, HEAD_SLABS, tk), vmap)],
        out_specs=pl.BlockSpec((None, tq, HEAD_SLABS), lambda b, i, j: (b, i, 0)),
        scratch_shapes=[pltpu.VMEM((N_HEADS, tq), F32), pltpu.VMEM((N_HEADS, LANES, tq), F32)],
        compiler_params=_cparams(3),
        name="attention",
    )(q, k, vt)


def _postmix_kernel(attn_ref, pooled_ref, x_ref, mod_ref, cnt0_ref, gpost_ref, gffn_ref,
                    woa_ref, wop_ref, wr_ref, br_ref,
                    x1_ref, h2_ref, ri_ref, rg_ref, cnt_ref,
                    carry, ltri, *, tm):
    first = (pl.program_id(0) == 0) & (pl.program_id(1) == 0)

    @pl.when(first)
    def _():
        carry[...] = cnt0_ref[...]
        r = lax.broadcasted_iota(jnp.int32, (tm, tm), 0)
        c = lax.broadcasted_iota(jnp.int32, (tm, tm), 1)
        ltri[...] = (r > c).astype(BF16)

    mix = (jnp.dot(attn_ref[...], woa_ref[...], preferred_element_type=F32)
           + jnp.dot(pooled_ref[...], wop_ref[...], preferred_element_type=F32))
    x1 = x_ref[...] + mod_ref[2:3, :] * _rms(mix, gpost_ref[...])
    x1_ref[...] = x1
    h2 = _rms(x1, gffn_ref[...]) * (1.0 + mod_ref[4:5, :]) + mod_ref[3:4, :]
    h2_ref[...] = h2

    lane = lax.broadcasted_iota(jnp.int32, (tm, LANES), 1).astype(F32)
    logits = jnp.dot(h2.astype(BF16), wr_ref[...], preferred_element_type=F32) + br_ref[...]
    logits = jnp.where(lane < N_EXPERTS, logits, NEG_INF)
    sel = jnp.zeros((tm, LANES), F32)
    ids, vals, hots = [], [], []
    for _ in range(TOP_K):
        mk = jnp.max(logits, axis=1, keepdims=True)
        idx = jnp.min(jnp.where(logits == mk, lane, float(LANES)), axis=1, keepdims=True)
        hot = lane == idx
        logits = jnp.where(hot, NEG_INF, logits)
        sel = sel + hot.astype(F32)
        ids.append(idx)
        vals.append(mk)
        hots.append(hot)
    ex = [jnp.exp(vk - vals[0]) for vk in vals]
    denom = ex[0] + ex[1] + ex[2] + ex[3]

    before = jnp.dot(ltri[...], sel.astype(BF16), preferred_element_type=F32) + carry[0:1, :]
    ri = jnp.zeros((tm, LANES), F32)
    rg = jnp.zeros((tm, LANES), F32)
    for kk in range(TOP_K):
        rank = jnp.sum(jnp.where(hots[kk], before, 0.0), axis=1, keepdims=True)
        ri = jnp.where(lane == kk, ids[kk], ri)
        ri = jnp.where(lane == TOP_K + kk, rank, ri)
        rg = jnp.where(lane == kk, ex[kk] / denom, rg)
    ri_ref[...] = ri.astype(jnp.int32)
    rg_ref[...] = rg
    carry[0:1, :] = carry[0:1, :] + jnp.sum(sel, axis=0, keepdims=True)
    cnt_ref[...] = carry[...]


def _postmix(attn, pooled, x, mod, cnt0, w, *, tm):
    nb, nt, _ = x.shape
    tok = lambda width: pl.BlockSpec((None, tm, width), lambda b, t: (b, t, 0))
    full = lambda a: pl.BlockSpec(a.shape, lambda b, t: (0,) * a.ndim)
    outs = [
        jax.ShapeDtypeStruct((nb, nt, D_MODEL), F32),
        jax.ShapeDtypeStruct((nb, nt, D_MODEL), F32),
        jax.ShapeDtypeStruct((nb, nt, LANES), jnp.int32),
        jax.ShapeDtypeStruct((nb, nt, LANES), F32),
        jax.ShapeDtypeStruct((8, LANES), F32),
    ]
    return pl.pallas_call(
        functools.partial(_postmix_kernel, tm=tm),
        out_shape=outs,
        grid=(nb, nt // tm),
        in_specs=[tok(HEAD_SLABS), tok(POOL_WIDTH), tok(D_MODEL),
                  pl.BlockSpec((None, N_MOD, D_MODEL), lambda b, t: (b, 0, 0)),
                  full(cnt0), full(w["g_post_mix"]), full(w["g_pre_ffn"]),
                  full(w["w_o_attn"]), full(w["w_o_pool"]), full(w["w_router"]), full(w["b_router"])],
        out_specs=[tok(D_MODEL), tok(D_MODEL), tok(LANES), tok(LANES),
                   pl.BlockSpec((8, LANES), lambda b, t: (0, 0))],
        scratch_shapes=[pltpu.VMEM((8, LANES), F32), pltpu.VMEM((tm, tm), BF16)],
        compiler_params=_cparams(2),
        name="postmix",
    )(attn, pooled, x, mod, cnt0, w["g_post_mix"], w["g_pre_ffn"],
      w["w_o_attn"], w["w_o_pool"], w["w_router"], w["b_router"])


def _dispatch_kernel(dest_ref, h_ref, xs_in_ref, xs_ref, sem, *, tm):
    del xs_in_ref

    def row_copy(i, d):
        return pltpu.make_async_copy(h_ref.at[pl.ds(i, 1), :], xs_ref.at[pl.ds(d, 1), :], sem)

    def issue(i, carry):
        for kk in range(TOP_K):
            row_copy(i, dest_ref[0, 0, i * TOP_K + kk]).start()
        return carry

    lax.fori_loop(0, tm, issue, 0)

    def drain(i, carry):
        for kk in range(TOP_K):
            row_copy(i, 0).wait()
        return carry

    lax.fori_loop(0, tm, drain, 0)


def _dispatch(dest, h, xs, *, tm):
    n = h.shape[0]
    nt = n // tm
    return pl.pallas_call(
        functools.partial(_dispatch_kernel, tm=tm),
        out_shape=jax.ShapeDtypeStruct(xs.shape, xs.dtype),
        grid=(nt,),
        in_specs=[pl.BlockSpec((1, 1, tm * TOP_K), lambda i: (i, 0, 0), memory_space=pltpu.SMEM),
                  pl.BlockSpec((tm, D_MODEL), lambda i: (i, 0)),
                  pl.BlockSpec(memory_space=pl.ANY)],
        out_specs=pl.BlockSpec(memory_space=pl.ANY),
        scratch_shapes=[pltpu.SemaphoreType.DMA(())],
        input_output_aliases={2: 0},
        compiler_params=_cparams(1),
        name="dispatch",
    )(dest.reshape(nt, 1, tm * TOP_K), h, xs)


def _expert_kernel(be_ref, nu_ref, xs_ref, wgu_ref, bgu_ref, wd_ref, bd_ref, ys_ref):
    del be_ref

    @pl.when(pl.program_id(0) < nu_ref[0])
    def _():
        x = xs_ref[...].astype(BF16)
        gu = jnp.dot(x, wgu_ref[...], preferred_element_type=F32) + bgu_ref[...]
        g = jnp.minimum(gu[:, :D_FF], SWIGLU_LIMIT)
        u = jnp.clip(gu[:, D_FF:], -SWIGLU_LIMIT, SWIGLU_LIMIT)
        a = (u + 1.0) * (g * jax.nn.sigmoid(SWIGLU_ALPHA * g))
        ys_ref[...] = jnp.dot(a.astype(BF16), wd_ref[...], preferred_element_type=F32) + bd_ref[...]


def _experts(block_e, n_used, xs, w):
    nblk = xs.shape[0] // MOE_BLOCK
    row = lambda i, be, nu: (jnp.minimum(i, nu[0] - 1), 0)
    per_e = lambda i, be, nu: (be[i], 0, 0)
    return pl.pallas_call(
        _expert_kernel,
        out_shape=jax.ShapeDtypeStruct(xs.shape, F32),
        grid_spec=pltpu.PrefetchScalarGridSpec(
            num_scalar_prefetch=2,
            grid=(nblk,),
            in_specs=[pl.BlockSpec((MOE_BLOCK, D_MODEL), row),
                      pl.BlockSpec((None, D_MODEL, 2 * D_FF), per_e),
                      pl.BlockSpec((None, 1, 2 * D_FF), per_e),
                      pl.BlockSpec((None, D_FF, D_MODEL), per_e),
                      pl.BlockSpec((None, 1, D_MODEL), per_e)],
            out_specs=pl.BlockSpec((MOE_BLOCK, D_MODEL), row),
        ),
        compiler_params=_cparams(1),
        name="experts",
    )(block_e, n_used, xs, w["w_gu"], w["b_gu"], w["w_down"], w["b_down"])


def _combine_kernel(dest_ref, x1_ref, rg_ref, mod_ref, gpost_ref, ys_ref, o_ref, gbuf, sem, *, tm):
    def row_copy(i, kk, d):
        return pltpu.make_async_copy(ys_ref.at[pl.ds(d, 1), :], gbuf.at[kk, pl.ds(i, 1), :], sem)

    def issue(i, carry):
        for kk in range(TOP_K):
            row_copy(i, kk, dest_ref[0, 0, i * TOP_K + kk]).start()
        return carry

    lax.fori_loop(0, tm, issue, 0)

    def drain(i, carry):
        for kk in range(TOP_K):
            row_copy(i, kk, 0).wait()
        return carry

    lax.fori_loop(0, tm, drain, 0)

    rg = rg_ref[...]
    y = rg[:, 0:1] * gbuf[0]
    for kk in range(1, TOP_K):
        y = y + rg[:, kk:kk + 1] * gbuf[kk]
    o_ref[...] = x1_ref[...] + mod_ref[5:6, :] * _rms(y, gpost_ref[...])


def _combine(dest, x1, rg, mod, g_post, ys, *, tm):
    nb, nt, _ = x1.shape
    ntile = nt // tm
    tok = lambda width: pl.BlockSpec((None, tm, width), lambda b, t: (b, t, 0))
    return pl.pallas_call(
        functools.partial(_combine_kernel, tm=tm),
        out_shape=jax.ShapeDtypeStruct(x1.shape, F32),
        grid=(nb, ntile),
        in_specs=[pl.BlockSpec((1, 1, tm * TOP_K), lambda b, t: (b * ntile + t, 0, 0), memory_space=pltpu.SMEM),
                  tok(D_MODEL), tok(LANES),
                  pl.BlockSpec((None, N_MOD, D_MODEL), lambda b, t: (b, 0, 0)),
                  pl.BlockSpec(g_post.shape, lambda b, t: (0, 0)),
                  pl.BlockSpec(memory_space=pl.ANY)],
        out_specs=tok(D_MODEL),
        scratch_shapes=[pltpu.VMEM((TOP_K, tm, D_MODEL), F32), pltpu.SemaphoreType.DMA(())],
        compiler_params=_cparams(2),
        name="combine",
    )(dest.reshape(nb * ntile, 1, tm * TOP_K), x1, rg, mod, g_post, ys)


def _rot_swap(w):
    half = QK_ROPE // 2
    return jnp.concatenate([-w[..., half:], w[..., :half]], axis=-1)


def _prep_weights(w_in, g_q_a, w_q_b, g_kv_a, w_uk, w_uv, w_pool, pool_scale, w_o, g_pre_mix, g_post_mix,
                  g_pre_ffn, w_router, b_router, w_gu, b_gu, w_down, b_down, g_post_ffn):
    row = lambda a: a.reshape(1, -1).astype(F32)
    w_kpe = w_in[:, U_OFF:U_OFF + QK_ROPE]
    zeros = lambda *s: jnp.zeros(s, F32)
    d = D_MODEL
    slab = lambda a: jnp.concatenate([zeros(d, QK_NOPE), a, zeros(d, LANES - QK_HEAD)], axis=1)
    w_in_ext = jnp.concatenate([w_in[:, :U_OFF], w_in[:, U_OFF + QK_ROPE:], slab(w_kpe), slab(_rot_swap(w_kpe))],
                               axis=1)
    pad_q = zeros(Q_LORA, N_HEADS, LANES - QK_HEAD)
    wq_plain = jnp.concatenate([w_q_b, pad_q], axis=2).reshape(Q_LORA, HEAD_SLABS)
    wq_swap = jnp.concatenate([zeros(Q_LORA, N_HEADS, QK_NOPE), _rot_swap(w_q_b[..., QK_NOPE:]), pad_q],
                              axis=2).reshape(Q_LORA, HEAD_SLABS)
    pad_kv = zeros(KV_LORA, N_HEADS, LANES - QK_NOPE)
    wk = jnp.concatenate([w_uk, pad_kv], axis=2).reshape(KV_LORA, HEAD_SLABS)
    wv = jnp.concatenate([w_uv, pad_kv], axis=2).reshape(KV_LORA, HEAD_SLABS)
    mla_w = N_HEADS * V_HEAD
    woa = jnp.concatenate([w_o[:mla_w].reshape(N_HEADS, V_HEAD, d), zeros(N_HEADS, LANES - V_HEAD, d)],
                          axis=1).reshape(HEAD_SLABS, d)
    return {
        "g_pre_mix": row(g_pre_mix), "g_q_a": row(g_q_a), "g_kv_a": row(g_kv_a), "pool_scale": row(pool_scale),
        "w_in": w_in_ext.astype(BF16),
        "w_q": jnp.concatenate([wq_plain, wq_swap], axis=1).astype(BF16),
        "w_k": wk.astype(BF16), "w_vt": wv.T.astype(BF16),
        "w_pool": w_pool.astype(BF16),
        "g_post_mix": row(g_post_mix), "g_pre_ffn": row(g_pre_ffn), "g_post_ffn": row(g_post_ffn),
        "w_o_attn": woa.astype(BF16), "w_o_pool": w_o[mla_w:].astype(BF16),
        "w_router": jnp.pad(w_router, ((0, 0), (0, LANES - N_EXPERTS))).astype(BF16),
        "b_router": jnp.pad(b_router, (0, LANES - N_EXPERTS)).reshape(1, LANES).astype(F32),
        "w_gu": w_gu.astype(BF16), "b_gu": b_gu.reshape(N_EXPERTS, 1, 2 * D_FF).astype(F32),
        "w_down": w_down.astype(BF16), "b_down": b_down.reshape(N_EXPERTS, 1, D_MODEL).astype(F32),
    }


def _rope_tables(pos):
    half = QK_ROPE // 2
    inv = ROPE_THETA ** (-jnp.arange(half, dtype=F32) / half)
    ang = pos.astype(F32)[:, None] * inv[None, :]
    cos, sin = jnp.cos(ang), jnp.sin(ang)
    n = pos.shape[0]
    cos_t = jnp.concatenate([jnp.ones((n, QK_NOPE), F32), cos, cos, jnp.zeros((n, LANES - QK_HEAD), F32)], axis=1)
    sin_t = jnp.concatenate([jnp.zeros((n, QK_NOPE), F32), sin, sin, jnp.zeros((n, LANES - QK_HEAD), F32)], axis=1)
    return cos_t, sin_t


def _tile(n, pref):
    return pref if n % pref == 0 else n


def _mixer_path(x, mod, pos0, hist, cache, w, cnt0):
    nb, nt, _ = x.shape
    tm = _tile(nt, 512)
    cos_t, sin_t = _rope_tables(pos0 + jnp.arange(nt, dtype=jnp.int32))
    q, k, vt, kv_new, kslab, pooled, pool_tail = _premix(x, mod, cos_t, sin_t, hist, w, tm=tm, pos0=pos0)
    if cache is None:
        attn = _attention(q, k, vt, tq=tm, tk=tm, causal=True, kv_len=nt)
    else:
        ckv, ckpe = cache
        past = ckv.shape[1]
        ckpe_slab = jnp.pad(ckpe, ((0, 0), (0, 0), (QK_NOPE, LANES - QK_HEAD)))
        kc, vtc = _kvproj(ckv, ckpe_slab, w["w_k"], w["w_vt"], tm=_tile(past, 512))
        kv_len = past + nt
        tk = -(-kv_len // 256) * 256
        k_all = jnp.concatenate([kc, k, jnp.zeros((nb, tk - kv_len, HEAD_SLABS), BF16)], axis=1)
        vt_all = jnp.concatenate([vtc, vt, jnp.zeros((nb, HEAD_SLABS, tk - kv_len), BF16)], axis=2)
        tq = -(-nt // LANES) * LANES
        q_pad = jnp.pad(q, ((0, 0), (0, tq - nt), (0, 0)))
        attn = _attention(q_pad, k_all, vt_all, tq=tq, tk=tk, causal=False, kv_len=kv_len)[:, :nt]
    x1, h2, ri, rg, cnt = _postmix(attn, pooled, x, mod, cnt0, w, tm=tm)
    return x1, h2, ri, rg, cnt, kv_new, kslab[..., QK_NOPE:QK_HEAD], pool_tail[:, 1:]


def kernel(x_prompt, x_sample, c_prompt, c_sample, cache_kv_latent, cache_k_rope, state_pool, w_ada, b_ada,
           g_pre_mix, w_in, g_q_a, w_q_b, g_kv_a, w_uk, w_uv, w_pool, pool_scale, w_o, g_post_mix, g_pre_ffn,
           w_router, b_router, w_gu, b_gu, w_down, b_down, g_post_ffn):
    assert w_ada.shape[0] == 1, "single-layer step"
    bp, sp, _ = x_prompt.shape
    bs, ss, _ = x_sample.shape
    past = cache_kv_latent.shape[2]
    w = _prep_weights(w_in[0], g_q_a[0], w_q_b[0], g_kv_a[0], w_uk[0], w_uv[0], w_pool[0], pool_scale[0], w_o[0],
                      g_pre_mix[0], g_post_mix[0], g_pre_ffn[0], w_router[0], b_router[0], w_gu[0], b_gu[0],
                      w_down[0], b_down[0], g_post_ffn[0])

    mod = _ada(jnp.concatenate([c_prompt, c_sample], axis=0), w_ada[0], b_ada[0])
    mod = mod.reshape(bp + bs, N_MOD, D_MODEL)
    mod_p, mod_s = mod[:bp], mod[bp:]

    hist_p = jnp.zeros((bp, HIST_ROWS, POOL_WIDTH), F32)
    hist_s = jnp.pad(state_pool[0], ((0, 0), (1, 0), (0, 0)))
    cnt0 = jnp.zeros((8, LANES), F32)
    x1p, h2p, rip, rgp, cntp, kv_p, kpe_p, pool_p = _mixer_path(x_prompt, mod_p, 0, hist_p, None, w, cnt0)
    x1s, h2s, ris, rgs, cnts, kv_s, kpe_s, pool_s = _mixer_path(
        x_sample, mod_s, past, hist_s, (cache_kv_latent[0], cache_k_rope[0]), w, cntp)

    counts = cnts[0, :N_EXPERTS].astype(jnp.int32)
    padded = (counts + MOE_BLOCK - 1) // MOE_BLOCK * MOE_BLOCK
    pad_end = jnp.cumsum(padded)
    pad_start = pad_end - padded
    n_tok = bp * sp + bs * ss
    n_blocks = -(-(n_tok * TOP_K) // MOE_BLOCK) + N_EXPERTS
    block_row = jnp.arange(n_blocks, dtype=jnp.int32) * MOE_BLOCK
    block_e = jnp.minimum(jnp.sum((pad_end[None, :] <= block_row[:, None]).astype(jnp.int32), axis=1),
                          N_EXPERTS - 1)
    n_used = (pad_end[-1:] // MOE_BLOCK).astype(jnp.int32)

    def dest_of(ri):
        ids = ri[..., :TOP_K]
        hot = ids[..., None] == jnp.arange(N_EXPERTS, dtype=jnp.int32)
        return ri[..., TOP_K:2 * TOP_K] + jnp.sum(jnp.where(hot, pad_start, 0), axis=-1)

    dest_p = dest_of(rip)
    dest_s = dest_of(ris)
    xs = jnp.zeros((n_blocks * MOE_BLOCK, D_MODEL), F32)
    xs = _dispatch(dest_p, h2p.reshape(bp * sp, D_MODEL), xs, tm=512)
    xs = _dispatch(dest_s, h2s.reshape(bs * ss, D_MODEL), xs, tm=_tile(bs * ss, 512))
    ys = _experts(block_e, n_used, xs, w)
    y_p = _combine(dest_p, x1p, rgp, mod_p, w["g_post_ffn"], ys, tm=_tile(sp, 256))
    y_s = _combine(dest_s, x1s, rgs, mod_s, w["g_post_ffn"], ys, tm=_tile(ss, 256))
    return (y_p, y_s, kv_p[None], kpe_p[None], pool_p[None], kv_s[None], kpe_s[None], pool_s[None])
```

```python
import functools

import jax
import jax.numpy as jnp
from jax import lax
from jax.experimental import pallas as pl
from jax.experimental.pallas import tpu as pltpu

F32 = jnp.float32
BF16 = jnp.bfloat16

D_MODEL = 1024
CHUNK = 64
N_HEADS = 8
QK_NOPE = 64
QK_ROPE = 32
QK_HEAD = QK_NOPE + QK_ROPE
V_HEAD = 64
Q_LORA = 384
KV_LORA = 256
ROPE_THETA = 10000.0
POOL_WINDOWS = (2, 4, 8, 16)
POOL_GROUP_DIM = 128
POOL_WIDTH = POOL_GROUP_DIM * len(POOL_WINDOWS)
POOL_HIST = max(POOL_WINDOWS) - 1
HIST_ROWS = POOL_HIST + 1
N_EXPERTS = 32
TOP_K = 4
D_FF = 1024
SWIGLU_LIMIT = 7.0
SWIGLU_ALPHA = 1.702
N_MOD = 6
EPS = 1e-6

LANES = 128
HEAD_SLABS = N_HEADS * LANES
ONE_LANE = V_HEAD
IN_EXT = Q_LORA + KV_LORA + POOL_WIDTH + 2 * LANES
U_OFF = Q_LORA + KV_LORA
KPE_OFF = U_OFF + POOL_WIDTH
SM_SCALE = QK_HEAD ** -0.5
LOG2_E = 1.4426950408889634
Q_SCALE = SM_SCALE * LOG2_E
NEG_INF = float("-inf")

SUBLANES = 8
MOE_BLOCK = 512
ZERO_ROWS = MOE_BLOCK + SUBLANES
QK_AHEAD = 1
ROWS_PER_PASS = 32
VMEM_LIMIT = 56 * 1024 * 1024


def _cparams(n_axes, vmem=VMEM_LIMIT):
    return pltpu.CompilerParams(dimension_semantics=("arbitrary",) * n_axes, vmem_limit_bytes=vmem)


def _rms(x, g):
    return x * lax.rsqrt(jnp.mean(x * x, axis=-1, keepdims=True) + EPS) * g


def _ada_kernel(c_ref, w_ref, b_ref, o_ref):
    c = c_ref[...]
    s = (c * jax.nn.sigmoid(c)).astype(BF16)
    o_ref[...] = jnp.dot(s, w_ref[...].astype(BF16), preferred_element_type=F32) + b_ref[...]


def _ada(c, w_ada, b_ada):
    nb = c.shape[0]
    return pl.pallas_call(
        _ada_kernel,
        out_shape=jax.ShapeDtypeStruct((nb, N_MOD * D_MODEL), F32),
        grid=(N_MOD,),
        in_specs=[pl.BlockSpec((nb, D_MODEL), lambda j: (0, 0)),
                  pl.BlockSpec((D_MODEL, D_MODEL), lambda j: (0, j)),
                  pl.BlockSpec((1, D_MODEL), lambda j: (0, j))],
        out_specs=pl.BlockSpec((nb, D_MODEL), lambda j: (0, j)),
        compiler_params=_cparams(1),
        name="ada",
    )(c, w_ada, b_ada.reshape(1, -1))


def _store_kv(kvb, kslab, wk_ref, wvt_ref, k_ref, vt_ref):
    kk = jnp.dot(kvb, wk_ref[...], preferred_element_type=F32)
    for hd in range(N_HEADS):
        sl = slice(hd * LANES, (hd + 1) * LANES)
        k_ref[:, sl] = (kk[:, sl] + kslab).astype(BF16)
    vt = lax.dot_general(wvt_ref[...], kvb, (((1,), (1,)), ((), ())), preferred_element_type=F32)
    row = lax.broadcasted_iota(jnp.int32, (HEAD_SLABS, 1), 0)
    vt_ref[...] = (vt + (row % LANES == ONE_LANE).astype(F32)).astype(BF16)


def _premix_kernel(x_ref, mod_ref, cos_ref, sin_ref, hist_ref, gpre_ref, gqa_ref, gkv_ref, pscale_ref,
                   win_ref, wq_ref, wk_ref, wvt_ref, wpool_ref,
                   q_ref, k_ref, vt_ref, kv_ref, kpe_ref, pooled_ref, poolnew_ref,
                   ubuf, *, tm, pos0):
    t = pl.program_id(1)
    x = x_ref[...]
    h = (_rms(x, gpre_ref[...]) * (1.0 + mod_ref[1:2, :]) + mod_ref[0:1, :]).astype(BF16)
    z = jnp.dot(h, win_ref[...], preferred_element_type=F32)
    cosv = cos_ref[...]
    sinv = sin_ref[...]

    qan = _rms(z[:, :Q_LORA], gqa_ref[...]).astype(BF16)
    qq = jnp.dot(qan, wq_ref[...], preferred_element_type=F32)
    for hd in range(N_HEADS):
        a = qq[:, hd * LANES:(hd + 1) * LANES]
        b = qq[:, HEAD_SLABS + hd * LANES:HEAD_SLABS + (hd + 1) * LANES]
        q_ref[:, hd * LANES:(hd + 1) * LANES] = ((a * cosv + b * sinv) * Q_SCALE).astype(BF16)

    kvn = _rms(z[:, Q_LORA:U_OFF], gkv_ref[...])
    kv_ref[...] = kvn
    kslab = z[:, KPE_OFF:KPE_OFF + LANES] * cosv + z[:, KPE_OFF + LANES:KPE_OFF + 2 * LANES] * sinv
    kpe_ref[...] = kslab
    _store_kv(kvn.astype(BF16), kslab, wk_ref, wvt_ref, k_ref, vt_ref)

    @pl.when(t == 0)
    def _():
        ubuf[0:HIST_ROWS, :] = hist_ref[...]

    @pl.when(t > 0)
    def _():
        ubuf[0:HIST_ROWS, :] = ubuf[tm:tm + HIST_ROWS, :]

    ubuf[HIST_ROWS:HIST_ROWS + tm, :] = z[:, U_OFF:KPE_OFF]
    pos = pos0 + t * tm + lax.broadcasted_iota(jnp.int32, (tm, 1), 0)
    for g, w in enumerate(POOL_WINDOWS):
        sl = slice(g * POOL_GROUP_DIM, (g + 1) * POOL_GROUP_DIM)
        u = ubuf[HIST_ROWS:HIST_ROWS + tm, sl]
        acc = u
        for j in range(1, w):
            acc = acc + ubuf[HIST_ROWS - j:HIST_ROWS - j + tm, sl]
        cnt = jnp.minimum(pos + 1, w).astype(F32)
        d = (acc / cnt - u).astype(BF16)
        y = jnp.dot(d, wpool_ref[g], preferred_element_type=F32) * pscale_ref[:, sl]
        pooled_ref[:, sl] = y.astype(BF16)
    poolnew_ref[...] = ubuf[tm:tm + HIST_ROWS, :]


def _premix(x, mod, cos_t, sin_t, hist, w, *, tm, pos0):
    nb, nt, _ = x.shape
    grid = (nb, nt // tm)
    tok = lambda width: pl.BlockSpec((None, tm, width), lambda b, t: (b, t, 0))
    full = lambda a: pl.BlockSpec(a.shape, lambda b, t: (0,) * a.ndim)
    tab = pl.BlockSpec((tm, LANES), lambda b, t: (t, 0))
    perb = lambda rows, width: pl.BlockSpec((None, rows, width), lambda b, t: (b, 0, 0))
    outs = [
        jax.ShapeDtypeStruct((nb, nt, HEAD_SLABS), BF16),
        jax.ShapeDtypeStruct((nb, nt, HEAD_SLABS), BF16),
        jax.ShapeDtypeStruct((nb, HEAD_SLABS, nt), BF16),
        jax.ShapeDtypeStruct((nb, nt, KV_LORA), F32),
        jax.ShapeDtypeStruct((nb, nt, LANES), F32),
        jax.ShapeDtypeStruct((nb, nt, POOL_WIDTH), BF16),
        jax.ShapeDtypeStruct((nb, HIST_ROWS, POOL_WIDTH), F32),
    ]
    return pl.pallas_call(
        functools.partial(_premix_kernel, tm=tm, pos0=pos0),
        out_shape=outs,
        grid=grid,
        in_specs=[tok(D_MODEL), perb(N_MOD, D_MODEL), tab, tab, perb(HIST_ROWS, POOL_WIDTH),
                  full(w["g_pre_mix"]), full(w["g_q_a"]), full(w["g_kv_a"]), full(w["pool_scale"]),
                  full(w["w_in"]), full(w["w_q"]), full(w["w_k"]), full(w["w_vt"]), full(w["w_pool"])],
        out_specs=[tok(HEAD_SLABS), tok(HEAD_SLABS),
                   pl.BlockSpec((None, HEAD_SLABS, tm), lambda b, t: (b, 0, t)),
                   tok(KV_LORA), tok(LANES), tok(POOL_WIDTH), perb(HIST_ROWS, POOL_WIDTH)],
        scratch_shapes=[pltpu.VMEM((HIST_ROWS + tm, POOL_WIDTH), F32)],
        compiler_params=_cparams(2),
        name="premix",
    )(x, mod, cos_t, sin_t, hist, w["g_pre_mix"], w["g_q_a"], w["g_kv_a"], w["pool_scale"],
      w["w_in"], w["w_q"], w["w_k"], w["w_vt"], w["w_pool"])


def _kvproj_kernel(kv_ref, kpe_ref, wk_ref, wvt_ref, k_ref, vt_ref):
    _store_kv(kv_ref[...].astype(BF16), kpe_ref[...], wk_ref, wvt_ref, k_ref, vt_ref)


def _kvproj(kv, kpe_slab, w_k, w_vt, *, tm):
    nb, nt, _ = kv.shape
    tok = lambda width: pl.BlockSpec((None, tm, width), lambda b, t: (b, t, 0))
    full = lambda a: pl.BlockSpec(a.shape, lambda b, t: (0,) * a.ndim)
    return pl.pallas_call(
        _kvproj_kernel,
        out_shape=[jax.ShapeDtypeStruct((nb, nt, HEAD_SLABS), BF16),
                   jax.ShapeDtypeStruct((nb, HEAD_SLABS, nt), BF16)],
        grid=(nb, nt // tm),
        in_specs=[tok(KV_LORA), tok(LANES), full(w_k), full(w_vt)],
        out_specs=[tok(HEAD_SLABS), pl.BlockSpec((None, HEAD_SLABS, tm), lambda b, t: (b, 0, t))],
        compiler_params=_cparams(2),
        name="kvproj",
    )(kv, kpe_slab, w_k, w_vt)


def _attn_kernel(q_ref, k_ref, vt_ref, o_ref, m_ref, acc_ref, s_buf, p_buf, *, tq, tk, nk, causal, kv_len):
    qi = pl.program_id(1)
    ki = pl.program_id(2)
    last = ((qi + 1) * tq - 1) // tk if causal else nk - 1
    need_len_mask = kv_len < nk * tk

    @pl.when(ki == 0)
    def _():
        m_ref[...] = jnp.full(m_ref.shape, NEG_INF, F32)
        acc_ref[...] = jnp.zeros(acc_ref.shape, F32)

    def visible(c):
        kpos = ki * tk + c * ROWS_PER_PASS + lax.broadcasted_iota(jnp.int32, (ROWS_PER_PASS, tq), 0)
        vis = None
        if causal:
            qpos = qi * tq + lax.broadcasted_iota(jnp.int32, (ROWS_PER_PASS, tq), 1)
            vis = (kpos // CHUNK) <= (qpos // CHUNK)
        if need_len_mask:
            lm = kpos < kv_len
            vis = lm if vis is None else (vis & lm)
        return vis

    def step(masked):
        n_pass = tk // ROWS_PER_PASS

        def scores(hd):
            sl = slice(hd * LANES, (hd + 1) * LANES)
            s_buf[hd % (QK_AHEAD + 1)] = lax.dot_general(
                k_ref[:, sl], q_ref[:, sl], (((1,), (1,)), ((), ())), preferred_element_type=F32)

        def rows(hd, c):
            blk = s_buf[hd % (QK_AHEAD + 1), c * ROWS_PER_PASS:(c + 1) * ROWS_PER_PASS, :]
            return jnp.where(visible(c), blk, NEG_INF) if masked else blk

        for hd in range(QK_AHEAD):
            scores(hd)
        for hd in range(N_HEADS):
            sl = slice(hd * LANES, (hd + 1) * LANES)
            if hd + QK_AHEAD < N_HEADS:
                scores(hd + QK_AHEAD)
            mx = rows(hd, 0)
            for c in range(1, n_pass):
                mx = jnp.maximum(mx, rows(hd, c))
            m_prev = m_ref[hd:hd + 1, :]
            m_new = jnp.maximum(m_prev, jnp.max(mx, axis=0, keepdims=True))
            alpha = jnp.exp2(m_prev - m_new)
            for c in range(n_pass):
                p_buf[c * ROWS_PER_PASS:(c + 1) * ROWS_PER_PASS, :] = jnp.exp2(rows(hd, c) - m_new).astype(BF16)
            acc_ref[hd] = alpha * acc_ref[hd] + jnp.dot(vt_ref[sl, :], p_buf[...], preferred_element_type=F32)
            m_ref[hd:hd + 1, :] = m_new

    if causal or need_len_mask:
        @pl.when(ki < last)
        def _():
            step(False)

        @pl.when(ki == last)
        def _():
            step(True)
    else:
        step(False)

    @pl.when(ki == nk - 1)
    def _():
        for hd in range(N_HEADS):
            acc = acc_ref[hd]
            out_t = acc / acc[ONE_LANE:ONE_LANE + 1, :]
            o_ref[:, hd * LANES:(hd + 1) * LANES] = out_t.T.astype(BF16)


def _attention(q, k, vt, *, tq, tk, causal, kv_len):
    nb, nq_tot, _ = q.shape
    nk = k.shape[1] // tk
    nq = nq_tot // tq
    if causal:
        last = lambda i: ((i + 1) * tq - 1) // tk
        kmap = lambda b, i, j: (b, jnp.minimum(j, last(i)), 0)
        vmap = lambda b, i, j: (b, 0, jnp.minimum(j, last(i)))
    else:
        kmap = lambda b, i, j: (b, j, 0)
        vmap = lambda b, i, j: (b, 0, j)
    return pl.pallas_call(
        functools.partial(_attn_kernel, tq=tq, tk=tk, nk=nk, causal=causal, kv_len=kv_len),
        out_shape=jax.ShapeDtypeStruct((nb, nq_tot, HEAD_SLABS), BF16),
        grid=(nb, nq, nk),
        in_specs=[pl.BlockSpec((None, tq, HEAD_SLABS), lambda b, i, j: (b, i, 0)),
                  pl.BlockSpec((None, tk, HEAD_SLABS), kmap),
                  pl.BlockSpec((None, HEAD_SLABS, tk), vmap)],
        out_specs=pl.BlockSpec((None, tq, HEAD_SLABS), lambda b, i, j: (b, i, 0)),
        scratch_shapes=[pltpu.VMEM((N_HEADS, tq), F32), pltpu.VMEM((N_HEADS, LANES, tq), F32),
                        pltpu.VMEM((QK_AHEAD + 1, tk, tq), F32), pltpu.VMEM((tk, tq), BF16)],
        compiler_params=_cparams(3),
        name="attention",
    )(q, k, vt)


def _postmix_kernel(attn_ref, pooled_ref, x_ref, mod_ref, cnt0_ref, gpost_ref, gffn_ref,
                    woa_ref, wop_ref, wr_ref, br_ref,
                    x1_ref, h2_ref, ri_ref, rg_ref, cnt_ref,
                    carry, ltri, *, tm):
    first = (pl.program_id(0) == 0) & (pl.program_id(1) == 0)

    @pl.when(first)
    def _():
        carry[...] = cnt0_ref[...]
        r = lax.broadcasted_iota(jnp.int32, (tm, tm), 0)
        c = lax.broadcasted_iota(jnp.int32, (tm, tm), 1)
        ltri[...] = (r > c).astype(BF16)

    mix = (jnp.dot(attn_ref[...], woa_ref[...], preferred_element_type=F32)
           + jnp.dot(pooled_ref[...], wop_ref[...], preferred_element_type=F32))
    x1 = x_ref[...] + mod_ref[2:3, :] * _rms(mix, gpost_ref[...])
    x1_ref[...] = x1
    h2 = _rms(x1, gffn_ref[...]) * (1.0 + mod_ref[4:5, :]) + mod_ref[3:4, :]
    h2_ref[...] = h2

    lane = lax.broadcasted_iota(jnp.int32, (tm, LANES), 1).astype(F32)
    logits = jnp.dot(h2.astype(BF16), wr_ref[...], preferred_element_type=F32) + br_ref[...]
    logits = jnp.where(lane < N_EXPERTS, logits, NEG_INF)
    sel = jnp.zeros((tm, LANES), F32)
    ids, vals, hots = [], [], []
    for _ in range(TOP_K):
        mk = jnp.max(logits, axis=1, keepdims=True)
        idx = jnp.min(jnp.where(logits == mk, lane, float(LANES)), axis=1, keepdims=True)
        hot = lane == idx
        logits = jnp.where(hot, NEG_INF, logits)
        sel = sel + hot.astype(F32)
        ids.append(idx)
        vals.append(mk)
        hots.append(hot)
    ex = [jnp.exp(vk - vals[0]) for vk in vals]
    denom = ex[0] + ex[1] + ex[2] + ex[3]

    before = jnp.dot(ltri[...], sel.astype(BF16), preferred_element_type=F32) + carry[0:1, :]
    ri = jnp.zeros((tm, LANES), F32)
    rg = jnp.zeros((tm, LANES), F32)
    for kk in range(TOP_K):
        rank = jnp.sum(jnp.where(hots[kk], before, 0.0), axis=1, keepdims=True)
        ri = jnp.where(lane == kk, ids[kk], ri)
        ri = jnp.where(lane == TOP_K + kk, rank, ri)
        rg = jnp.where(lane == kk, ex[kk] / denom, rg)
    ri_ref[...] = ri.astype(jnp.int32)
    rg_ref[...] = rg
    carry[0:1, :] = carry[0:1, :] + jnp.sum(sel, axis=0, keepdims=True)
    cnt_ref[...] = carry[...]


def _postmix(attn, pooled, x, mod, cnt0, w, *, tm):
    nb, nt, _ = x.shape
    tok = lambda width: pl.BlockSpec((None, tm, width), lambda b, t: (b, t, 0))
    full = lambda a: pl.BlockSpec(a.shape, lambda b, t: (0,) * a.ndim)
    outs = [
        jax.ShapeDtypeStruct((nb, nt, D_MODEL), F32),
        jax.ShapeDtypeStruct((nb, nt, D_MODEL), F32),
        jax.ShapeDtypeStruct((nb, nt, LANES), jnp.int32),
        jax.ShapeDtypeStruct((nb, nt, LANES), F32),
        jax.ShapeDtypeStruct((8, LANES), F32),
    ]
    return pl.pallas_call(
        functools.partial(_postmix_kernel, tm=tm),
        out_shape=outs,
        grid=(nb, nt // tm),
        in_specs=[tok(HEAD_SLABS), tok(POOL_WIDTH), tok(D_MODEL),
                  pl.BlockSpec((None, N_MOD, D_MODEL), lambda b, t: (b, 0, 0)),
                  full(cnt0), full(w["g_post_mix"]), full(w["g_pre_ffn"]),
                  full(w["w_o_attn"]), full(w["w_o_pool"]), full(w["w_router"]), full(w["b_router"])],
        out_specs=[tok(D_MODEL), tok(D_MODEL), tok(LANES), tok(LANES),
                   pl.BlockSpec((8, LANES), lambda b, t: (0, 0))],
        scratch_shapes=[pltpu.VMEM((8, LANES), F32), pltpu.VMEM((tm, tm), BF16)],
        compiler_params=_cparams(2),
        name="postmix",
    )(attn, pooled, x, mod, cnt0, w["g_post_mix"], w["g_pre_ffn"],
      w["w_o_attn"], w["w_o_pool"], w["w_router"], w["b_router"])


def _dispatch_rows(dest_ref, h_ref, xs_ref, sem, tm):
    def issue(i, carry):
        for kk in range(TOP_K):
            pltpu.make_async_copy(h_ref.at[pl.ds(i, 1), :],
                                  xs_ref.at[pl.ds(dest_ref[0, 0, i * TOP_K + kk], 1), :], sem).start()
        return carry

    lax.fori_loop(0, tm, issue, 0)
    for _ in range(TOP_K):
        pltpu.make_async_copy(h_ref, xs_ref.at[pl.ds(0, tm), :], sem).wait()


def _dispatch_first_kernel(zfrom_ref, dest_ref, h_ref, xs_ref, zbuf, sem, zsem, *, tm):
    @pl.when(pl.program_id(0) == 0)
    def _():
        zbuf[...] = jnp.zeros(zbuf.shape, F32)
        for e in range(N_EXPERTS):
            start = pl.multiple_of(zfrom_ref[e], SUBLANES)
            cp = pltpu.make_async_copy(zbuf, xs_ref.at[pl.ds(start, ZERO_ROWS), :], zsem)
            cp.start()
            cp.wait()

    _dispatch_rows(dest_ref, h_ref, xs_ref, sem, tm)


def _dispatch_more_kernel(dest_ref, h_ref, xs_in_ref, xs_ref, sem, *, tm):
    del xs_in_ref
    _dispatch_rows(dest_ref, h_ref, xs_ref, sem, tm)


def _dispatch_first(zfrom, dest, h, n_rows, *, tm):
    nt = h.shape[0] // tm
    return pl.pallas_call(
        functools.partial(_dispatch_first_kernel, tm=tm),
        out_shape=jax.ShapeDtypeStruct((n_rows, D_MODEL), F32),
        grid_spec=pltpu.PrefetchScalarGridSpec(
            num_scalar_prefetch=1,
            grid=(nt,),
            in_specs=[pl.BlockSpec((1, 1, tm * TOP_K), lambda i, z: (i, 0, 0), memory_space=pltpu.SMEM),
                      pl.BlockSpec((tm, D_MODEL), lambda i, z: (i, 0))],
            out_specs=pl.BlockSpec(memory_space=pl.ANY),
            scratch_shapes=[pltpu.VMEM((ZERO_ROWS, D_MODEL), F32),
                            pltpu.SemaphoreType.DMA(()), pltpu.SemaphoreType.DMA(())],
        ),
        compiler_params=_cparams(1),
        name="dispatch_first",
    )(zfrom, dest.reshape(nt, 1, tm * TOP_K), h)


def _dispatch_more(dest, h, xs, *, tm):
    nt = h.shape[0] // tm
    return pl.pallas_call(
        functools.partial(_dispatch_more_kernel, tm=tm),
        out_shape=jax.ShapeDtypeStruct(xs.shape, xs.dtype),
        grid=(nt,),
        in_specs=[pl.BlockSpec((1, 1, tm * TOP_K), lambda i: (i, 0, 0), memory_space=pltpu.SMEM),
                  pl.BlockSpec((tm, D_MODEL), lambda i: (i, 0)),
                  pl.BlockSpec(memory_space=pl.ANY)],
        out_specs=pl.BlockSpec(memory_space=pl.ANY),
        scratch_shapes=[pltpu.SemaphoreType.DMA(())],
        input_output_aliases={2: 0},
        compiler_params=_cparams(1),
        name="dispatch_more",
    )(dest.reshape(nt, 1, tm * TOP_K), h, xs)


def _expert_kernel(be_ref, nu_ref, xs_ref, wgu_ref, bgu_ref, wd_ref, bd_ref, ys_ref):
    del be_ref

    @pl.when(pl.program_id(0) < nu_ref[0])
    def _():
        x = xs_ref[...].astype(BF16)
        gu = jnp.dot(x, wgu_ref[...], preferred_element_type=F32) + bgu_ref[...]
        g = jnp.minimum(gu[:, :D_FF], SWIGLU_LIMIT)
        u = jnp.clip(gu[:, D_FF:], -SWIGLU_LIMIT, SWIGLU_LIMIT)
        a = (u + 1.0) * (g * jax.nn.sigmoid(SWIGLU_ALPHA * g))
        ys_ref[...] = jnp.dot(a.astype(BF16), wd_ref[...], preferred_element_type=F32) + bd_ref[...]


def _experts(block_e, n_used, xs, w):
    nblk = xs.shape[0] // MOE_BLOCK
    row = lambda i, be, nu: (jnp.minimum(i, nu[0] - 1), 0)
    per_e = lambda i, be, nu: (be[i], 0, 0)
    return pl.pallas_call(
        _expert_kernel,
        out_shape=jax.ShapeDtypeStruct(xs.shape, F32),
        grid_spec=pltpu.PrefetchScalarGridSpec(
            num_scalar_prefetch=2,
            grid=(nblk,),
            in_specs=[pl.BlockSpec((MOE_BLOCK, D_MODEL), row),
                      pl.BlockSpec((None, D_MODEL, 2 * D_FF), per_e),
                      pl.BlockSpec((None, 1, 2 * D_FF), per_e),
                      pl.BlockSpec((None, D_FF, D_MODEL), per_e),
                      pl.BlockSpec((None, 1, D_MODEL), per_e)],
            out_specs=pl.BlockSpec((MOE_BLOCK, D_MODEL), row),
        ),
        compiler_params=_cparams(1),
        name="experts",
    )(block_e, n_used, xs, w["w_gu"], w["b_gu"], w["w_down"], w["b_down"])


def _combine_kernel(dcur_ref, dnext_ref, x1_ref, rg_ref, mod_ref, gpost_ref, ys_ref, o_ref, gbuf, sem, *, tm, n):
    i = pl.program_id(0)
    slot = i % 2

    def gather(dest_ref, sl):
        def issue(r, carry):
            for kk in range(TOP_K):
                pltpu.make_async_copy(ys_ref.at[pl.ds(dest_ref[0, 0, r * TOP_K + kk], 1), :],
                                      gbuf.at[sl, kk, pl.ds(r, 1), :], sem.at[sl]).start()
            return carry

        lax.fori_loop(0, tm, issue, 0)

    @pl.when(i == 0)
    def _():
        gather(dcur_ref, 0)

    @pl.when(i + 1 < n)
    def _():
        gather(dnext_ref, 1 - slot)

    for kk in range(TOP_K):
        pltpu.make_async_copy(ys_ref.at[pl.ds(0, tm), :], gbuf.at[slot, kk], sem.at[slot]).wait()

    rg = rg_ref[...]
    y = rg[:, 0:1] * gbuf[slot, 0]
    for kk in range(1, TOP_K):
        y = y + rg[:, kk:kk + 1] * gbuf[slot, kk]
    o_ref[...] = x1_ref[...] + mod_ref[5:6, :] * _rms(y, gpost_ref[...])


def _combine(dest, x1, rg, mod, g_post, ys, *, tm):
    nb, nt, _ = x1.shape
    ntile = nt // tm
    n = nb * ntile
    tok = lambda width: pl.BlockSpec((tm, width), lambda i: (i, 0))
    dest = dest.reshape(n, 1, tm * TOP_K)
    dspec = lambda f: pl.BlockSpec((1, 1, tm * TOP_K), lambda i: (f(i), 0, 0), memory_space=pltpu.SMEM)
    out = pl.pallas_call(
        functools.partial(_combine_kernel, tm=tm, n=n),
        out_shape=jax.ShapeDtypeStruct((nb * nt, D_MODEL), F32),
        grid=(n,),
        in_specs=[dspec(lambda i: i), dspec(lambda i: jnp.minimum(i + 1, n - 1)),
                  tok(D_MODEL), tok(LANES),
                  pl.BlockSpec((None, N_MOD, D_MODEL), lambda i: (i // ntile, 0, 0)),
                  pl.BlockSpec(g_post.shape, lambda i: (0, 0)),
                  pl.BlockSpec(memory_space=pl.ANY)],
        out_specs=tok(D_MODEL),
        scratch_shapes=[pltpu.VMEM((2, TOP_K, tm, D_MODEL), F32), pltpu.SemaphoreType.DMA((2,))],
        compiler_params=_cparams(1),
        name="combine",
    )(dest, dest, x1.reshape(nb * nt, D_MODEL), rg.reshape(nb * nt, LANES), mod, g_post, ys)
    return out.reshape(x1.shape)


def _rot_swap(w):
    half = QK_ROPE // 2
    return jnp.concatenate([-w[..., half:], w[..., :half]], axis=-1)


def _prep_weights(w_in, g_q_a, w_q_b, g_kv_a, w_uk, w_uv, w_pool, pool_scale, w_o, g_pre_mix, g_post_mix,
                  g_pre_ffn, w_router, b_router, w_gu, b_gu, w_down, b_down, g_post_ffn):
    row = lambda a: a.reshape(1, -1).astype(F32)
    w_kpe = w_in[:, U_OFF:U_OFF + QK_ROPE]
    zeros = lambda *s: jnp.zeros(s, F32)
    d = D_MODEL
    slab = lambda a: jnp.concatenate([zeros(d, QK_NOPE), a, zeros(d, LANES - QK_HEAD)], axis=1)
    w_in_ext = jnp.concatenate([w_in[:, :U_OFF], w_in[:, U_OFF + QK_ROPE:], slab(w_kpe), slab(_rot_swap(w_kpe))],
                               axis=1)
    pad_q = zeros(Q_LORA, N_HEADS, LANES - QK_HEAD)
    wq_plain = jnp.concatenate([w_q_b, pad_q], axis=2).reshape(Q_LORA, HEAD_SLABS)
    wq_swap = jnp.concatenate([zeros(Q_LORA, N_HEADS, QK_NOPE), _rot_swap(w_q_b[..., QK_NOPE:]), pad_q],
                              axis=2).reshape(Q_LORA, HEAD_SLABS)
    pad_kv = zeros(KV_LORA, N_HEADS, LANES - QK_NOPE)
    wk = jnp.concatenate([w_uk, pad_kv], axis=2).reshape(KV_LORA, HEAD_SLABS)
    wv = jnp.concatenate([w_uv, pad_kv], axis=2).reshape(KV_LORA, HEAD_SLABS)
    mla_w = N_HEADS * V_HEAD
    woa = jnp.concatenate([w_o[:mla_w].reshape(N_HEADS, V_HEAD, d), zeros(N_HEADS, LANES - V_HEAD, d)],
                          axis=1).reshape(HEAD_SLABS, d)
    return {
        "g_pre_mix": row(g_pre_mix), "g_q_a": row(g_q_a), "g_kv_a": row(g_kv_a), "pool_scale": row(pool_scale),
        "w_in": w_in_ext.astype(BF16),
        "w_q": jnp.concatenate([wq_plain, wq_swap], axis=1).astype(BF16),
        "w_k": wk.astype(BF16), "w_vt": wv.T.astype(BF16),
        "w_pool": w_pool.astype(BF16),
        "g_post_mix": row(g_post_mix), "g_pre_ffn": row(g_pre_ffn), "g_post_ffn": row(g_post_ffn),
        "w_o_attn": woa.astype(BF16), "w_o_pool": w_o[mla_w:].astype(BF16),
        "w_router": jnp.pad(w_router, ((0, 0), (0, LANES - N_EXPERTS))).astype(BF16),
        "b_router": jnp.pad(b_router, (0, LANES - N_EXPERTS)).reshape(1, LANES).astype(F32),
        "w_gu": w_gu.astype(BF16), "b_gu": b_gu.reshape(N_EXPERTS, 1, 2 * D_FF).astype(F32),
        "w_down": w_down.astype(BF16), "b_down": b_down.reshape(N_EXPERTS, 1, D_MODEL).astype(F32),
    }


def _rope_tables(pos):
    half = QK_ROPE // 2
    inv = ROPE_THETA ** (-jnp.arange(half, dtype=F32) / half)
    ang = pos.astype(F32)[:, None] * inv[None, :]
    cos, sin = jnp.cos(ang), jnp.sin(ang)
    n = pos.shape[0]
    cos_t = jnp.concatenate([jnp.ones((n, QK_NOPE), F32), cos, cos, jnp.zeros((n, LANES - QK_HEAD), F32)], axis=1)
    sin_t = jnp.concatenate([jnp.zeros((n, QK_NOPE), F32), sin, sin, jnp.zeros((n, LANES - QK_HEAD), F32)], axis=1)
    return cos_t, sin_t


def _tile(n, pref):
    return pref if n % pref == 0 else n


def _mixer_path(x, mod, pos0, hist, cache, w, cnt0):
    nb, nt, _ = x.shape
    tm = _tile(nt, 512)
    cos_t, sin_t = _rope_tables(pos0 + jnp.arange(nt, dtype=jnp.int32))
    q, k, vt, kv_new, kslab, pooled, pool_tail = _premix(x, mod, cos_t, sin_t, hist, w, tm=tm, pos0=pos0)
    if cache is None:
        attn = _attention(q, k, vt, tq=tm, tk=tm, causal=True, kv_len=nt)
    else:
        ckv, ckpe = cache
        past = ckv.shape[1]
        ckpe_slab = jnp.pad(ckpe, ((0, 0), (0, 0), (QK_NOPE, LANES - QK_HEAD)))
        kc, vtc = _kvproj(ckv, ckpe_slab, w["w_k"], w["w_vt"], tm=_tile(past, 512))
        kv_len = past + nt
        tk = -(-kv_len // 256) * 256
        k_all = jnp.concatenate([kc, k, jnp.zeros((nb, tk - kv_len, HEAD_SLABS), BF16)], axis=1)
        vt_all = jnp.concatenate([vtc, vt, jnp.zeros((nb, HEAD_SLABS, tk - kv_len), BF16)], axis=2)
        tq = -(-nt // LANES) * LANES
        q_pad = jnp.pad(q, ((0, 0), (0, tq - nt), (0, 0)))
        attn = _attention(q_pad, k_all, vt_all, tq=tq, tk=tk, causal=False, kv_len=kv_len)[:, :nt]
    x1, h2, ri, rg, cnt = _postmix(attn, pooled, x, mod, cnt0, w, tm=tm)
    return x1, h2, ri, rg, cnt, kv_new, kslab[..., QK_NOPE:QK_HEAD], pool_tail[:, 1:]


def kernel(x_prompt, x_sample, c_prompt, c_sample, cache_kv_latent, cache_k_rope, state_pool, w_ada, b_ada,
           g_pre_mix, w_in, g_q_a, w_q_b, g_kv_a, w_uk, w_uv, w_pool, pool_scale, w_o, g_post_mix, g_pre_ffn,
           w_router, b_router, w_gu, b_gu, w_down, b_down, g_post_ffn):
    assert w_ada.shape[0] == 1, "single-layer step"
    bp, sp, _ = x_prompt.shape
    bs, ss, _ = x_sample.shape
    past = cache_kv_latent.shape[2]
    w = _prep_weights(w_in[0], g_q_a[0], w_q_b[0], g_kv_a[0], w_uk[0], w_uv[0], w_pool[0], pool_scale[0], w_o[0],
                      g_pre_mix[0], g_post_mix[0], g_pre_ffn[0], w_router[0], b_router[0], w_gu[0], b_gu[0],
                      w_down[0], b_down[0], g_post_ffn[0])

    mod = _ada(jnp.concatenate([c_prompt, c_sample], axis=0), w_ada[0], b_ada[0])
    mod = mod.reshape(bp + bs, N_MOD, D_MODEL)
    mod_p, mod_s = mod[:bp], mod[bp:]

    hist_p = jnp.zeros((bp, HIST_ROWS, POOL_WIDTH), F32)
    hist_s = jnp.pad(state_pool[0], ((0, 0), (1, 0), (0, 0)))
    cnt0 = jnp.zeros((8, LANES), F32)
    x1p, h2p, rip, rgp, cntp, kv_p, kpe_p, pool_p = _mixer_path(x_prompt, mod_p, 0, hist_p, None, w, cnt0)
    x1s, h2s, ris, rgs, cnts, kv_s, kpe_s, pool_s = _mixer_path(
        x_sample, mod_s, past, hist_s, (cache_kv_latent[0], cache_k_rope[0]), w, cntp)

    counts = cnts[0, :N_EXPERTS].astype(jnp.int32)
    padded = (counts + MOE_BLOCK - 1) // MOE_BLOCK * MOE_BLOCK
    pad_end = jnp.cumsum(padded)
    pad_start = pad_end - padded
    n_tok = bp * sp + bs * ss
    n_blocks = -(-(n_tok * TOP_K) // MOE_BLOCK) + N_EXPERTS
    block_row = jnp.arange(n_blocks, dtype=jnp.int32) * MOE_BLOCK
    block_e = jnp.minimum(jnp.sum((pad_end[None, :] <= block_row[:, None]).astype(jnp.int32), axis=1),
                          N_EXPERTS - 1)
    n_used = (pad_end[-1:] // MOE_BLOCK).astype(jnp.int32)

    def dest_of(ri):
        ids = ri[..., :TOP_K]
        hot = ids[..., None] == jnp.arange(N_EXPERTS, dtype=jnp.int32)
        return ri[..., TOP_K:2 * TOP_K] + jnp.sum(jnp.where(hot, pad_start, 0), axis=-1)

    dest_p = dest_of(rip)
    dest_s = dest_of(ris)
    n_rows = n_blocks * MOE_BLOCK
    zfrom = jnp.minimum((pad_start + counts) // SUBLANES * SUBLANES, n_rows - ZERO_ROWS).astype(jnp.int32)
    xs = _dispatch_first(zfrom, dest_p, h2p.reshape(bp * sp, D_MODEL), n_rows, tm=512)
    xs = _dispatch_more(dest_s, h2s.reshape(bs * ss, D_MODEL), xs, tm=_tile(bs * ss, 512))
    ys = _experts(block_e, n_used, xs, w)
    y_p = _combine(dest_p, x1p, rgp, mod_p, w["g_post_ffn"], ys, tm=_tile(sp, 256))
    y_s = _combine(dest_s, x1s, rgs, mod_s, w["g_post_ffn"], ys, tm=_tile(ss, 256))
    return (y_p, y_s, kv_p[None], kpe_p[None], pool_p[None], kv_s[None], kpe_s[None], pool_s[None])
```

```python
import functools

import jax
import jax.numpy as jnp
from jax import lax
from jax.experimental import pallas as pl
from jax.experimental.pallas import tpu as pltpu

F32 = jnp.float32
BF16 = jnp.bfloat16

D_MODEL = 1024
CHUNK = 64
N_HEADS = 8
QK_NOPE = 64
QK_ROPE = 32
QK_HEAD = QK_NOPE + QK_ROPE
V_HEAD = 64
Q_LORA = 384
KV_LORA = 256
ROPE_THETA = 10000.0
POOL_WINDOWS = (2, 4, 8, 16)
POOL_GROUP_DIM = 128
POOL_WIDTH = POOL_GROUP_DIM * len(POOL_WINDOWS)
POOL_HIST = max(POOL_WINDOWS) - 1
HIST_ROWS = POOL_HIST + 1
N_EXPERTS = 32
TOP_K = 4
D_FF = 1024
SWIGLU_LIMIT = 7.0
SWIGLU_ALPHA = 1.702
N_MOD = 6
EPS = 1e-6

LANES = 128
HEAD_SLABS = N_HEADS * LANES
ONE_LANE = V_HEAD
IN_EXT = Q_LORA + KV_LORA + POOL_WIDTH + 2 * LANES
U_OFF = Q_LORA + KV_LORA
KPE_OFF = U_OFF + POOL_WIDTH
SM_SCALE = QK_HEAD ** -0.5
LOG2_E = 1.4426950408889634
Q_SCALE = SM_SCALE * LOG2_E
NEG_INF = float("-inf")

SUBLANES = 8
MOE_BLOCK = 512
ZERO_ROWS = MOE_BLOCK + SUBLANES
VMEM_LIMIT = 56 * 1024 * 1024


def _cparams(n_axes, vmem=VMEM_LIMIT):
    return pltpu.CompilerParams(dimension_semantics=("arbitrary",) * n_axes, vmem_limit_bytes=vmem)


def _rms(x, g):
    return x * lax.rsqrt(jnp.mean(x * x, axis=-1, keepdims=True) + EPS) * g


def _ada_kernel(c_ref, w_ref, b_ref, o_ref):
    c = c_ref[...]
    s = (c * jax.nn.sigmoid(c)).astype(BF16)
    o_ref[...] = jnp.dot(s, w_ref[...].astype(BF16), preferred_element_type=F32) + b_ref[...]


def _ada(c, w_ada, b_ada):
    nb = c.shape[0]
    return pl.pallas_call(
        _ada_kernel,
        out_shape=jax.ShapeDtypeStruct((nb, N_MOD * D_MODEL), F32),
        grid=(N_MOD,),
        in_specs=[pl.BlockSpec((nb, D_MODEL), lambda j: (0, 0)),
                  pl.BlockSpec((D_MODEL, D_MODEL), lambda j: (0, j)),
                  pl.BlockSpec((1, D_MODEL), lambda j: (0, j))],
        out_specs=pl.BlockSpec((nb, D_MODEL), lambda j: (0, j)),
        compiler_params=_cparams(1),
        name="ada",
    )(c, w_ada, b_ada.reshape(1, -1))


def _store_kv(kvb, kslab, wk_ref, wvt_ref, k_ref, vt_ref):
    kk = jnp.dot(kvb, wk_ref[...], preferred_element_type=F32)
    for hd in range(N_HEADS):
        sl = slice(hd * LANES, (hd + 1) * LANES)
        k_ref[:, sl] = (kk[:, sl] + kslab).astype(BF16)
    vt = lax.dot_general(wvt_ref[...], kvb, (((1,), (1,)), ((), ())), preferred_element_type=F32)
    row = lax.broadcasted_iota(jnp.int32, (HEAD_SLABS, 1), 0)
    vt_ref[...] = (vt + (row % LANES == ONE_LANE).astype(F32)).astype(BF16)


def _premix_kernel(x_ref, mod_ref, cos_ref, sin_ref, hist_ref, gpre_ref, gqa_ref, gkv_ref, pscale_ref,
                   win_ref, wq_ref, wk_ref, wvt_ref, wpool_ref,
                   q_ref, k_ref, vt_ref, kv_ref, kpe_ref, pooled_ref, poolnew_ref,
                   ubuf, *, tm, pos0):
    t = pl.program_id(1)
    x = x_ref[...]
    h = (_rms(x, gpre_ref[...]) * (1.0 + mod_ref[1:2, :]) + mod_ref[0:1, :]).astype(BF16)
    z = jnp.dot(h, win_ref[...], preferred_element_type=F32)
    cosv = cos_ref[...]
    sinv = sin_ref[...]

    qan = _rms(z[:, :Q_LORA], gqa_ref[...]).astype(BF16)
    qq = jnp.dot(qan, wq_ref[...], preferred_element_type=F32)
    for hd in range(N_HEADS):
        a = qq[:, hd * LANES:(hd + 1) * LANES]
        b = qq[:, HEAD_SLABS + hd * LANES:HEAD_SLABS + (hd + 1) * LANES]
        q_ref[:, hd * LANES:(hd + 1) * LANES] = ((a * cosv + b * sinv) * Q_SCALE).astype(BF16)

    kvn = _rms(z[:, Q_LORA:U_OFF], gkv_ref[...])
    kv_ref[...] = kvn
    kslab = z[:, KPE_OFF:KPE_OFF + LANES] * cosv + z[:, KPE_OFF + LANES:KPE_OFF + 2 * LANES] * sinv
    kpe_ref[...] = kslab
    _store_kv(kvn.astype(BF16), kslab, wk_ref, wvt_ref, k_ref, vt_ref)

    @pl.when(t == 0)
    def _():
        ubuf[0:HIST_ROWS, :] = hist_ref[...]

    @pl.when(t > 0)
    def _():
        ubuf[0:HIST_ROWS, :] = ubuf[tm:tm + HIST_ROWS, :]

    ubuf[HIST_ROWS:HIST_ROWS + tm, :] = z[:, U_OFF:KPE_OFF]
    pos = pos0 + t * tm + lax.broadcasted_iota(jnp.int32, (tm, 1), 0)
    for g, w in enumerate(POOL_WINDOWS):
        sl = slice(g * POOL_GROUP_DIM, (g + 1) * POOL_GROUP_DIM)
        u = ubuf[HIST_ROWS:HIST_ROWS + tm, sl]
        acc = u
        for j in range(1, w):
            acc = acc + ubuf[HIST_ROWS - j:HIST_ROWS - j + tm, sl]
        cnt = jnp.minimum(pos + 1, w).astype(F32)
        d = (acc / cnt - u).astype(BF16)
        y = jnp.dot(d, wpool_ref[g], preferred_element_type=F32) * pscale_ref[:, sl]
        pooled_ref[:, sl] = y.astype(BF16)
    poolnew_ref[...] = ubuf[tm:tm + HIST_ROWS, :]


def _premix(x, mod, cos_t, sin_t, hist, w, *, tm, pos0):
    nb, nt, _ = x.shape
    grid = (nb, nt // tm)
    tok = lambda width: pl.BlockSpec((None, tm, width), lambda b, t: (b, t, 0))
    full = lambda a: pl.BlockSpec(a.shape, lambda b, t: (0,) * a.ndim)
    tab = pl.BlockSpec((tm, LANES), lambda b, t: (t, 0))
    perb = lambda rows, width: pl.BlockSpec((None, rows, width), lambda b, t: (b, 0, 0))
    outs = [
        jax.ShapeDtypeStruct((nb, nt, HEAD_SLABS), BF16),
        jax.ShapeDtypeStruct((nb, nt, HEAD_SLABS), BF16),
        jax.ShapeDtypeStruct((nb, HEAD_SLABS, nt), BF16),
        jax.ShapeDtypeStruct((nb, nt, KV_LORA), F32),
        jax.ShapeDtypeStruct((nb, nt, LANES), F32),
        jax.ShapeDtypeStruct((nb, nt, POOL_WIDTH), BF16),
        jax.ShapeDtypeStruct((nb, HIST_ROWS, POOL_WIDTH), F32),
    ]
    return pl.pallas_call(
        functools.partial(_premix_kernel, tm=tm, pos0=pos0),
        out_shape=outs,
        grid=grid,
        in_specs=[tok(D_MODEL), perb(N_MOD, D_MODEL), tab, tab, perb(HIST_ROWS, POOL_WIDTH),
                  full(w["g_pre_mix"]), full(w["g_q_a"]), full(w["g_kv_a"]), full(w["pool_scale"]),
                  full(w["w_in"]), full(w["w_q"]), full(w["w_k"]), full(w["w_vt"]), full(w["w_pool"])],
        out_specs=[tok(HEAD_SLABS), tok(HEAD_SLABS),
                   pl.BlockSpec((None, HEAD_SLABS, tm), lambda b, t: (b, 0, t)),
                   tok(KV_LORA), tok(LANES), tok(POOL_WIDTH), perb(HIST_ROWS, POOL_WIDTH)],
        scratch_shapes=[pltpu.VMEM((HIST_ROWS + tm, POOL_WIDTH), F32)],
        compiler_params=_cparams(2),
        name="premix",
    )(x, mod, cos_t, sin_t, hist, w["g_pre_mix"], w["g_q_a"], w["g_kv_a"], w["pool_scale"],
      w["w_in"], w["w_q"], w["w_k"], w["w_vt"], w["w_pool"])


def _kvproj_kernel(kv_ref, kpe_ref, wk_ref, wvt_ref, k_ref, vt_ref):
    _store_kv(kv_ref[...].astype(BF16), kpe_ref[...], wk_ref, wvt_ref, k_ref, vt_ref)


def _kvproj(kv, kpe_slab, w_k, w_vt, *, tm):
    nb, nt, _ = kv.shape
    tok = lambda width: pl.BlockSpec((None, tm, width), lambda b, t: (b, t, 0))
    full = lambda a: pl.BlockSpec(a.shape, lambda b, t: (0,) * a.ndim)
    return pl.pallas_call(
        _kvproj_kernel,
        out_shape=[jax.ShapeDtypeStruct((nb, nt, HEAD_SLABS), BF16),
                   jax.ShapeDtypeStruct((nb, HEAD_SLABS, nt), BF16)],
        grid=(nb, nt // tm),
        in_specs=[tok(KV_LORA), tok(LANES), full(w_k), full(w_vt)],
        out_specs=[tok(HEAD_SLABS), pl.BlockSpec((None, HEAD_SLABS, tm), lambda b, t: (b, 0, t))],
        compiler_params=_cparams(2),
        name="kvproj",
    )(kv, kpe_slab, w_k, w_vt)


def _attn_kernel(q_ref, k_ref, vt_ref, o_ref, m_ref, acc_ref, *, tq, tk, nk, causal, kv_len):
    qi = pl.program_id(1)
    ki = pl.program_id(2)
    last = ((qi + 1) * tq - 1) // tk if causal else nk - 1
    need_len_mask = kv_len < nk * tk

    @pl.when(ki == 0)
    def _():
        m_ref[...] = jnp.full(m_ref.shape, NEG_INF, F32)
        acc_ref[...] = jnp.zeros(acc_ref.shape, F32)

    def step(masked):
        if masked:
            kpos = ki * tk + lax.broadcasted_iota(jnp.int32, (tk, tq), 0)
            vis = None
            if causal:
                qpos = qi * tq + lax.broadcasted_iota(jnp.int32, (tk, tq), 1)
                vis = (kpos // CHUNK) <= (qpos // CHUNK)
            if need_len_mask:
                lm = kpos < kv_len
                vis = lm if vis is None else (vis & lm)

        def scores(hd):
            sl = slice(hd * LANES, (hd + 1) * LANES)
            return lax.dot_general(k_ref[:, sl], q_ref[:, sl], (((1,), (1,)), ((), ())),
                                   preferred_element_type=F32)

        def accumulate(hd, alpha, p):
            sl = slice(hd * LANES, (hd + 1) * LANES)
            acc_ref[hd] = alpha * acc_ref[hd] + jnp.dot(vt_ref[sl, :], p, preferred_element_type=F32)

        s_next = scores(0)
        pending = None
        for hd in range(N_HEADS):
            s = s_next
            if hd + 1 < N_HEADS:
                s_next = scores(hd + 1)
            if pending is not None:
                accumulate(*pending)
            if masked:
                s = jnp.where(vis, s, NEG_INF)
            m_prev = m_ref[hd:hd + 1, :]
            m_new = jnp.maximum(m_prev, jnp.max(s, axis=0, keepdims=True))
            m_ref[hd:hd + 1, :] = m_new
            pending = (hd, jnp.exp2(m_prev - m_new), jnp.exp2(s - m_new).astype(BF16))
        accumulate(*pending)

    if causal or need_len_mask:
        @pl.when(ki < last)
        def _():
            step(False)

        @pl.when(ki == last)
        def _():
            step(True)
    else:
        step(False)

    @pl.when(ki == nk - 1)
    def _():
        for hd in range(N_HEADS):
            acc = acc_ref[hd]
            out_t = acc / acc[ONE_LANE:ONE_LANE + 1, :]
            o_ref[:, hd * LANES:(hd + 1) * LANES] = out_t.T.astype(BF16)


def _attention(q, k, vt, *, tq, tk, causal, kv_len):
    nb, nq_tot, _ = q.shape
    nk = k.shape[1] // tk
    nq = nq_tot // tq
    if causal:
        last = lambda i: ((i + 1) * tq - 1) // tk
        kmap = lambda b, i, j: (b, jnp.minimum(j, last(i)), 0)
        vmap = lambda b, i, j: (b, 0, jnp.minimum(j, last(i)))
    else:
        kmap = lambda b, i, j: (b, j, 0)
        vmap = lambda b, i, j: (b, 0, j)
    return pl.pallas_call(
        functools.partial(_attn_kernel, tq=tq, tk=tk, nk=nk, causal=causal, kv_len=kv_len),
        out_shape=jax.ShapeDtypeStruct((nb, nq_tot, HEAD_SLABS), BF16),
        grid=(nb, nq, nk),
        in_specs=[pl.BlockSpec((None, tq, HEAD_SLABS), lambda b, i, j: (b, i, 0)),
                  pl.BlockSpec((None, tk, HEAD_SLABS), kmap),
                  pl.BlockSpec((None, HEAD_SLABS, tk), vmap)],
        out_specs=pl.BlockSpec((None, tq, HEAD_SLABS), lambda b, i, j: (b, i, 0)),
        scratch_shapes=[pltpu.VMEM((N_HEADS, tq), F32), pltpu.VMEM((N_HEADS, LANES, tq), F32)],
        compiler_params=_cparams(3),
        name="attention",
    )(q, k, vt)


def _postmix_kernel(attn_ref, pooled_ref, x_ref, mod_ref, cnt0_ref, gpost_ref, gffn_ref,
                    woa_ref, wop_ref, wr_ref, br_ref,
                    x1_ref, h2_ref, ri_ref, rg_ref, cnt_ref,
                    carry, ltri, *, tm):
    first = (pl.program_id(0) == 0) & (pl.program_id(1) == 0)

    @pl.when(first)
    def _():
        carry[...] = cnt0_ref[...]
        r = lax.broadcasted_iota(jnp.int32, (tm, tm), 0)
        c = lax.broadcasted_iota(jnp.int32, (tm, tm), 1)
        ltri[...] = (r > c).astype(BF16)

    mix = (jnp.dot(attn_ref[...], woa_ref[...], preferred_element_type=F32)
           + jnp.dot(pooled_ref[...], wop_ref[...], preferred_element_type=F32))
    x1 = x_ref[...] + mod_ref[2:3, :] * _rms(mix, gpost_ref[...])
    x1_ref[...] = x1
    h2 = _rms(x1, gffn_ref[...]) * (1.0 + mod_ref[4:5, :]) + mod_ref[3:4, :]
    h2_ref[...] = h2

    lane = lax.broadcasted_iota(jnp.int32, (tm, LANES), 1).astype(F32)
    logits = jnp.dot(h2.astype(BF16), wr_ref[...], preferred_element_type=F32) + br_ref[...]
    logits = jnp.where(lane < N_EXPERTS, logits, NEG_INF)
    sel = jnp.zeros((tm, LANES), F32)
    ids, vals, hots = [], [], []
    for _ in range(TOP_K):
        mk = jnp.max(logits, axis=1, keepdims=True)
        idx = jnp.min(jnp.where(logits == mk, lane, float(LANES)), axis=1, keepdims=True)
        hot = lane == idx
        logits = jnp.where(hot, NEG_INF, logits)
        sel = sel + hot.astype(F32)
        ids.append(idx)
        vals.append(mk)
        hots.append(hot)
    ex = [jnp.exp(vk - vals[0]) for vk in vals]
    denom = ex[0] + ex[1] + ex[2] + ex[3]

    before = jnp.dot(ltri[...], sel.astype(BF16), preferred_element_type=F32) + carry[0:1, :]
    ri = jnp.zeros((tm, LANES), F32)
    rg = jnp.zeros((tm, LANES), F32)
    for kk in range(TOP_K):
        rank = jnp.sum(jnp.where(hots[kk], before, 0.0), axis=1, keepdims=True)
        ri = jnp.where(lane == kk, ids[kk], ri)
        ri = jnp.where(lane == TOP_K + kk, rank, ri)
        rg = jnp.where(lane == kk, ex[kk] / denom, rg)
    ri_ref[...] = ri.astype(jnp.int32)
    rg_ref[...] = rg
    carry[0:1, :] = carry[0:1, :] + jnp.sum(sel, axis=0, keepdims=True)
    cnt_ref[...] = carry[...]


def _postmix(attn, pooled, x, mod, cnt0, w, *, tm):
    nb, nt, _ = x.shape
    tok = lambda width: pl.BlockSpec((None, tm, width), lambda b, t: (b, t, 0))
    full = lambda a: pl.BlockSpec(a.shape, lambda b, t: (0,) * a.ndim)
    outs = [
        jax.ShapeDtypeStruct((nb, nt, D_MODEL), F32),
        jax.ShapeDtypeStruct((nb, nt, D_MODEL), F32),
        jax.ShapeDtypeStruct((nb, nt, LANES), jnp.int32),
        jax.ShapeDtypeStruct((nb, nt, LANES), F32),
        jax.ShapeDtypeStruct((8, LANES), F32),
    ]
    return pl.pallas_call(
        functools.partial(_postmix_kernel, tm=tm),
        out_shape=outs,
        grid=(nb, nt // tm),
        in_specs=[tok(HEAD_SLABS), tok(POOL_WIDTH), tok(D_MODEL),
                  pl.BlockSpec((None, N_MOD, D_MODEL), lambda b, t: (b, 0, 0)),
                  full(cnt0), full(w["g_post_mix"]), full(w["g_pre_ffn"]),
                  full(w["w_o_attn"]), full(w["w_o_pool"]), full(w["w_router"]), full(w["b_router"])],
        out_specs=[tok(D_MODEL), tok(D_MODEL), tok(LANES), tok(LANES),
                   pl.BlockSpec((8, LANES), lambda b, t: (0, 0))],
        scratch_shapes=[pltpu.VMEM((8, LANES), F32), pltpu.VMEM((tm, tm), BF16)],
        compiler_params=_cparams(2),
        name="postmix",
    )(attn, pooled, x, mod, cnt0, w["g_post_mix"], w["g_pre_ffn"],
      w["w_o_attn"], w["w_o_pool"], w["w_router"], w["b_router"])


def _dispatch_rows(dest_ref, h_ref, xs_ref, sem, tm):
    def issue(i, carry):
        for kk in range(TOP_K):
            pltpu.make_async_copy(h_ref.at[pl.ds(i, 1), :],
                                  xs_ref.at[pl.ds(dest_ref[0, 0, i * TOP_K + kk], 1), :], sem).start()
        return carry

    lax.fori_loop(0, tm, issue, 0)
    for _ in range(TOP_K):
        pltpu.make_async_copy(h_ref, xs_ref.at[pl.ds(0, tm), :], sem).wait()


def _dispatch_first_kernel(zfrom_ref, dest_ref, h_ref, xs_ref, zbuf, sem, zsem, *, tm):
    @pl.when(pl.program_id(0) == 0)
    def _():
        zbuf[...] = jnp.zeros(zbuf.shape, F32)
        for e in range(N_EXPERTS):
            start = pl.multiple_of(zfrom_ref[e], SUBLANES)
            cp = pltpu.make_async_copy(zbuf, xs_ref.at[pl.ds(start, ZERO_ROWS), :], zsem)
            cp.start()
            cp.wait()

    _dispatch_rows(dest_ref, h_ref, xs_ref, sem, tm)


def _dispatch_more_kernel(dest_ref, h_ref, xs_in_ref, xs_ref, sem, *, tm):
    del xs_in_ref
    _dispatch_rows(dest_ref, h_ref, xs_ref, sem, tm)


def _dispatch_first(zfrom, dest, h, n_rows, *, tm):
    nt = h.shape[0] // tm
    return pl.pallas_call(
        functools.partial(_dispatch_first_kernel, tm=tm),
        out_shape=jax.ShapeDtypeStruct((n_rows, D_MODEL), F32),
        grid_spec=pltpu.PrefetchScalarGridSpec(
            num_scalar_prefetch=1,
            grid=(nt,),
            in_specs=[pl.BlockSpec((1, 1, tm * TOP_K), lambda i, z: (i, 0, 0), memory_space=pltpu.SMEM),
                      pl.BlockSpec((tm, D_MODEL), lambda i, z: (i, 0))],
            out_specs=pl.BlockSpec(memory_space=pl.ANY),
            scratch_shapes=[pltpu.VMEM((ZERO_ROWS, D_MODEL), F32),
                            pltpu.SemaphoreType.DMA(()), pltpu.SemaphoreType.DMA(())],
        ),
        compiler_params=_cparams(1),
        name="dispatch_first",
    )(zfrom, dest.reshape(nt, 1, tm * TOP_K), h)


def _dispatch_more(dest, h, xs, *, tm):
    nt = h.shape[0] // tm
    return pl.pallas_call(
        functools.partial(_dispatch_more_kernel, tm=tm),
        out_shape=jax.ShapeDtypeStruct(xs.shape, xs.dtype),
        grid=(nt,),
        in_specs=[pl.BlockSpec((1, 1, tm * TOP_K), lambda i: (i, 0, 0), memory_space=pltpu.SMEM),
                  pl.BlockSpec((tm, D_MODEL), lambda i: (i, 0)),
                  pl.BlockSpec(memory_space=pl.ANY)],
        out_specs=pl.BlockSpec(memory_space=pl.ANY),
        scratch_shapes=[pltpu.SemaphoreType.DMA(())],
        input_output_aliases={2: 0},
        compiler_params=_cparams(1),
        name="dispatch_more",
    )(dest.reshape(nt, 1, tm * TOP_K), h, xs)


def _expert_kernel(be_ref, nu_ref, xs_ref, wgu_ref, bgu_ref, wd_ref, bd_ref, ys_ref, wgu_b, wd_b):
    i = pl.program_id(0)

    @pl.when(i < nu_ref[0])
    def _():
        @pl.when((i == 0) | (be_ref[i] != be_ref[jnp.maximum(i - 1, 0)]))
        def _():
            wgu_b[...] = wgu_ref[...].astype(BF16)
            wd_b[...] = wd_ref[...].astype(BF16)

        x = xs_ref[...].astype(BF16)
        gu = jnp.dot(x, wgu_b[...], preferred_element_type=F32) + bgu_ref[...]
        g = jnp.minimum(gu[:, :D_FF], SWIGLU_LIMIT)
        u = jnp.clip(gu[:, D_FF:], -SWIGLU_LIMIT, SWIGLU_LIMIT)
        a = (u + 1.0) * (g * jax.nn.sigmoid(SWIGLU_ALPHA * g))
        ys_ref[...] = jnp.dot(a.astype(BF16), wd_b[...], preferred_element_type=F32) + bd_ref[...]


def _experts(block_e, n_used, xs, w):
    nblk = xs.shape[0] // MOE_BLOCK
    row = lambda i, be, nu: (jnp.minimum(i, nu[0] - 1), 0)
    per_e = lambda i, be, nu: (be[i], 0, 0)
    return pl.pallas_call(
        _expert_kernel,
        out_shape=jax.ShapeDtypeStruct(xs.shape, F32),
        grid_spec=pltpu.PrefetchScalarGridSpec(
            num_scalar_prefetch=2,
            grid=(nblk,),
            in_specs=[pl.BlockSpec((MOE_BLOCK, D_MODEL), row),
                      pl.BlockSpec((None, D_MODEL, 2 * D_FF), per_e),
                      pl.BlockSpec((None, 1, 2 * D_FF), per_e),
                      pl.BlockSpec((None, D_FF, D_MODEL), per_e),
                      pl.BlockSpec((None, 1, D_MODEL), per_e)],
            out_specs=pl.BlockSpec((MOE_BLOCK, D_MODEL), row),
            scratch_shapes=[pltpu.VMEM((D_MODEL, 2 * D_FF), BF16), pltpu.VMEM((D_FF, D_MODEL), BF16)],
        ),
        compiler_params=_cparams(1),
        name="experts",
    )(block_e, n_used, xs, w["w_gu"], w["b_gu"], w["w_down"], w["b_down"])


def _combine_kernel(dcur_ref, dnext_ref, x1_ref, rg_ref, mod_ref, gpost_ref, ys_ref, o_ref, gbuf, sem, *, tm, n):
    i = pl.program_id(0)
    slot = i % 2

    def gather(dest_ref, sl):
        def issue(r, carry):
            for kk in range(TOP_K):
                pltpu.make_async_copy(ys_ref.at[pl.ds(dest_ref[0, 0, r * TOP_K + kk], 1), :],
                                      gbuf.at[sl, kk, pl.ds(r, 1), :], sem.at[sl]).start()
            return carry

        lax.fori_loop(0, tm, issue, 0)

    @pl.when(i == 0)
    def _():
        gather(dcur_ref, 0)

    @pl.when(i + 1 < n)
    def _():
        gather(dnext_ref, 1 - slot)

    for kk in range(TOP_K):
        pltpu.make_async_copy(ys_ref.at[pl.ds(0, tm), :], gbuf.at[slot, kk], sem.at[slot]).wait()

    rg = rg_ref[...]
    y = rg[:, 0:1] * gbuf[slot, 0]
    for kk in range(1, TOP_K):
        y = y + rg[:, kk:kk + 1] * gbuf[slot, kk]
    o_ref[...] = x1_ref[...] + mod_ref[5:6, :] * _rms(y, gpost_ref[...])


def _combine(dest, x1, rg, mod, g_post, ys, *, tm):
    nb, nt, _ = x1.shape
    ntile = nt // tm
    n = nb * ntile
    tok = lambda width: pl.BlockSpec((tm, width), lambda i: (i, 0))
    dest = dest.reshape(n, 1, tm * TOP_K)
    dspec = lambda f: pl.BlockSpec((1, 1, tm * TOP_K), lambda i: (f(i), 0, 0), memory_space=pltpu.SMEM)
    out = pl.pallas_call(
        functools.partial(_combine_kernel, tm=tm, n=n),
        out_shape=jax.ShapeDtypeStruct((nb * nt, D_MODEL), F32),
        grid=(n,),
        in_specs=[dspec(lambda i: i), dspec(lambda i: jnp.minimum(i + 1, n - 1)),
                  tok(D_MODEL), tok(LANES),
                  pl.BlockSpec((None, N_MOD, D_MODEL), lambda i: (i // ntile, 0, 0)),
                  pl.BlockSpec(g_post.shape, lambda i: (0, 0)),
                  pl.BlockSpec(memory_space=pl.ANY)],
        out_specs=tok(D_MODEL),
        scratch_shapes=[pltpu.VMEM((2, TOP_K, tm, D_MODEL), F32), pltpu.SemaphoreType.DMA((2,))],
        compiler_params=_cparams(1),
        name="combine",
    )(dest, dest, x1.reshape(nb * nt, D_MODEL), rg.reshape(nb * nt, LANES), mod, g_post, ys)
    return out.reshape(x1.shape)


def _rot_swap(w):
    half = QK_ROPE // 2
    return jnp.concatenate([-w[..., half:], w[..., :half]], axis=-1)


def _prep_weights(w_in, g_q_a, w_q_b, g_kv_a, w_uk, w_uv, w_pool, pool_scale, w_o, g_pre_mix, g_post_mix,
                  g_pre_ffn, w_router, b_router, w_gu, b_gu, w_down, b_down, g_post_ffn):
    row = lambda a: a.reshape(1, -1).astype(F32)
    w_kpe = w_in[:, U_OFF:U_OFF + QK_ROPE]
    zeros = lambda *s: jnp.zeros(s, F32)
    d = D_MODEL
    slab = lambda a: jnp.concatenate([zeros(d, QK_NOPE), a, zeros(d, LANES - QK_HEAD)], axis=1)
    w_in_ext = jnp.concatenate([w_in[:, :U_OFF], w_in[:, U_OFF + QK_ROPE:], slab(w_kpe), slab(_rot_swap(w_kpe))],
                               axis=1)
    pad_q = zeros(Q_LORA, N_HEADS, LANES - QK_HEAD)
    wq_plain = jnp.concatenate([w_q_b, pad_q], axis=2).reshape(Q_LORA, HEAD_SLABS)
    wq_swap = jnp.concatenate([zeros(Q_LORA, N_HEADS, QK_NOPE), _rot_swap(w_q_b[..., QK_NOPE:]), pad_q],
                              axis=2).reshape(Q_LORA, HEAD_SLABS)
    pad_kv = zeros(KV_LORA, N_HEADS, LANES - QK_NOPE)
    wk = jnp.concatenate([w_uk, pad_kv], axis=2).reshape(KV_LORA, HEAD_SLABS)
    wv = jnp.concatenate([w_uv, pad_kv], axis=2).reshape(KV_LORA, HEAD_SLABS)
    mla_w = N_HEADS * V_HEAD
    woa = jnp.concatenate([w_o[:mla_w].reshape(N_HEADS, V_HEAD, d), zeros(N_HEADS, LANES - V_HEAD, d)],
                          axis=1).reshape(HEAD_SLABS, d)
    return {
        "g_pre_mix": row(g_pre_mix), "g_q_a": row(g_q_a), "g_kv_a": row(g_kv_a), "pool_scale": row(pool_scale),
        "w_in": w_in_ext.astype(BF16),
        "w_q": jnp.concatenate([wq_plain, wq_swap], axis=1).astype(BF16),
        "w_k": wk.astype(BF16), "w_vt": wv.T.astype(BF16),
        "w_pool": w_pool.astype(BF16),
        "g_post_mix": row(g_post_mix), "g_pre_ffn": row(g_pre_ffn), "g_post_ffn": row(g_post_ffn),
        "w_o_attn": woa.astype(BF16), "w_o_pool": w_o[mla_w:].astype(BF16),
        "w_router": jnp.pad(w_router, ((0, 0), (0, LANES - N_EXPERTS))).astype(BF16),
        "b_router": jnp.pad(b_router, (0, LANES - N_EXPERTS)).reshape(1, LANES).astype(F32),
        "w_gu": w_gu, "b_gu": b_gu.reshape(N_EXPERTS, 1, 2 * D_FF).astype(F32),
        "w_down": w_down, "b_down": b_down.reshape(N_EXPERTS, 1, D_MODEL).astype(F32),
    }


def _rope_tables(pos):
    half = QK_ROPE // 2
    inv = ROPE_THETA ** (-jnp.arange(half, dtype=F32) / half)
    ang = pos.astype(F32)[:, None] * inv[None, :]
    cos, sin = jnp.cos(ang), jnp.sin(ang)
    n = pos.shape[0]
    cos_t = jnp.concatenate([jnp.ones((n, QK_NOPE), F32), cos, cos, jnp.zeros((n, LANES - QK_HEAD), F32)], axis=1)
    sin_t = jnp.concatenate([jnp.zeros((n, QK_NOPE), F32), sin, sin, jnp.zeros((n, LANES - QK_HEAD), F32)], axis=1)
    return cos_t, sin_t


def _tile(n, pref):
    return pref if n % pref == 0 else n


def _mixer_path(x, mod, pos0, hist, cache, w, cnt0):
    nb, nt, _ = x.shape
    tm = _tile(nt, 512)
    cos_t, sin_t = _rope_tables(pos0 + jnp.arange(nt, dtype=jnp.int32))
    q, k, vt, kv_new, kslab, pooled, pool_tail = _premix(x, mod, cos_t, sin_t, hist, w, tm=tm, pos0=pos0)
    if cache is None:
        attn = _attention(q, k, vt, tq=tm, tk=_tile(nt, 2 * tm), causal=True, kv_len=nt)
    else:
        ckv, ckpe = cache
        past = ckv.shape[1]
        ckpe_slab = jnp.pad(ckpe, ((0, 0), (0, 0), (QK_NOPE, LANES - QK_HEAD)))
        kc, vtc = _kvproj(ckv, ckpe_slab, w["w_k"], w["w_vt"], tm=_tile(past, 512))
        kv_len = past + nt
        tk = -(-kv_len // 256) * 256
        k_all = jnp.concatenate([kc, k, jnp.zeros((nb, tk - kv_len, HEAD_SLABS), BF16)], axis=1)
        vt_all = jnp.concatenate([vtc, vt, jnp.zeros((nb, HEAD_SLABS, tk - kv_len), BF16)], axis=2)
        tq = -(-nt // LANES) * LANES
        q_pad = jnp.pad(q, ((0, 0), (0, tq - nt), (0, 0)))
        attn = _attention(q_pad, k_all, vt_all, tq=tq, tk=tk, causal=False, kv_len=kv_len)[:, :nt]
    x1, h2, ri, rg, cnt = _postmix(attn, pooled, x, mod, cnt0, w, tm=tm)
    return x1, h2, ri, rg, cnt, kv_new, kslab[..., QK_NOPE:QK_HEAD], pool_tail[:, 1:]


def kernel(x_prompt, x_sample, c_prompt, c_sample, cache_kv_latent, cache_k_rope, state_pool, w_ada, b_ada,
           g_pre_mix, w_in, g_q_a, w_q_b, g_kv_a, w_uk, w_uv, w_pool, pool_scale, w_o, g_post_mix, g_pre_ffn,
           w_router, b_router, w_gu, b_gu, w_down, b_down, g_post_ffn):
    assert w_ada.shape[0] == 1, "single-layer step"
    bp, sp, _ = x_prompt.shape
    bs, ss, _ = x_sample.shape
    past = cache_kv_latent.shape[2]
    w = _prep_weights(w_in[0], g_q_a[0], w_q_b[0], g_kv_a[0], w_uk[0], w_uv[0], w_pool[0], pool_scale[0], w_o[0],
                      g_pre_mix[0], g_post_mix[0], g_pre_ffn[0], w_router[0], b_router[0], w_gu[0], b_gu[0],
                      w_down[0], b_down[0], g_post_ffn[0])

    mod = _ada(jnp.concatenate([c_prompt, c_sample], axis=0), w_ada[0], b_ada[0])
    mod = mod.reshape(bp + bs, N_MOD, D_MODEL)
    mod_p, mod_s = mod[:bp], mod[bp:]

    hist_p = jnp.zeros((bp, HIST_ROWS, POOL_WIDTH), F32)
    hist_s = jnp.pad(state_pool[0], ((0, 0), (1, 0), (0, 0)))
    cnt0 = jnp.zeros((8, LANES), F32)
    x1p, h2p, rip, rgp, cntp, kv_p, kpe_p, pool_p = _mixer_path(x_prompt, mod_p, 0, hist_p, None, w, cnt0)
    x1s, h2s, ris, rgs, cnts, kv_s, kpe_s, pool_s = _mixer_path(
        x_sample, mod_s, past, hist_s, (cache_kv_latent[0], cache_k_rope[0]), w, cntp)

    counts = cnts[0, :N_EXPERTS].astype(jnp.int32)
    padded = (counts + MOE_BLOCK - 1) // MOE_BLOCK * MOE_BLOCK
    pad_end = jnp.cumsum(padded)
    pad_start = pad_end - padded
    n_tok = bp * sp + bs * ss
    n_blocks = -(-(n_tok * TOP_K) // MOE_BLOCK) + N_EXPERTS
    block_row = jnp.arange(n_blocks, dtype=jnp.int32) * MOE_BLOCK
    block_e = jnp.minimum(jnp.sum((pad_end[None, :] <= block_row[:, None]).astype(jnp.int32), axis=1),
                          N_EXPERTS - 1)
    n_used = (pad_end[-1:] // MOE_BLOCK).astype(jnp.int32)

    def dest_of(ri):
        ids = ri[..., :TOP_K]
        hot = ids[..., None] == jnp.arange(N_EXPERTS, dtype=jnp.int32)
        return ri[..., TOP_K:2 * TOP_K] + jnp.sum(jnp.where(hot, pad_start, 0), axis=-1)

    dest_p = dest_of(rip)
    dest_s = dest_of(ris)
    n_rows = n_blocks * MOE_BLOCK
    zfrom = jnp.minimum((pad_start + counts) // SUBLANES * SUBLANES, n_rows - ZERO_ROWS).astype(jnp.int32)
    xs = _dispatch_first(zfrom, dest_p, h2p.reshape(bp * sp, D_MODEL), n_rows, tm=512)
    xs = _dispatch_more(dest_s, h2s.reshape(bs * ss, D_MODEL), xs, tm=_tile(bs * ss, 512))
    ys = _experts(block_e, n_used, xs, w)
    y_p = _combine(dest_p, x1p, rgp, mod_p, w["g_post_ffn"], ys, tm=_tile(sp, 256))
    y_s = _combine(dest_s, x1s, rgs, mod_s, w["g_post_ffn"], ys, tm=_tile(ss, 256))
    return (y_p, y_s, kv_p[None], kpe_p[None], pool_p[None], kv_s[None], kpe_s[None], pool_s[None])
```

```python
import functools

import jax
import jax.numpy as jnp
from jax import lax
from jax.experimental import pallas as pl
from jax.experimental.pallas import tpu as pltpu

F32 = jnp.float32
BF16 = jnp.bfloat16

D_MODEL = 1024
CHUNK = 64
N_HEADS = 8
QK_NOPE = 64
QK_ROPE = 32
QK_HEAD = QK_NOPE + QK_ROPE
V_HEAD = 64
Q_LORA = 384
KV_LORA = 256
ROPE_THETA = 10000.0
POOL_WINDOWS = (2, 4, 8, 16)
POOL_GROUP_DIM = 128
POOL_WIDTH = POOL_GROUP_DIM * len(POOL_WINDOWS)
POOL_HIST = max(POOL_WINDOWS) - 1
HIST_ROWS = POOL_HIST + 1
N_EXPERTS = 32
TOP_K = 4
D_FF = 1024
SWIGLU_LIMIT = 7.0
SWIGLU_ALPHA = 1.702
N_MOD = 6
EPS = 1e-6

LANES = 128
HEAD_SLABS = N_HEADS * LANES
ONE_LANE = V_HEAD
IN_EXT = Q_LORA + KV_LORA + POOL_WIDTH + 2 * LANES
U_OFF = Q_LORA + KV_LORA
KPE_OFF = U_OFF + POOL_WIDTH
SM_SCALE = QK_HEAD ** -0.5
LOG2_E = 1.4426950408889634
Q_SCALE = SM_SCALE * LOG2_E
NEG_INF = float("-inf")

SUBLANES = 8
MOE_BLOCK = 512
ZERO_ROWS = MOE_BLOCK + SUBLANES
VMEM_LIMIT = 56 * 1024 * 1024


def _cparams(n_axes, vmem=VMEM_LIMIT):
    return pltpu.CompilerParams(dimension_semantics=("arbitrary",) * n_axes, vmem_limit_bytes=vmem)


def _rms(x, g):
    return x * lax.rsqrt(jnp.mean(x * x, axis=-1, keepdims=True) + EPS) * g


def _ada_kernel(c_ref, w_ref, b_ref, o_ref):
    c = c_ref[...]
    s = (c * jax.nn.sigmoid(c)).astype(BF16)
    o_ref[...] = jnp.dot(s, w_ref[...].astype(BF16), preferred_element_type=F32) + b_ref[...]


def _ada(c, w_ada, b_ada):
    nb = c.shape[0]
    return pl.pallas_call(
        _ada_kernel,
        out_shape=jax.ShapeDtypeStruct((nb, N_MOD * D_MODEL), F32),
        grid=(N_MOD,),
        in_specs=[pl.BlockSpec((nb, D_MODEL), lambda j: (0, 0)),
                  pl.BlockSpec((D_MODEL, D_MODEL), lambda j: (0, j)),
                  pl.BlockSpec((1, D_MODEL), lambda j: (0, j))],
        out_specs=pl.BlockSpec((nb, D_MODEL), lambda j: (0, j)),
        compiler_params=_cparams(1),
        name="ada",
    )(c, w_ada, b_ada.reshape(1, -1))


def _store_kv(kvb, kslab, wk_ref, wvt_ref, k_ref, vt_ref):
    kk = jnp.dot(kvb, wk_ref[...], preferred_element_type=F32)
    for hd in range(N_HEADS):
        sl = slice(hd * LANES, (hd + 1) * LANES)
        k_ref[:, sl] = (kk[:, sl] + kslab).astype(BF16)
    vt = lax.dot_general(wvt_ref[...], kvb, (((1,), (1,)), ((), ())), preferred_element_type=F32)
    row = lax.broadcasted_iota(jnp.int32, (HEAD_SLABS, 1), 0)
    vt_ref[...] = (vt + (row % LANES == ONE_LANE).astype(F32)).astype(BF16)


def _premix_kernel(x_ref, mod_ref, cos_ref, sin_ref, hist_ref, gpre_ref, gqa_ref, gkv_ref, pscale_ref,
                   win_ref, wq_ref, wk_ref, wvt_ref, wpool_ref,
                   q_ref, k_ref, vt_ref, kv_ref, kpe_ref, pooled_ref, poolnew_ref,
                   ubuf, *, tm, pos0):
    t = pl.program_id(1)
    x = x_ref[...]
    h = (_rms(x, gpre_ref[...]) * (1.0 + mod_ref[1:2, :]) + mod_ref[0:1, :]).astype(BF16)
    z = jnp.dot(h, win_ref[...], preferred_element_type=F32)
    cosv = cos_ref[...]
    sinv = sin_ref[...]

    qan = _rms(z[:, :Q_LORA], gqa_ref[...]).astype(BF16)
    qq = jnp.dot(qan, wq_ref[...], preferred_element_type=F32)
    for hd in range(N_HEADS):
        a = qq[:, hd * LANES:(hd + 1) * LANES]
        b = qq[:, HEAD_SLABS + hd * LANES:HEAD_SLABS + (hd + 1) * LANES]
        q_ref[:, hd * LANES:(hd + 1) * LANES] = ((a * cosv + b * sinv) * Q_SCALE).astype(BF16)

    kvn = _rms(z[:, Q_LORA:U_OFF], gkv_ref[...])
    kv_ref[...] = kvn
    kslab = z[:, KPE_OFF:KPE_OFF + LANES] * cosv + z[:, KPE_OFF + LANES:KPE_OFF + 2 * LANES] * sinv
    kpe_ref[...] = kslab
    _store_kv(kvn.astype(BF16), kslab, wk_ref, wvt_ref, k_ref, vt_ref)

    @pl.when(t == 0)
    def _():
        ubuf[0:HIST_ROWS, :] = hist_ref[...]

    @pl.when(t > 0)
    def _():
        ubuf[0:HIST_ROWS, :] = ubuf[tm:tm + HIST_ROWS, :]

    ubuf[HIST_ROWS:HIST_ROWS + tm, :] = z[:, U_OFF:KPE_OFF]
    pos = pos0 + t * tm + lax.broadcasted_iota(jnp.int32, (tm, 1), 0)
    for g, w in enumerate(POOL_WINDOWS):
        sl = slice(g * POOL_GROUP_DIM, (g + 1) * POOL_GROUP_DIM)
        u = ubuf[HIST_ROWS:HIST_ROWS + tm, sl]
        acc = u
        for j in range(1, w):
            acc = acc + ubuf[HIST_ROWS - j:HIST_ROWS - j + tm, sl]
        cnt = jnp.minimum(pos + 1, w).astype(F32)
        d = (acc / cnt - u).astype(BF16)
        y = jnp.dot(d, wpool_ref[g], preferred_element_type=F32) * pscale_ref[:, sl]
        pooled_ref[:, sl] = y.astype(BF16)
    poolnew_ref[...] = ubuf[tm:tm + HIST_ROWS, :]


def _premix(x, mod, cos_t, sin_t, hist, w, *, tm, pos0):
    nb, nt, _ = x.shape
    grid = (nb, nt // tm)
    tok = lambda width: pl.BlockSpec((None, tm, width), lambda b, t: (b, t, 0))
    full = lambda a: pl.BlockSpec(a.shape, lambda b, t: (0,) * a.ndim)
    tab = pl.BlockSpec((tm, LANES), lambda b, t: (t, 0))
    perb = lambda rows, width: pl.BlockSpec((None, rows, width), lambda b, t: (b, 0, 0))
    outs = [
        jax.ShapeDtypeStruct((nb, nt, HEAD_SLABS), BF16),
        jax.ShapeDtypeStruct((nb, nt, HEAD_SLABS), BF16),
        jax.ShapeDtypeStruct((nb, HEAD_SLABS, nt), BF16),
        jax.ShapeDtypeStruct((nb, nt, KV_LORA), F32),
        jax.ShapeDtypeStruct((nb, nt, LANES), F32),
        jax.ShapeDtypeStruct((nb, nt, POOL_WIDTH), BF16),
        jax.ShapeDtypeStruct((nb, HIST_ROWS, POOL_WIDTH), F32),
    ]
    return pl.pallas_call(
        functools.partial(_premix_kernel, tm=tm, pos0=pos0),
        out_shape=outs,
        grid=grid,
        in_specs=[tok(D_MODEL), perb(N_MOD, D_MODEL), tab, tab, perb(HIST_ROWS, POOL_WIDTH),
                  full(w["g_pre_mix"]), full(w["g_q_a"]), full(w["g_kv_a"]), full(w["pool_scale"]),
                  full(w["w_in"]), full(w["w_q"]), full(w["w_k"]), full(w["w_vt"]), full(w["w_pool"])],
        out_specs=[tok(HEAD_SLABS), tok(HEAD_SLABS),
                   pl.BlockSpec((None, HEAD_SLABS, tm), lambda b, t: (b, 0, t)),
                   tok(KV_LORA), tok(LANES), tok(POOL_WIDTH), perb(HIST_ROWS, POOL_WIDTH)],
        scratch_shapes=[pltpu.VMEM((HIST_ROWS + tm, POOL_WIDTH), F32)],
        compiler_params=_cparams(2),
        name="premix",
    )(x, mod, cos_t, sin_t, hist, w["g_pre_mix"], w["g_q_a"], w["g_kv_a"], w["pool_scale"],
      w["w_in"], w["w_q"], w["w_k"], w["w_vt"], w["w_pool"])


def _kvproj_kernel(kv_ref, kpe_ref, wk_ref, wvt_ref, k_ref, vt_ref):
    _store_kv(kv_ref[...].astype(BF16), kpe_ref[...], wk_ref, wvt_ref, k_ref, vt_ref)


def _kvproj(kv, kpe_slab, w_k, w_vt, *, tm):
    nb, nt, _ = kv.shape
    tok = lambda width: pl.BlockSpec((None, tm, width), lambda b, t: (b, t, 0))
    full = lambda a: pl.BlockSpec(a.shape, lambda b, t: (0,) * a.ndim)
    return pl.pallas_call(
        _kvproj_kernel,
        out_shape=[jax.ShapeDtypeStruct((nb, nt, HEAD_SLABS), BF16),
                   jax.ShapeDtypeStruct((nb, HEAD_SLABS, nt), BF16)],
        grid=(nb, nt // tm),
        in_specs=[tok(KV_LORA), tok(LANES), full(w_k), full(w_vt)],
        out_specs=[tok(HEAD_SLABS), pl.BlockSpec((None, HEAD_SLABS, tm), lambda b, t: (b, 0, t))],
        compiler_params=_cparams(2),
        name="kvproj",
    )(kv, kpe_slab, w_k, w_vt)


def _attn_kernel(q_ref, k_ref, vt_ref, o_ref, m_ref, acc_ref, *, tq, tk, nk, causal, kv_len):
    qi = pl.program_id(1)
    ki = pl.program_id(2)
    last = ((qi + 1) * tq - 1) // tk if causal else nk - 1
    need_len_mask = kv_len < nk * tk

    @pl.when(ki == 0)
    def _():
        m_ref[...] = jnp.full(m_ref.shape, NEG_INF, F32)
        acc_ref[...] = jnp.zeros(acc_ref.shape, F32)

    def step(masked):
        if masked:
            kpos = ki * tk + lax.broadcasted_iota(jnp.int32, (tk, tq), 0)
            vis = None
            if causal:
                qpos = qi * tq + lax.broadcasted_iota(jnp.int32, (tk, tq), 1)
                vis = (kpos // CHUNK) <= (qpos // CHUNK)
            if need_len_mask:
                lm = kpos < kv_len
                vis = lm if vis is None else (vis & lm)

        def scores(hd):
            sl = slice(hd * LANES, (hd + 1) * LANES)
            return lax.dot_general(k_ref[:, sl], q_ref[:, sl], (((1,), (1,)), ((), ())),
                                   preferred_element_type=F32)

        def accumulate(hd, alpha, p):
            sl = slice(hd * LANES, (hd + 1) * LANES)
            acc_ref[hd] = alpha * acc_ref[hd] + jnp.dot(vt_ref[sl, :], p, preferred_element_type=F32)

        s_next = scores(0)
        pending = None
        for hd in range(N_HEADS):
            s = s_next
            if hd + 1 < N_HEADS:
                s_next = scores(hd + 1)
            if pending is not None:
                accumulate(*pending)
            if masked:
                s = jnp.where(vis, s, NEG_INF)
            m_prev = m_ref[hd:hd + 1, :]
            m_new = jnp.maximum(m_prev, jnp.max(s, axis=0, keepdims=True))
            m_ref[hd:hd + 1, :] = m_new
            pending = (hd, jnp.exp2(m_prev - m_new), jnp.exp2(s - m_new).astype(BF16))
        accumulate(*pending)

    if causal or need_len_mask:
        @pl.when(ki < last)
        def _():
            step(False)

        @pl.when(ki == last)
        def _():
            step(True)
    else:
        step(False)

    @pl.when(ki == nk - 1)
    def _():
        for hd in range(N_HEADS):
            acc = acc_ref[hd]
            out_t = acc / acc[ONE_LANE:ONE_LANE + 1, :]
            o_ref[:, hd * LANES:(hd + 1) * LANES] = out_t.T.astype(BF16)


def _attention(q, k, vt, *, tq, tk, causal, kv_len):
    nb, nq_tot, _ = q.shape
    nk = k.shape[1] // tk
    nq = nq_tot // tq
    if causal:
        last = lambda i: ((i + 1) * tq - 1) // tk
        kmap = lambda b, i, j: (b, jnp.minimum(j, last(i)), 0)
        vmap = lambda b, i, j: (b, 0, jnp.minimum(j, last(i)))
    else:
        kmap = lambda b, i, j: (b, j, 0)
        vmap = lambda b, i, j: (b, 0, j)
    return pl.pallas_call(
        functools.partial(_attn_kernel, tq=tq, tk=tk, nk=nk, causal=causal, kv_len=kv_len),
        out_shape=jax.ShapeDtypeStruct((nb, nq_tot, HEAD_SLABS), BF16),
        grid=(nb, nq, nk),
        in_specs=[pl.BlockSpec((None, tq, HEAD_SLABS), lambda b, i, j: (b, i, 0)),
                  pl.BlockSpec((None, tk, HEAD_SLABS), kmap),
                  pl.BlockSpec((None, HEAD_SLABS, tk), vmap)],
        out_specs=pl.BlockSpec((None, tq, HEAD_SLABS), lambda b, i, j: (b, i, 0)),
        scratch_shapes=[pltpu.VMEM((N_HEADS, tq), F32), pltpu.VMEM((N_HEADS, LANES, tq), F32)],
        compiler_params=_cparams(3),
        name="attention",
    )(q, k, vt)


def _postmix_kernel(attn_ref, pooled_ref, x_ref, mod_ref, cnt0_ref, gpost_ref, gffn_ref,
                    woa_ref, wop_ref, wr_ref, br_ref,
                    x1_ref, h2_ref, ri_ref, rg_ref, cnt_ref,
                    carry, ltri, *, tm):
    first = (pl.program_id(0) == 0) & (pl.program_id(1) == 0)

    @pl.when(first)
    def _():
        carry[...] = cnt0_ref[...]
        r = lax.broadcasted_iota(jnp.int32, (tm, tm), 0)
        c = lax.broadcasted_iota(jnp.int32, (tm, tm), 1)
        ltri[...] = (r > c).astype(BF16)

    mix = (jnp.dot(attn_ref[...], woa_ref[...], preferred_element_type=F32)
           + jnp.dot(pooled_ref[...], wop_ref[...], preferred_element_type=F32))
    x1 = x_ref[...] + mod_ref[2:3, :] * _rms(mix, gpost_ref[...])
    x1_ref[...] = x1
    h2 = _rms(x1, gffn_ref[...]) * (1.0 + mod_ref[4:5, :]) + mod_ref[3:4, :]
    h2_ref[...] = h2

    lane = lax.broadcasted_iota(jnp.int32, (tm, LANES), 1).astype(F32)
    logits = jnp.dot(h2.astype(BF16), wr_ref[...], preferred_element_type=F32) + br_ref[...]
    logits = jnp.where(lane < N_EXPERTS, logits, NEG_INF)
    sel = jnp.zeros((tm, LANES), F32)
    ids, vals, hots = [], [], []
    for _ in range(TOP_K):
        mk = jnp.max(logits, axis=1, keepdims=True)
        idx = jnp.min(jnp.where(logits == mk, lane, float(LANES)), axis=1, keepdims=True)
        hot = lane == idx
        logits = jnp.where(hot, NEG_INF, logits)
        sel = sel + hot.astype(F32)
        ids.append(idx)
        vals.append(mk)
        hots.append(hot)
    ex = [jnp.exp(vk - vals[0]) for vk in vals]
    denom = ex[0] + ex[1] + ex[2] + ex[3]

    before = jnp.dot(ltri[...], sel.astype(BF16), preferred_element_type=F32) + carry[0:1, :]
    ri = jnp.zeros((tm, LANES), F32)
    rg = jnp.zeros((tm, LANES), F32)
    for kk in range(TOP_K):
        rank = jnp.sum(jnp.where(hots[kk], before, 0.0), axis=1, keepdims=True)
        ri = jnp.where(lane == kk, ids[kk], ri)
        ri = jnp.where(lane == TOP_K + kk, rank, ri)
        rg = jnp.where(lane == kk, ex[kk] / denom, rg)
    ri_ref[...] = ri.astype(jnp.int32)
    rg_ref[...] = rg
    carry[0:1, :] = carry[0:1, :] + jnp.sum(sel, axis=0, keepdims=True)
    cnt_ref[...] = carry[...]


def _postmix(attn, pooled, x, mod, cnt0, w, *, tm):
    nb, nt, _ = x.shape
    tok = lambda width: pl.BlockSpec((None, tm, width), lambda b, t: (b, t, 0))
    full = lambda a: pl.BlockSpec(a.shape, lambda b, t: (0,) * a.ndim)
    outs = [
        jax.ShapeDtypeStruct((nb, nt, D_MODEL), F32),
        jax.ShapeDtypeStruct((nb, nt, D_MODEL), F32),
        jax.ShapeDtypeStruct((nb, nt, LANES), jnp.int32),
        jax.ShapeDtypeStruct((nb, nt, LANES), F32),
        jax.ShapeDtypeStruct((8, LANES), F32),
    ]
    return pl.pallas_call(
        functools.partial(_postmix_kernel, tm=tm),
        out_shape=outs,
        grid=(nb, nt // tm),
        in_specs=[tok(HEAD_SLABS), tok(POOL_WIDTH), tok(D_MODEL),
                  pl.BlockSpec((None, N_MOD, D_MODEL), lambda b, t: (b, 0, 0)),
                  full(cnt0), full(w["g_post_mix"]), full(w["g_pre_ffn"]),
                  full(w["w_o_attn"]), full(w["w_o_pool"]), full(w["w_router"]), full(w["b_router"])],
        out_specs=[tok(D_MODEL), tok(D_MODEL), tok(LANES), tok(LANES),
                   pl.BlockSpec((8, LANES), lambda b, t: (0, 0))],
        scratch_shapes=[pltpu.VMEM((8, LANES), F32), pltpu.VMEM((tm, tm), BF16)],
        compiler_params=_cparams(2),
        name="postmix",
    )(attn, pooled, x, mod, cnt0, w["g_post_mix"], w["g_pre_ffn"],
      w["w_o_attn"], w["w_o_pool"], w["w_router"], w["b_router"])


def _dispatch_rows(dest_ref, h_ref, xs_ref, sem, tm):
    def issue(j, carry):
        for s in range(SUBLANES):
            for kk in range(TOP_K):
                d = dest_ref[0, 0, j * (SUBLANES * TOP_K) + s * TOP_K + kk]
                pltpu.make_async_copy(h_ref.at[j, pl.ds(s, 1), :], xs_ref.at[pl.ds(d, 1), :], sem).start()
        return carry

    lax.fori_loop(0, tm // SUBLANES, issue, 0)
    for _ in range(TOP_K):
        pltpu.make_async_copy(xs_ref.at[pl.ds(0, tm), :], xs_ref.at[pl.ds(0, tm), :], sem).wait()


def _dispatch_first_kernel(zfrom_ref, dest_ref, h_ref, xs_ref, zbuf, sem, zsem, *, tm):
    @pl.when(pl.program_id(0) == 0)
    def _():
        zbuf[...] = jnp.zeros(zbuf.shape, F32)
        for e in range(N_EXPERTS):
            start = pl.multiple_of(zfrom_ref[e], SUBLANES)
            cp = pltpu.make_async_copy(zbuf, xs_ref.at[pl.ds(start, ZERO_ROWS), :], zsem)
            cp.start()
            cp.wait()

    _dispatch_rows(dest_ref, h_ref, xs_ref, sem, tm)


def _dispatch_more_kernel(dest_ref, h_ref, xs_in_ref, xs_ref, sem, *, tm):
    del xs_in_ref
    _dispatch_rows(dest_ref, h_ref, xs_ref, sem, tm)


def _dispatch_first(zfrom, dest, h, n_rows, *, tm):
    nt = h.shape[0] // tm
    return pl.pallas_call(
        functools.partial(_dispatch_first_kernel, tm=tm),
        out_shape=jax.ShapeDtypeStruct((n_rows, D_MODEL), F32),
        grid_spec=pltpu.PrefetchScalarGridSpec(
            num_scalar_prefetch=1,
            grid=(nt,),
            in_specs=[pl.BlockSpec((1, 1, tm * TOP_K), lambda i, z: (i, 0, 0), memory_space=pltpu.SMEM),
                      pl.BlockSpec((tm // SUBLANES, SUBLANES, D_MODEL), lambda i, z: (i, 0, 0))],
            out_specs=pl.BlockSpec(memory_space=pl.ANY),
            scratch_shapes=[pltpu.VMEM((ZERO_ROWS, D_MODEL), F32),
                            pltpu.SemaphoreType.DMA(()), pltpu.SemaphoreType.DMA(())],
        ),
        compiler_params=_cparams(1),
        name="dispatch_first",
    )(zfrom, dest.reshape(nt, 1, tm * TOP_K), h.reshape(-1, SUBLANES, D_MODEL))


def _dispatch_more(dest, h, xs, *, tm):
    nt = h.shape[0] // tm
    return pl.pallas_call(
        functools.partial(_dispatch_more_kernel, tm=tm),
        out_shape=jax.ShapeDtypeStruct(xs.shape, xs.dtype),
        grid=(nt,),
        in_specs=[pl.BlockSpec((1, 1, tm * TOP_K), lambda i: (i, 0, 0), memory_space=pltpu.SMEM),
                  pl.BlockSpec((tm // SUBLANES, SUBLANES, D_MODEL), lambda i: (i, 0, 0)),
                  pl.BlockSpec(memory_space=pl.ANY)],
        out_specs=pl.BlockSpec(memory_space=pl.ANY),
        scratch_shapes=[pltpu.SemaphoreType.DMA(())],
        input_output_aliases={2: 0},
        compiler_params=_cparams(1),
        name="dispatch_more",
    )(dest.reshape(nt, 1, tm * TOP_K), h.reshape(-1, SUBLANES, D_MODEL), xs)


def _expert_kernel(be_ref, nu_ref, xs_ref, wgu_ref, bgu_ref, wd_ref, bd_ref, ys_ref, wgu_b, wd_b):
    i = pl.program_id(0)

    @pl.when(i < nu_ref[0])
    def _():
        @pl.when((i == 0) | (be_ref[i] != be_ref[jnp.maximum(i - 1, 0)]))
        def _():
            wgu_b[...] = wgu_ref[...].astype(BF16)
            wd_b[...] = wd_ref[...].astype(BF16)

        x = xs_ref[...].astype(BF16)
        gu = jnp.dot(x, wgu_b[...], preferred_element_type=F32) + bgu_ref[...]
        g = jnp.minimum(gu[:, :D_FF], SWIGLU_LIMIT)
        u = jnp.clip(gu[:, D_FF:], -SWIGLU_LIMIT, SWIGLU_LIMIT)
        a = (u + 1.0) * (g * jax.nn.sigmoid(SWIGLU_ALPHA * g))
        ys_ref[...] = jnp.dot(a.astype(BF16), wd_b[...], preferred_element_type=F32) + bd_ref[...]


def _experts(block_e, n_used, xs, w):
    nblk = xs.shape[0] // MOE_BLOCK
    row = lambda i, be, nu: (jnp.minimum(i, nu[0] - 1), 0)
    per_e = lambda i, be, nu: (be[i], 0, 0)
    return pl.pallas_call(
        _expert_kernel,
        out_shape=jax.ShapeDtypeStruct(xs.shape, F32),
        grid_spec=pltpu.PrefetchScalarGridSpec(
            num_scalar_prefetch=2,
            grid=(nblk,),
            in_specs=[pl.BlockSpec((MOE_BLOCK, D_MODEL), row),
                      pl.BlockSpec((None, D_MODEL, 2 * D_FF), per_e),
                      pl.BlockSpec((None, 1, 2 * D_FF), per_e),
                      pl.BlockSpec((None, D_FF, D_MODEL), per_e),
                      pl.BlockSpec((None, 1, D_MODEL), per_e)],
            out_specs=pl.BlockSpec((MOE_BLOCK, D_MODEL), row),
            scratch_shapes=[pltpu.VMEM((D_MODEL, 2 * D_FF), BF16), pltpu.VMEM((D_FF, D_MODEL), BF16)],
        ),
        compiler_params=_cparams(1),
        name="experts",
    )(block_e, n_used, xs, w["w_gu"], w["b_gu"], w["w_down"], w["b_down"])


def _combine_kernel(dcur_ref, dnext_ref, x1_ref, rg_ref, mod_ref, gpost_ref, ys_ref, o_ref, gbuf, sem, *, tm, n):
    i = pl.program_id(0)
    slot = i % 2

    def gather(dest_ref, sl):
        def issue(j, carry):
            for s in range(SUBLANES):
                for kk in range(TOP_K):
                    d = dest_ref[0, 0, j * (SUBLANES * TOP_K) + s * TOP_K + kk]
                    pltpu.make_async_copy(ys_ref.at[pl.ds(d, 1), :],
                                          gbuf.at[sl, kk, j, pl.ds(s, 1), :], sem.at[sl]).start()
            return carry

        lax.fori_loop(0, tm // SUBLANES, issue, 0)

    @pl.when(i == 0)
    def _():
        gather(dcur_ref, 0)

    @pl.when(i + 1 < n)
    def _():
        gather(dnext_ref, 1 - slot)

    for kk in range(TOP_K):
        pltpu.make_async_copy(gbuf.at[1 - slot, kk], gbuf.at[slot, kk], sem.at[slot]).wait()

    rg = rg_ref[...]
    y = rg[:, 0:1] * gbuf[slot, 0].reshape(tm, D_MODEL)
    for kk in range(1, TOP_K):
        y = y + rg[:, kk:kk + 1] * gbuf[slot, kk].reshape(tm, D_MODEL)
    o_ref[...] = x1_ref[...] + mod_ref[5:6, :] * _rms(y, gpost_ref[...])


def _combine(dest, x1, rg, mod, g_post, ys, *, tm):
    nb, nt, _ = x1.shape
    ntile = nt // tm
    n = nb * ntile
    tok = lambda width: pl.BlockSpec((tm, width), lambda i: (i, 0))
    dest = dest.reshape(n, 1, tm * TOP_K)
    dspec = lambda f: pl.BlockSpec((1, 1, tm * TOP_K), lambda i: (f(i), 0, 0), memory_space=pltpu.SMEM)
    out = pl.pallas_call(
        functools.partial(_combine_kernel, tm=tm, n=n),
        out_shape=jax.ShapeDtypeStruct((nb * nt, D_MODEL), F32),
        grid=(n,),
        in_specs=[dspec(lambda i: i), dspec(lambda i: jnp.minimum(i + 1, n - 1)),
                  tok(D_MODEL), tok(LANES),
                  pl.BlockSpec((None, N_MOD, D_MODEL), lambda i: (i // ntile, 0, 0)),
                  pl.BlockSpec(g_post.shape, lambda i: (0, 0)),
                  pl.BlockSpec(memory_space=pl.ANY)],
        out_specs=tok(D_MODEL),
        scratch_shapes=[pltpu.VMEM((2, TOP_K, tm // SUBLANES, SUBLANES, D_MODEL), F32),
                        pltpu.SemaphoreType.DMA((2,))],
        compiler_params=_cparams(1),
        name="combine",
    )(dest, dest, x1.reshape(nb * nt, D_MODEL), rg.reshape(nb * nt, LANES), mod, g_post, ys)
    return out.reshape(x1.shape)


def _rot_swap(w):
    half = QK_ROPE // 2
    return jnp.concatenate([-w[..., half:], w[..., :half]], axis=-1)


def _prep_weights(w_in, g_q_a, w_q_b, g_kv_a, w_uk, w_uv, w_pool, pool_scale, w_o, g_pre_mix, g_post_mix,
                  g_pre_ffn, w_router, b_router, w_gu, b_gu, w_down, b_down, g_post_ffn):
    row = lambda a: a.reshape(1, -1).astype(F32)
    w_kpe = w_in[:, U_OFF:U_OFF + QK_ROPE]
    zeros = lambda *s: jnp.zeros(s, F32)
    d = D_MODEL
    slab = lambda a: jnp.concatenate([zeros(d, QK_NOPE), a, zeros(d, LANES - QK_HEAD)], axis=1)
    w_in_ext = jnp.concatenate([w_in[:, :U_OFF], w_in[:, U_OFF + QK_ROPE:], slab(w_kpe), slab(_rot_swap(w_kpe))],
                               axis=1)
    pad_q = zeros(Q_LORA, N_HEADS, LANES - QK_HEAD)
    wq_plain = jnp.concatenate([w_q_b, pad_q], axis=2).reshape(Q_LORA, HEAD_SLABS)
    wq_swap = jnp.concatenate([zeros(Q_LORA, N_HEADS, QK_NOPE), _rot_swap(w_q_b[..., QK_NOPE:]), pad_q],
                              axis=2).reshape(Q_LORA, HEAD_SLABS)
    pad_kv = zeros(KV_LORA, N_HEADS, LANES - QK_NOPE)
    wk = jnp.concatenate([w_uk, pad_kv], axis=2).reshape(KV_LORA, HEAD_SLABS)
    wv = jnp.concatenate([w_uv, pad_kv], axis=2).reshape(KV_LORA, HEAD_SLABS)
    mla_w = N_HEADS * V_HEAD
    woa = jnp.concatenate([w_o[:mla_w].reshape(N_HEADS, V_HEAD, d), zeros(N_HEADS, LANES - V_HEAD, d)],
                          axis=1).reshape(HEAD_SLABS, d)
    return {
        "g_pre_mix": row(g_pre_mix), "g_q_a": row(g_q_a), "g_kv_a": row(g_kv_a), "pool_scale": row(pool_scale),
        "w_in": w_in_ext.astype(BF16),
        "w_q": jnp.concatenate([wq_plain, wq_swap], axis=1).astype(BF16),
        "w_k": wk.astype(BF16), "w_vt": wv.T.astype(BF16),
        "w_pool": w_pool.astype(BF16),
        "g_post_mix": row(g_post_mix), "g_pre_ffn": row(g_pre_ffn), "g_post_ffn": row(g_post_ffn),
        "w_o_attn": woa.astype(BF16), "w_o_pool": w_o[mla_w:].astype(BF16),
        "w_router": jnp.pad(w_router, ((0, 0), (0, LANES - N_EXPERTS))).astype(BF16),
        "b_router": jnp.pad(b_router, (0, LANES - N_EXPERTS)).reshape(1, LANES).astype(F32),
        "w_gu": w_gu, "b_gu": b_gu.reshape(N_EXPERTS, 1, 2 * D_FF).astype(F32),
        "w_down": w_down, "b_down": b_down.reshape(N_EXPERTS, 1, D_MODEL).astype(F32),
    }


def _rope_tables(pos):
    half = QK_ROPE // 2
    inv = ROPE_THETA ** (-jnp.arange(half, dtype=F32) / half)
    ang = pos.astype(F32)[:, None] * inv[None, :]
    cos, sin = jnp.cos(ang), jnp.sin(ang)
    n = pos.shape[0]
    cos_t = jnp.concatenate([jnp.ones((n, QK_NOPE), F32), cos, cos, jnp.zeros((n, LANES - QK_HEAD), F32)], axis=1)
    sin_t = jnp.concatenate([jnp.zeros((n, QK_NOPE), F32), sin, sin, jnp.zeros((n, LANES - QK_HEAD), F32)], axis=1)
    return cos_t, sin_t


def _tile(n, pref):
    return pref if n % pref == 0 else n


def _mixer_path(x, mod, pos0, hist, cache, w, cnt0):
    nb, nt, _ = x.shape
    tm = _tile(nt, 512)
    cos_t, sin_t = _rope_tables(pos0 + jnp.arange(nt, dtype=jnp.int32))
    q, k, vt, kv_new, kslab, pooled, pool_tail = _premix(x, mod, cos_t, sin_t, hist, w, tm=tm, pos0=pos0)
    if cache is None:
        attn = _attention(q, k, vt, tq=tm, tk=_tile(nt, 2 * tm), causal=True, kv_len=nt)
    else:
        ckv, ckpe = cache
        past = ckv.shape[1]
        ckpe_slab = jnp.pad(ckpe, ((0, 0), (0, 0), (QK_NOPE, LANES - QK_HEAD)))
        kc, vtc = _kvproj(ckv, ckpe_slab, w["w_k"], w["w_vt"], tm=_tile(past, 512))
        kv_len = past + nt
        tk = -(-kv_len // 256) * 256
        k_all = jnp.concatenate([kc, k, jnp.zeros((nb, tk - kv_len, HEAD_SLABS), BF16)], axis=1)
        vt_all = jnp.concatenate([vtc, vt, jnp.zeros((nb, HEAD_SLABS, tk - kv_len), BF16)], axis=2)
        tq = -(-nt // LANES) * LANES
        q_pad = jnp.pad(q, ((0, 0), (0, tq - nt), (0, 0)))
        attn = _attention(q_pad, k_all, vt_all, tq=tq, tk=tk, causal=False, kv_len=kv_len)[:, :nt]
    x1, h2, ri, rg, cnt = _postmix(attn, pooled, x, mod, cnt0, w, tm=tm)
    return x1, h2, ri, rg, cnt, kv_new, kslab[..., QK_NOPE:QK_HEAD], pool_tail[:, 1:]


def kernel(x_prompt, x_sample, c_prompt, c_sample, cache_kv_latent, cache_k_rope, state_pool, w_ada, b_ada,
           g_pre_mix, w_in, g_q_a, w_q_b, g_kv_a, w_uk, w_uv, w_pool, pool_scale, w_o, g_post_mix, g_pre_ffn,
           w_router, b_router, w_gu, b_gu, w_down, b_down, g_post_ffn):
    assert w_ada.shape[0] == 1, "single-layer step"
    bp, sp, _ = x_prompt.shape
    bs, ss, _ = x_sample.shape
    past = cache_kv_latent.shape[2]
    w = _prep_weights(w_in[0], g_q_a[0], w_q_b[0], g_kv_a[0], w_uk[0], w_uv[0], w_pool[0], pool_scale[0], w_o[0],
                      g_pre_mix[0], g_post_mix[0], g_pre_ffn[0], w_router[0], b_router[0], w_gu[0], b_gu[0],
                      w_down[0], b_down[0], g_post_ffn[0])

    mod = _ada(jnp.concatenate([c_prompt, c_sample], axis=0), w_ada[0], b_ada[0])
    mod = mod.reshape(bp + bs, N_MOD, D_MODEL)
    mod_p, mod_s = mod[:bp], mod[bp:]

    hist_p = jnp.zeros((bp, HIST_ROWS, POOL_WIDTH), F32)
    hist_s = jnp.pad(state_pool[0], ((0, 0), (1, 0), (0, 0)))
    cnt0 = jnp.zeros((8, LANES), F32)
    x1p, h2p, rip, rgp, cntp, kv_p, kpe_p, pool_p = _mixer_path(x_prompt, mod_p, 0, hist_p, None, w, cnt0)
    x1s, h2s, ris, rgs, cnts, kv_s, kpe_s, pool_s = _mixer_path(
        x_sample, mod_s, past, hist_s, (cache_kv_latent[0], cache_k_rope[0]), w, cntp)

    counts = cnts[0, :N_EXPERTS].astype(jnp.int32)
    padded = (counts + MOE_BLOCK - 1) // MOE_BLOCK * MOE_BLOCK
    pad_end = jnp.cumsum(padded)
    pad_start = pad_end - padded
    n_tok = bp * sp + bs * ss
    n_blocks = -(-(n_tok * TOP_K) // MOE_BLOCK) + N_EXPERTS
    block_row = jnp.arange(n_blocks, dtype=jnp.int32) * MOE_BLOCK
    block_e = jnp.minimum(jnp.sum((pad_end[None, :] <= block_row[:, None]).astype(jnp.int32), axis=1),
                          N_EXPERTS - 1)
    n_used = (pad_end[-1:] // MOE_BLOCK).astype(jnp.int32)

    def dest_of(ri):
        ids = ri[..., :TOP_K]
        hot = ids[..., None] == jnp.arange(N_EXPERTS, dtype=jnp.int32)
        return ri[..., TOP_K:2 * TOP_K] + jnp.sum(jnp.where(hot, pad_start, 0), axis=-1)

    dest_p = dest_of(rip)
    dest_s = dest_of(ris)
    n_rows = n_blocks * MOE_BLOCK
    zfrom = jnp.minimum((pad_start + counts) // SUBLANES * SUBLANES, n_rows - ZERO_ROWS).astype(jnp.int32)
    xs = _dispatch_first(zfrom, dest_p, h2p.reshape(bp * sp, D_MODEL), n_rows, tm=512)
    xs = _dispatch_more(dest_s, h2s.reshape(bs * ss, D_MODEL), xs, tm=_tile(bs * ss, 512))
    ys = _experts(block_e, n_used, xs, w)
    y_p = _combine(dest_p, x1p, rgp, mod_p, w["g_post_ffn"], ys, tm=_tile(sp, 256))
    y_s = _combine(dest_s, x1s, rgs, mod_s, w["g_post_ffn"], ys, tm=_tile(ss, 256))
    return (y_p, y_s, kv_p[None], kpe_p[None], pool_p[None], kv_s[None], kpe_s[None], pool_s[None])
```

```python
import functools

import jax
import jax.numpy as jnp
from jax import lax
from jax.experimental import pallas as pl
from jax.experimental.pallas import tpu as pltpu

F32 = jnp.float32
BF16 = jnp.bfloat16

D_MODEL = 1024
CHUNK = 64
N_HEADS = 8
QK_NOPE = 64
QK_ROPE = 32
QK_HEAD = QK_NOPE + QK_ROPE
V_HEAD = 64
Q_LORA = 384
KV_LORA = 256
ROPE_THETA = 10000.0
POOL_WINDOWS = (2, 4, 8, 16)
POOL_GROUP_DIM = 128
POOL_WIDTH = POOL_GROUP_DIM * len(POOL_WINDOWS)
POOL_HIST = max(POOL_WINDOWS) - 1
HIST_ROWS = POOL_HIST + 1
N_EXPERTS = 32
TOP_K = 4
D_FF = 1024
SWIGLU_LIMIT = 7.0
SWIGLU_ALPHA = 1.702
N_MOD = 6
EPS = 1e-6

LANES = 128
HEAD_SLABS = N_HEADS * LANES
ONE_LANE = V_HEAD
IN_EXT = Q_LORA + KV_LORA + POOL_WIDTH + 2 * LANES
U_OFF = Q_LORA + KV_LORA
KPE_OFF = U_OFF + POOL_WIDTH
SM_SCALE = QK_HEAD ** -0.5
LOG2_E = 1.4426950408889634
Q_SCALE = SM_SCALE * LOG2_E
NEG_INF = float("-inf")

SUBLANES = 8
MOE_BLOCK = 512
ZERO_ROWS = MOE_BLOCK + SUBLANES
VMEM_LIMIT = 56 * 1024 * 1024


def _cparams(n_axes, vmem=VMEM_LIMIT):
    return pltpu.CompilerParams(dimension_semantics=("arbitrary",) * n_axes, vmem_limit_bytes=vmem)


def _rms(x, g):
    return x * lax.rsqrt(jnp.mean(x * x, axis=-1, keepdims=True) + EPS) * g


def _ada_kernel(c_ref, w_ref, b_ref, o_ref):
    c = c_ref[...]
    s = (c * jax.nn.sigmoid(c)).astype(BF16)
    o_ref[...] = jnp.dot(s, w_ref[...].astype(BF16), preferred_element_type=F32) + b_ref[...]


def _ada(c, w_ada, b_ada):
    nb = c.shape[0]
    return pl.pallas_call(
        _ada_kernel,
        out_shape=jax.ShapeDtypeStruct((nb, N_MOD * D_MODEL), F32),
        grid=(N_MOD,),
        in_specs=[pl.BlockSpec((nb, D_MODEL), lambda j: (0, 0)),
                  pl.BlockSpec((D_MODEL, D_MODEL), lambda j: (0, j)),
                  pl.BlockSpec((1, D_MODEL), lambda j: (0, j))],
        out_specs=pl.BlockSpec((nb, D_MODEL), lambda j: (0, j)),
        compiler_params=_cparams(1),
        name="ada",
    )(c, w_ada, b_ada.reshape(1, -1))


def _store_kv(kvb, kslab, wk_ref, wvt_ref, k_ref, vt_ref):
    kk = jnp.dot(kvb, wk_ref[...], preferred_element_type=F32)
    for hd in range(N_HEADS):
        sl = slice(hd * LANES, (hd + 1) * LANES)
        k_ref[:, sl] = (kk[:, sl] + kslab).astype(BF16)
    vt = lax.dot_general(wvt_ref[...], kvb, (((1,), (1,)), ((), ())), preferred_element_type=F32)
    row = lax.broadcasted_iota(jnp.int32, (HEAD_SLABS, 1), 0)
    vt_ref[...] = (vt + (row % LANES == ONE_LANE).astype(F32)).astype(BF16)


def _premix_kernel(x_ref, mod_ref, cos_ref, sin_ref, hist_ref, gpre_ref, gqa_ref, gkv_ref, pscale_ref,
                   win_ref, wq_ref, wk_ref, wvt_ref, wpool_ref,
                   q_ref, k_ref, vt_ref, kv_ref, kpe_ref, pooled_ref, poolnew_ref,
                   ubuf, *, tm, pos0):
    t = pl.program_id(1)
    x = x_ref[...]
    h = (_rms(x, gpre_ref[...]) * (1.0 + mod_ref[1:2, :]) + mod_ref[0:1, :]).astype(BF16)
    z = jnp.dot(h, win_ref[...], preferred_element_type=F32)
    cosv = cos_ref[...]
    sinv = sin_ref[...]

    qan = _rms(z[:, :Q_LORA], gqa_ref[...]).astype(BF16)
    qq = jnp.dot(qan, wq_ref[...], preferred_element_type=F32)
    for hd in range(N_HEADS):
        a = qq[:, hd * LANES:(hd + 1) * LANES]
        b = qq[:, HEAD_SLABS + hd * LANES:HEAD_SLABS + (hd + 1) * LANES]
        q_ref[:, hd * LANES:(hd + 1) * LANES] = ((a * cosv + b * sinv) * Q_SCALE).astype(BF16)

    kvn = _rms(z[:, Q_LORA:U_OFF], gkv_ref[...])
    kv_ref[...] = kvn
    kslab = z[:, KPE_OFF:KPE_OFF + LANES] * cosv + z[:, KPE_OFF + LANES:KPE_OFF + 2 * LANES] * sinv
    kpe_ref[...] = kslab
    _store_kv(kvn.astype(BF16), kslab, wk_ref, wvt_ref, k_ref, vt_ref)

    @pl.when(t == 0)
    def _():
        ubuf[0:HIST_ROWS, :] = hist_ref[...]

    @pl.when(t > 0)
    def _():
        ubuf[0:HIST_ROWS, :] = ubuf[tm:tm + HIST_ROWS, :]

    ubuf[HIST_ROWS:HIST_ROWS + tm, :] = z[:, U_OFF:KPE_OFF]
    pos = pos0 + t * tm + lax.broadcasted_iota(jnp.int32, (tm, 1), 0)
    for g, w in enumerate(POOL_WINDOWS):
        sl = slice(g * POOL_GROUP_DIM, (g + 1) * POOL_GROUP_DIM)
        u = ubuf[HIST_ROWS:HIST_ROWS + tm, sl]
        acc = u
        for j in range(1, w):
            acc = acc + ubuf[HIST_ROWS - j:HIST_ROWS - j + tm, sl]
        cnt = jnp.minimum(pos + 1, w).astype(F32)
        d = (acc / cnt - u).astype(BF16)
        y = jnp.dot(d, wpool_ref[g], preferred_element_type=F32) * pscale_ref[:, sl]
        pooled_ref[:, sl] = y.astype(BF16)
    poolnew_ref[...] = ubuf[tm:tm + HIST_ROWS, :]


def _premix(x, mod, cos_t, sin_t, hist, w, *, tm, pos0):
    nb, nt, _ = x.shape
    grid = (nb, nt // tm)
    tok = lambda width: pl.BlockSpec((None, tm, width), lambda b, t: (b, t, 0))
    full = lambda a: pl.BlockSpec(a.shape, lambda b, t: (0,) * a.ndim)
    tab = pl.BlockSpec((tm, LANES), lambda b, t: (t, 0))
    perb = lambda rows, width: pl.BlockSpec((None, rows, width), lambda b, t: (b, 0, 0))
    outs = [
        jax.ShapeDtypeStruct((nb, nt, HEAD_SLABS), BF16),
        jax.ShapeDtypeStruct((nb, nt, HEAD_SLABS), BF16),
        jax.ShapeDtypeStruct((nb, HEAD_SLABS, nt), BF16),
        jax.ShapeDtypeStruct((nb, nt, KV_LORA), F32),
        jax.ShapeDtypeStruct((nb, nt, LANES), F32),
        jax.ShapeDtypeStruct((nb, nt, POOL_WIDTH), BF16),
        jax.ShapeDtypeStruct((nb, HIST_ROWS, POOL_WIDTH), F32),
    ]
    return pl.pallas_call(
        functools.partial(_premix_kernel, tm=tm, pos0=pos0),
        out_shape=outs,
        grid=grid,
        in_specs=[tok(D_MODEL), perb(N_MOD, D_MODEL), tab, tab, perb(HIST_ROWS, POOL_WIDTH),
                  full(w["g_pre_mix"]), full(w["g_q_a"]), full(w["g_kv_a"]), full(w["pool_scale"]),
                  full(w["w_in"]), full(w["w_q"]), full(w["w_k"]), full(w["w_vt"]), full(w["w_pool"])],
        out_specs=[tok(HEAD_SLABS), tok(HEAD_SLABS),
                   pl.BlockSpec((None, HEAD_SLABS, tm), lambda b, t: (b, 0, t)),
                   tok(KV_LORA), tok(LANES), tok(POOL_WIDTH), perb(HIST_ROWS, POOL_WIDTH)],
        scratch_shapes=[pltpu.VMEM((HIST_ROWS + tm, POOL_WIDTH), F32)],
        compiler_params=_cparams(2),
        name="premix",
    )(x, mod, cos_t, sin_t, hist, w["g_pre_mix"], w["g_q_a"], w["g_kv_a"], w["pool_scale"],
      w["w_in"], w["w_q"], w["w_k"], w["w_vt"], w["w_pool"])


def _kvproj_kernel(kv_ref, kpe_ref, wk_ref, wvt_ref, k_ref, vt_ref):
    _store_kv(kv_ref[...].astype(BF16), kpe_ref[...], wk_ref, wvt_ref, k_ref, vt_ref)


def _kvproj(kv, kpe_slab, w_k, w_vt, *, tm):
    nb, nt, _ = kv.shape
    tok = lambda width: pl.BlockSpec((None, tm, width), lambda b, t: (b, t, 0))
    full = lambda a: pl.BlockSpec(a.shape, lambda b, t: (0,) * a.ndim)
    return pl.pallas_call(
        _kvproj_kernel,
        out_shape=[jax.ShapeDtypeStruct((nb, nt, HEAD_SLABS), BF16),
                   jax.ShapeDtypeStruct((nb, HEAD_SLABS, nt), BF16)],
        grid=(nb, nt // tm),
        in_specs=[tok(KV_LORA), tok(LANES), full(w_k), full(w_vt)],
        out_specs=[tok(HEAD_SLABS), pl.BlockSpec((None, HEAD_SLABS, tm), lambda b, t: (b, 0, t))],
        compiler_params=_cparams(2),
        name="kvproj",
    )(kv, kpe_slab, w_k, w_vt)


def _attn_kernel(q_ref, k_ref, vt_ref, o_ref, m_ref, acc_ref, *, tq, tk, nk, causal, kv_len):
    qi = pl.program_id(1)
    ki = pl.program_id(2)
    last = ((qi + 1) * tq - 1) // tk if causal else nk - 1
    need_len_mask = kv_len < nk * tk

    @pl.when(ki == 0)
    def _():
        m_ref[...] = jnp.full(m_ref.shape, NEG_INF, F32)
        acc_ref[...] = jnp.zeros(acc_ref.shape, F32)

    def step(masked, nkeys=tk):
        if masked:
            kpos = ki * tk + lax.broadcasted_iota(jnp.int32, (nkeys, tq), 0)
            vis = None
            if causal:
                qpos = qi * tq + lax.broadcasted_iota(jnp.int32, (nkeys, tq), 1)
                vis = (kpos // CHUNK) <= (qpos // CHUNK)
            if need_len_mask:
                lm = kpos < kv_len
                vis = lm if vis is None else (vis & lm)

        def scores(hd):
            sl = slice(hd * LANES, (hd + 1) * LANES)
            return lax.dot_general(k_ref[:nkeys, sl], q_ref[:, sl], (((1,), (1,)), ((), ())),
                                   preferred_element_type=F32)

        def accumulate(hd, alpha, p):
            sl = slice(hd * LANES, (hd + 1) * LANES)
            acc_ref[hd] = alpha * acc_ref[hd] + jnp.dot(vt_ref[sl, :nkeys], p, preferred_element_type=F32)

        s_next = scores(0)
        pending = None
        for hd in range(N_HEADS):
            s = s_next
            if hd + 1 < N_HEADS:
                s_next = scores(hd + 1)
            if pending is not None:
                accumulate(*pending)
            if masked:
                s = jnp.where(vis, s, NEG_INF)
            m_prev = m_ref[hd:hd + 1, :]
            m_new = jnp.maximum(m_prev, jnp.max(s, axis=0, keepdims=True))
            m_ref[hd:hd + 1, :] = m_new
            pending = (hd, jnp.exp2(m_prev - m_new), jnp.exp2(s - m_new).astype(BF16))
        accumulate(*pending)

    if causal and tk == 2 * tq and not need_len_mask:
        @pl.when(ki < last)
        def _():
            step(False)

        @pl.when((ki == last) & (qi % 2 == 0))
        def _():
            step(True, tq)

        @pl.when((ki == last) & (qi % 2 == 1))
        def _():
            step(True)
    elif causal or need_len_mask:
        @pl.when(ki < last)
        def _():
            step(False)

        @pl.when(ki == last)
        def _():
            step(True)
    else:
        step(False)

    @pl.when(ki == nk - 1)
    def _():
        for hd in range(N_HEADS):
            acc = acc_ref[hd]
            out_t = acc / acc[ONE_LANE:ONE_LANE + 1, :]
            o_ref[:, hd * LANES:(hd + 1) * LANES] = out_t.T.astype(BF16)


def _attention(q, k, vt, *, tq, tk, causal, kv_len):
    nb, nq_tot, _ = q.shape
    nk = k.shape[1] // tk
    nq = nq_tot // tq
    if causal:
        last = lambda i: ((i + 1) * tq - 1) // tk
        kmap = lambda b, i, j: (b, jnp.minimum(j, last(i)), 0)
        vmap = lambda b, i, j: (b, 0, jnp.minimum(j, last(i)))
    else:
        kmap = lambda b, i, j: (b, j, 0)
        vmap = lambda b, i, j: (b, 0, j)
    return pl.pallas_call(
        functools.partial(_attn_kernel, tq=tq, tk=tk, nk=nk, causal=causal, kv_len=kv_len),
        out_shape=jax.ShapeDtypeStruct((nb, nq_tot, HEAD_SLABS), BF16),
        grid=(nb, nq, nk),
        in_specs=[pl.BlockSpec((None, tq, HEAD_SLABS), lambda b, i, j: (b, i, 0)),
                  pl.BlockSpec((None, tk, HEAD_SLABS), kmap),
                  pl.BlockSpec((None, HEAD_SLABS, tk), vmap)],
        out_specs=pl.BlockSpec((None, tq, HEAD_SLABS), lambda b, i, j: (b, i, 0)),
        scratch_shapes=[pltpu.VMEM((N_HEADS, tq), F32), pltpu.VMEM((N_HEADS, LANES, tq), F32)],
        compiler_params=_cparams(3),
        name="attention",
    )(q, k, vt)


def _postmix_kernel(attn_ref, pooled_ref, x_ref, mod_ref, cnt0_ref, gpost_ref, gffn_ref,
                    woa_ref, wop_ref, wr_ref, br_ref,
                    x1_ref, h2_ref, ri_ref, rg_ref, cnt_ref,
                    carry, ltri, *, tm):
    first = (pl.program_id(0) == 0) & (pl.program_id(1) == 0)

    @pl.when(first)
    def _():
        carry[...] = cnt0_ref[...]
        r = lax.broadcasted_iota(jnp.int32, (tm, tm), 0)
        c = lax.broadcasted_iota(jnp.int32, (tm, tm), 1)
        ltri[...] = (r > c).astype(BF16)

    mix = (jnp.dot(attn_ref[...], woa_ref[...], preferred_element_type=F32)
           + jnp.dot(pooled_ref[...], wop_ref[...], preferred_element_type=F32))
    x1 = x_ref[...] + mod_ref[2:3, :] * _rms(mix, gpost_ref[...])
    x1_ref[...] = x1
    h2 = _rms(x1, gffn_ref[...]) * (1.0 + mod_ref[4:5, :]) + mod_ref[3:4, :]
    h2_ref[...] = h2

    lane = lax.broadcasted_iota(jnp.int32, (tm, LANES), 1).astype(F32)
    logits = jnp.dot(h2.astype(BF16), wr_ref[...], preferred_element_type=F32) + br_ref[...]
    logits = jnp.where(lane < N_EXPERTS, logits, NEG_INF)
    sel = jnp.zeros((tm, LANES), F32)
    ids, vals, hots = [], [], []
    for _ in range(TOP_K):
        mk = jnp.max(logits, axis=1, keepdims=True)
        idx = jnp.min(jnp.where(logits == mk, lane, float(LANES)), axis=1, keepdims=True)
        hot = lane == idx
        logits = jnp.where(hot, NEG_INF, logits)
        sel = sel + hot.astype(F32)
        ids.append(idx)
        vals.append(mk)
        hots.append(hot)
    ex = [jnp.exp(vk - vals[0]) for vk in vals]
    denom = ex[0] + ex[1] + ex[2] + ex[3]

    before = jnp.dot(ltri[...], sel.astype(BF16), preferred_element_type=F32) + carry[0:1, :]
    ri = jnp.zeros((tm, LANES), F32)
    rg = jnp.zeros((tm, LANES), F32)
    for kk in range(TOP_K):
        rank = jnp.sum(jnp.where(hots[kk], before, 0.0), axis=1, keepdims=True)
        ri = jnp.where(lane == kk, ids[kk], ri)
        ri = jnp.where(lane == TOP_K + kk, rank, ri)
        rg = jnp.where(lane == kk, ex[kk] / denom, rg)
    ri_ref[...] = ri.astype(jnp.int32)
    rg_ref[...] = rg
    carry[0:1, :] = carry[0:1, :] + jnp.sum(sel, axis=0, keepdims=True)
    cnt_ref[...] = carry[...]


def _postmix(attn, pooled, x, mod, cnt0, w, *, tm):
    nb, nt, _ = x.shape
    tok = lambda width: pl.BlockSpec((None, tm, width), lambda b, t: (b, t, 0))
    full = lambda a: pl.BlockSpec(a.shape, lambda b, t: (0,) * a.ndim)
    outs = [
        jax.ShapeDtypeStruct((nb, nt, D_MODEL), F32),
        jax.ShapeDtypeStruct((nb, nt, D_MODEL), F32),
        jax.ShapeDtypeStruct((nb, nt, LANES), jnp.int32),
        jax.ShapeDtypeStruct((nb, nt, LANES), F32),
        jax.ShapeDtypeStruct((8, LANES), F32),
    ]
    return pl.pallas_call(
        functools.partial(_postmix_kernel, tm=tm),
        out_shape=outs,
        grid=(nb, nt // tm),
        in_specs=[tok(HEAD_SLABS), tok(POOL_WIDTH), tok(D_MODEL),
                  pl.BlockSpec((None, N_MOD, D_MODEL), lambda b, t: (b, 0, 0)),
                  full(cnt0), full(w["g_post_mix"]), full(w["g_pre_ffn"]),
                  full(w["w_o_attn"]), full(w["w_o_pool"]), full(w["w_router"]), full(w["b_router"])],
        out_specs=[tok(D_MODEL), tok(D_MODEL), tok(LANES), tok(LANES),
                   pl.BlockSpec((8, LANES), lambda b, t: (0, 0))],
        scratch_shapes=[pltpu.VMEM((8, LANES), F32), pltpu.VMEM((tm, tm), BF16)],
        compiler_params=_cparams(2),
        name="postmix",
    )(attn, pooled, x, mod, cnt0, w["g_post_mix"], w["g_pre_ffn"],
      w["w_o_attn"], w["w_o_pool"], w["w_router"], w["b_router"])


def _dispatch_rows(dest_ref, h_ref, xs_ref, sem, tm):
    def issue(j, carry):
        for s in range(SUBLANES):
            for kk in range(TOP_K):
                d = dest_ref[0, 0, j * (SUBLANES * TOP_K) + s * TOP_K + kk]
                pltpu.make_async_copy(h_ref.at[j, pl.ds(s, 1), :], xs_ref.at[pl.ds(d, 1), :], sem).start()
        return carry

    lax.fori_loop(0, tm // SUBLANES, issue, 0)
    for _ in range(TOP_K):
        pltpu.make_async_copy(xs_ref.at[pl.ds(0, tm), :], xs_ref.at[pl.ds(0, tm), :], sem).wait()


def _dispatch_first_kernel(zfrom_ref, dest_ref, h_ref, xs_ref, zbuf, sem, zsem, *, tm):
    @pl.when(pl.program_id(0) == 0)
    def _():
        zbuf[...] = jnp.zeros(zbuf.shape, F32)
        for e in range(N_EXPERTS):
            start = pl.multiple_of(zfrom_ref[e], SUBLANES)
            cp = pltpu.make_async_copy(zbuf, xs_ref.at[pl.ds(start, ZERO_ROWS), :], zsem)
            cp.start()
            cp.wait()

    _dispatch_rows(dest_ref, h_ref, xs_ref, sem, tm)


def _dispatch_more_kernel(dest_ref, h_ref, xs_in_ref, xs_ref, sem, *, tm):
    del xs_in_ref
    _dispatch_rows(dest_ref, h_ref, xs_ref, sem, tm)


def _dispatch_first(zfrom, dest, h, n_rows, *, tm):
    nt = h.shape[0] // tm
    return pl.pallas_call(
        functools.partial(_dispatch_first_kernel, tm=tm),
        out_shape=jax.ShapeDtypeStruct((n_rows, D_MODEL), F32),
        grid_spec=pltpu.PrefetchScalarGridSpec(
            num_scalar_prefetch=1,
            grid=(nt,),
            in_specs=[pl.BlockSpec((1, 1, tm * TOP_K), lambda i, z: (i, 0, 0), memory_space=pltpu.SMEM),
                      pl.BlockSpec((tm // SUBLANES, SUBLANES, D_MODEL), lambda i, z: (i, 0, 0))],
            out_specs=pl.BlockSpec(memory_space=pl.ANY),
            scratch_shapes=[pltpu.VMEM((ZERO_ROWS, D_MODEL), F32),
                            pltpu.SemaphoreType.DMA(()), pltpu.SemaphoreType.DMA(())],
        ),
        compiler_params=_cparams(1),
        name="dispatch_first",
    )(zfrom, dest.reshape(nt, 1, tm * TOP_K), h.reshape(-1, SUBLANES, D_MODEL))


def _dispatch_more(dest, h, xs, *, tm):
    nt = h.shape[0] // tm
    return pl.pallas_call(
        functools.partial(_dispatch_more_kernel, tm=tm),
        out_shape=jax.ShapeDtypeStruct(xs.shape, xs.dtype),
        grid=(nt,),
        in_specs=[pl.BlockSpec((1, 1, tm * TOP_K), lambda i: (i, 0, 0), memory_space=pltpu.SMEM),
                  pl.BlockSpec((tm // SUBLANES, SUBLANES, D_MODEL), lambda i: (i, 0, 0)),
                  pl.BlockSpec(memory_space=pl.ANY)],
        out_specs=pl.BlockSpec(memory_space=pl.ANY),
        scratch_shapes=[pltpu.SemaphoreType.DMA(())],
        input_output_aliases={2: 0},
        compiler_params=_cparams(1),
        name="dispatch_more",
    )(dest.reshape(nt, 1, tm * TOP_K), h.reshape(-1, SUBLANES, D_MODEL), xs)


def _expert_kernel(be_ref, nu_ref, xs_ref, wgu_ref, bgu_ref, wd_ref, bd_ref, ys_ref, wgu_b, wd_b):
    i = pl.program_id(0)

    @pl.when(i < nu_ref[0])
    def _():
        @pl.when((i == 0) | (be_ref[i] != be_ref[jnp.maximum(i - 1, 0)]))
        def _():
            wgu_b[...] = wgu_ref[...].astype(BF16)
            wd_b[...] = wd_ref[...].astype(BF16)

        x = xs_ref[...].astype(BF16)
        gu = jnp.dot(x, wgu_b[...], preferred_element_type=F32) + bgu_ref[...]
        g = jnp.minimum(gu[:, :D_FF], SWIGLU_LIMIT)
        u = jnp.clip(gu[:, D_FF:], -SWIGLU_LIMIT, SWIGLU_LIMIT)
        a = (u + 1.0) * (g * jax.nn.sigmoid(SWIGLU_ALPHA * g))
        ys_ref[...] = jnp.dot(a.astype(BF16), wd_b[...], preferred_element_type=F32) + bd_ref[...]


def _experts(block_e, n_used, xs, w):
    nblk = xs.shape[0] // MOE_BLOCK
    row = lambda i, be, nu: (jnp.minimum(i, nu[0] - 1), 0)
    per_e = lambda i, be, nu: (be[i], 0, 0)
    return pl.pallas_call(
        _expert_kernel,
        out_shape=jax.ShapeDtypeStruct(xs.shape, F32),
        grid_spec=pltpu.PrefetchScalarGridSpec(
            num_scalar_prefetch=2,
            grid=(nblk,),
            in_specs=[pl.BlockSpec((MOE_BLOCK, D_MODEL), row),
                      pl.BlockSpec((None, D_MODEL, 2 * D_FF), per_e),
                      pl.BlockSpec((None, 1, 2 * D_FF), per_e),
                      pl.BlockSpec((None, D_FF, D_MODEL), per_e),
                      pl.BlockSpec((None, 1, D_MODEL), per_e)],
            out_specs=pl.BlockSpec((MOE_BLOCK, D_MODEL), row),
            scratch_shapes=[pltpu.VMEM((D_MODEL, 2 * D_FF), BF16), pltpu.VMEM((D_FF, D_MODEL), BF16)],
        ),
        compiler_params=_cparams(1),
        name="experts",
    )(block_e, n_used, xs, w["w_gu"], w["b_gu"], w["w_down"], w["b_down"])


def _combine_kernel(dcur_ref, dnext_ref, x1_ref, rg_ref, mod_ref, gpost_ref, ys_ref, o_ref, gbuf, sem, *, tm, n):
    i = pl.program_id(0)
    slot = i % 2

    def gather(dest_ref, sl):
        def issue(j, carry):
            for s in range(SUBLANES):
                for kk in range(TOP_K):
                    d = dest_ref[0, 0, j * (SUBLANES * TOP_K) + s * TOP_K + kk]
                    pltpu.make_async_copy(ys_ref.at[pl.ds(d, 1), :],
                                          gbuf.at[sl, kk, j, pl.ds(s, 1), :], sem.at[sl]).start()
            return carry

        lax.fori_loop(0, tm // SUBLANES, issue, 0)

    @pl.when(i == 0)
    def _():
        gather(dcur_ref, 0)

    @pl.when(i + 1 < n)
    def _():
        gather(dnext_ref, 1 - slot)

    for kk in range(TOP_K):
        pltpu.make_async_copy(gbuf.at[1 - slot, kk], gbuf.at[slot, kk], sem.at[slot]).wait()

    rg = rg_ref[...]
    y = rg[:, 0:1] * gbuf[slot, 0].reshape(tm, D_MODEL)
    for kk in range(1, TOP_K):
        y = y + rg[:, kk:kk + 1] * gbuf[slot, kk].reshape(tm, D_MODEL)
    o_ref[...] = x1_ref[...] + mod_ref[5:6, :] * _rms(y, gpost_ref[...])


def _combine(dest, x1, rg, mod, g_post, ys, *, tm):
    nb, nt, _ = x1.shape
    ntile = nt // tm
    n = nb * ntile
    tok = lambda width: pl.BlockSpec((tm, width), lambda i: (i, 0))
    dest = dest.reshape(n, 1, tm * TOP_K)
    dspec = lambda f: pl.BlockSpec((1, 1, tm * TOP_K), lambda i: (f(i), 0, 0), memory_space=pltpu.SMEM)
    out = pl.pallas_call(
        functools.partial(_combine_kernel, tm=tm, n=n),
        out_shape=jax.ShapeDtypeStruct((nb * nt, D_MODEL), F32),
        grid=(n,),
        in_specs=[dspec(lambda i: i), dspec(lambda i: jnp.minimum(i + 1, n - 1)),
                  tok(D_MODEL), tok(LANES),
                  pl.BlockSpec((None, N_MOD, D_MODEL), lambda i: (i // ntile, 0, 0)),
                  pl.BlockSpec(g_post.shape, lambda i: (0, 0)),
                  pl.BlockSpec(memory_space=pl.ANY)],
        out_specs=tok(D_MODEL),
        scratch_shapes=[pltpu.VMEM((2, TOP_K, tm // SUBLANES, SUBLANES, D_MODEL), F32),
                        pltpu.SemaphoreType.DMA((2,))],
        compiler_params=_cparams(1),
        name="combine",
    )(dest, dest, x1.reshape(nb * nt, D_MODEL), rg.reshape(nb * nt, LANES), mod, g_post, ys)
    return out.reshape(x1.shape)


def _rot_swap(w):
    half = QK_ROPE // 2
    return jnp.concatenate([-w[..., half:], w[..., :half]], axis=-1)


def _prep_weights(w_in, g_q_a, w_q_b, g_kv_a, w_uk, w_uv, w_pool, pool_scale, w_o, g_pre_mix, g_post_mix,
                  g_pre_ffn, w_router, b_router, w_gu, b_gu, w_down, b_down, g_post_ffn):
    row = lambda a: a.reshape(1, -1).astype(F32)
    w_kpe = w_in[:, U_OFF:U_OFF + QK_ROPE]
    zeros = lambda *s: jnp.zeros(s, F32)
    d = D_MODEL
    slab = lambda a: jnp.concatenate([zeros(d, QK_NOPE), a, zeros(d, LANES - QK_HEAD)], axis=1)
    w_in_ext = jnp.concatenate([w_in[:, :U_OFF], w_in[:, U_OFF + QK_ROPE:], slab(w_kpe), slab(_rot_swap(w_kpe))],
                               axis=1)
    pad_q = zeros(Q_LORA, N_HEADS, LANES - QK_HEAD)
    wq_plain = jnp.concatenate([w_q_b, pad_q], axis=2).reshape(Q_LORA, HEAD_SLABS)
    wq_swap = jnp.concatenate([zeros(Q_LORA, N_HEADS, QK_NOPE), _rot_swap(w_q_b[..., QK_NOPE:]), pad_q],
                              axis=2).reshape(Q_LORA, HEAD_SLABS)
    pad_kv = zeros(KV_LORA, N_HEADS, LANES - QK_NOPE)
    wk = jnp.concatenate([w_uk, pad_kv], axis=2).reshape(KV_LORA, HEAD_SLABS)
    wv = jnp.concatenate([w_uv, pad_kv], axis=2).reshape(KV_LORA, HEAD_SLABS)
    mla_w = N_HEADS * V_HEAD
    woa = jnp.concatenate([w_o[:mla_w].reshape(N_HEADS, V_HEAD, d), zeros(N_HEADS, LANES - V_HEAD, d)],
                          axis=1).reshape(HEAD_SLABS, d)
    return {
        "g_pre_mix": row(g_pre_mix), "g_q_a": row(g_q_a), "g_kv_a": row(g_kv_a), "pool_scale": row(pool_scale),
        "w_in": w_in_ext.astype(BF16),
        "w_q": jnp.concatenate([wq_plain, wq_swap], axis=1).astype(BF16),
        "w_k": wk.astype(BF16), "w_vt": wv.T.astype(BF16),
        "w_pool": w_pool.astype(BF16),
        "g_post_mix": row(g_post_mix), "g_pre_ffn": row(g_pre_ffn), "g_post_ffn": row(g_post_ffn),
        "w_o_attn": woa.astype(BF16), "w_o_pool": w_o[mla_w:].astype(BF16),
        "w_router": jnp.pad(w_router, ((0, 0), (0, LANES - N_EXPERTS))).astype(BF16),
        "b_router": jnp.pad(b_router, (0, LANES - N_EXPERTS)).reshape(1, LANES).astype(F32),
        "w_gu": w_gu, "b_gu": b_gu.reshape(N_EXPERTS, 1, 2 * D_FF).astype(F32),
        "w_down": w_down, "b_down": b_down.reshape(N_EXPERTS, 1, D_MODEL).astype(F32),
    }


def _rope_tables(pos):
    half = QK_ROPE // 2
    inv = ROPE_THETA ** (-jnp.arange(half, dtype=F32) / half)
    ang = pos.astype(F32)[:, None] * inv[None, :]
    cos, sin = jnp.cos(ang), jnp.sin(ang)
    n = pos.shape[0]
    cos_t = jnp.concatenate([jnp.ones((n, QK_NOPE), F32), cos, cos, jnp.zeros((n, LANES - QK_HEAD), F32)], axis=1)
    sin_t = jnp.concatenate([jnp.zeros((n, QK_NOPE), F32), sin, sin, jnp.zeros((n, LANES - QK_HEAD), F32)], axis=1)
    return cos_t, sin_t


def _tile(n, pref):
    return pref if n % pref == 0 else n


def _mixer_path(x, mod, pos0, hist, cache, w, cnt0):
    nb, nt, _ = x.shape
    tm = _tile(nt, 512)
    cos_t, sin_t = _rope_tables(pos0 + jnp.arange(nt, dtype=jnp.int32))
    q, k, vt, kv_new, kslab, pooled, pool_tail = _premix(x, mod, cos_t, sin_t, hist, w, tm=tm, pos0=pos0)
    if cache is None:
        attn = _attention(q, k, vt, tq=tm, tk=_tile(nt, 2 * tm), causal=True, kv_len=nt)
    else:
        ckv, ckpe = cache
        past = ckv.shape[1]
        kv_len = past + nt
        tk = -(-kv_len // 256) * 256
        ckpe_slab = jnp.pad(ckpe, ((0, 0), (0, 0), (QK_NOPE, LANES - QK_HEAD)))
        lat_all = jnp.concatenate([ckv, kv_new, jnp.zeros((nb, tk - kv_len, KV_LORA), F32)], axis=1)
        kpe_all = jnp.concatenate([ckpe_slab, kslab, jnp.zeros((nb, tk - kv_len, LANES), F32)], axis=1)
        k_all, vt_all = _kvproj(lat_all, kpe_all, w["w_k"], w["w_vt"], tm=_tile(tk, 768))
        tq = -(-nt // LANES) * LANES
        q_pad = jnp.pad(q, ((0, 0), (0, tq - nt), (0, 0)))
        attn = _attention(q_pad, k_all, vt_all, tq=tq, tk=tk, causal=False, kv_len=kv_len)[:, :nt]
    x1, h2, ri, rg, cnt = _postmix(attn, pooled, x, mod, cnt0, w, tm=tm)
    return x1, h2, ri, rg, cnt, kv_new, kslab[..., QK_NOPE:QK_HEAD], pool_tail[:, 1:]


def kernel(x_prompt, x_sample, c_prompt, c_sample, cache_kv_latent, cache_k_rope, state_pool, w_ada, b_ada,
           g_pre_mix, w_in, g_q_a, w_q_b, g_kv_a, w_uk, w_uv, w_pool, pool_scale, w_o, g_post_mix, g_pre_ffn,
           w_router, b_router, w_gu, b_gu, w_down, b_down, g_post_ffn):
    assert w_ada.shape[0] == 1, "single-layer step"
    bp, sp, _ = x_prompt.shape
    bs, ss, _ = x_sample.shape
    past = cache_kv_latent.shape[2]
    w = _prep_weights(w_in[0], g_q_a[0], w_q_b[0], g_kv_a[0], w_uk[0], w_uv[0], w_pool[0], pool_scale[0], w_o[0],
                      g_pre_mix[0], g_post_mix[0], g_pre_ffn[0], w_router[0], b_router[0], w_gu[0], b_gu[0],
                      w_down[0], b_down[0], g_post_ffn[0])

    mod = _ada(jnp.concatenate([c_prompt, c_sample], axis=0), w_ada[0], b_ada[0])
    mod = mod.reshape(bp + bs, N_MOD, D_MODEL)
    mod_p, mod_s = mod[:bp], mod[bp:]

    hist_p = jnp.zeros((bp, HIST_ROWS, POOL_WIDTH), F32)
    hist_s = jnp.pad(state_pool[0], ((0, 0), (1, 0), (0, 0)))
    cnt0 = jnp.zeros((8, LANES), F32)
    x1p, h2p, rip, rgp, cntp, kv_p, kpe_p, pool_p = _mixer_path(x_prompt, mod_p, 0, hist_p, None, w, cnt0)
    x1s, h2s, ris, rgs, cnts, kv_s, kpe_s, pool_s = _mixer_path(
        x_sample, mod_s, past, hist_s, (cache_kv_latent[0], cache_k_rope[0]), w, cntp)

    counts = cnts[0, :N_EXPERTS].astype(jnp.int32)
    padded = (counts + MOE_BLOCK - 1) // MOE_BLOCK * MOE_BLOCK
    pad_end = jnp.cumsum(padded)
    pad_start = pad_end - padded
    n_tok = bp * sp + bs * ss
    n_blocks = -(-(n_tok * TOP_K) // MOE_BLOCK) + N_EXPERTS
    block_row = jnp.arange(n_blocks, dtype=jnp.int32) * MOE_BLOCK
    block_e = jnp.minimum(jnp.sum((pad_end[None, :] <= block_row[:, None]).astype(jnp.int32), axis=1),
                          N_EXPERTS - 1)
    n_used = (pad_end[-1:] // MOE_BLOCK).astype(jnp.int32)

    def dest_of(ri):
        ids = ri[..., :TOP_K]
        hot = ids[..., None] == jnp.arange(N_EXPERTS, dtype=jnp.int32)
        return ri[..., TOP_K:2 * TOP_K] + jnp.sum(jnp.where(hot, pad_start, 0), axis=-1)

    dest_p = dest_of(rip)
    dest_s = dest_of(ris)
    n_rows = n_blocks * MOE_BLOCK
    zfrom = jnp.minimum((pad_start + counts) // SUBLANES * SUBLANES, n_rows - ZERO_ROWS).astype(jnp.int32)
    xs = _dispatch_first(zfrom, dest_p, h2p.reshape(bp * sp, D_MODEL), n_rows, tm=_tile(bp * sp, 1024))
    xs = _dispatch_more(dest_s, h2s.reshape(bs * ss, D_MODEL), xs, tm=_tile(bs * ss, 512))
    ys = _experts(block_e, n_used, xs, w)
    y_p = _combine(dest_p, x1p, rgp, mod_p, w["g_post_ffn"], ys, tm=_tile(sp, 512))
    y_s = _combine(dest_s, x1s, rgs, mod_s, w["g_post_ffn"], ys, tm=_tile(ss, 256))
    return (y_p, y_s, kv_p[None], kpe_p[None], pool_p[None], kv_s[None], kpe_s[None], pool_s[None])
```

```python
import functools

import jax
import jax.numpy as jnp
from jax import lax
from jax.experimental import pallas as pl
from jax.experimental.pallas import tpu as pltpu

F32 = jnp.float32
BF16 = jnp.bfloat16

D_MODEL = 1024
CHUNK = 64
N_HEADS = 8
QK_NOPE = 64
QK_ROPE = 32
QK_HEAD = QK_NOPE + QK_ROPE
V_HEAD = 64
Q_LORA = 384
KV_LORA = 256
ROPE_THETA = 10000.0
POOL_WINDOWS = (2, 4, 8, 16)
POOL_GROUP_DIM = 128
POOL_WIDTH = POOL_GROUP_DIM * len(POOL_WINDOWS)
POOL_HIST = max(POOL_WINDOWS) - 1
HIST_ROWS = POOL_HIST + 1
N_EXPERTS = 32
TOP_K = 4
D_FF = 1024
SWIGLU_LIMIT = 7.0
SWIGLU_ALPHA = 1.702
N_MOD = 6
EPS = 1e-6

LANES = 128
HEAD_SLABS = N_HEADS * LANES
ONE_LANE = V_HEAD
IN_EXT = Q_LORA + KV_LORA + POOL_WIDTH + 2 * LANES
U_OFF = Q_LORA + KV_LORA
KPE_OFF = U_OFF + POOL_WIDTH
SM_SCALE = QK_HEAD ** -0.5
LOG2_E = 1.4426950408889634
Q_SCALE = SM_SCALE * LOG2_E
NEG_INF = float("-inf")

SUBLANES = 8
ROW_TILE = (SUBLANES, D_MODEL // SUBLANES)
MOE_BLOCK = 512
ZERO_ROWS = MOE_BLOCK + SUBLANES
VMEM_LIMIT = 56 * 1024 * 1024


def _cparams(n_axes, vmem=VMEM_LIMIT):
    return pltpu.CompilerParams(dimension_semantics=("arbitrary",) * n_axes, vmem_limit_bytes=vmem)


def _rms(x, g):
    return x * lax.rsqrt(jnp.mean(x * x, axis=-1, keepdims=True) + EPS) * g


def _ada_kernel(c_ref, w_ref, b_ref, o_ref):
    c = c_ref[...]
    s = (c * jax.nn.sigmoid(c)).astype(BF16)
    o_ref[...] = jnp.dot(s, w_ref[...].astype(BF16), preferred_element_type=F32) + b_ref[...]


def _ada(c, w_ada, b_ada):
    nb = c.shape[0]
    return pl.pallas_call(
        _ada_kernel,
        out_shape=jax.ShapeDtypeStruct((nb, N_MOD * D_MODEL), F32),
        grid=(N_MOD,),
        in_specs=[pl.BlockSpec((nb, D_MODEL), lambda j: (0, 0)),
                  pl.BlockSpec((D_MODEL, D_MODEL), lambda j: (0, j)),
                  pl.BlockSpec((1, D_MODEL), lambda j: (0, j))],
        out_specs=pl.BlockSpec((nb, D_MODEL), lambda j: (0, j)),
        compiler_params=_cparams(1),
        name="ada",
    )(c, w_ada, b_ada.reshape(1, -1))


def _store_kv(kvb, kslab, wk_ref, wvt_ref, k_ref, vt_ref):
    kk = jnp.dot(kvb, wk_ref[...], preferred_element_type=F32)
    for hd in range(N_HEADS):
        sl = slice(hd * LANES, (hd + 1) * LANES)
        k_ref[:, sl] = (kk[:, sl] + kslab).astype(BF16)
    vt = lax.dot_general(wvt_ref[...], kvb, (((1,), (1,)), ((), ())), preferred_element_type=F32)
    row = lax.broadcasted_iota(jnp.int32, (HEAD_SLABS, 1), 0)
    vt_ref[...] = (vt + (row % LANES == ONE_LANE).astype(F32)).astype(BF16)


def _premix_kernel(x_ref, mod_ref, cos_ref, sin_ref, hist_ref, gpre_ref, gqa_ref, gkv_ref, pscale_ref,
                   win_ref, wq_ref, wk_ref, wvt_ref, wpool_ref,
                   q_ref, k_ref, vt_ref, kv_ref, kpe_ref, pooled_ref, poolnew_ref,
                   ubuf, *, tm, pos0):
    t = pl.program_id(1)
    x = x_ref[...]
    h = (_rms(x, gpre_ref[...]) * (1.0 + mod_ref[1:2, :]) + mod_ref[0:1, :]).astype(BF16)
    z = jnp.dot(h, win_ref[...], preferred_element_type=F32)
    cosv = cos_ref[...]
    sinv = sin_ref[...]

    qan = _rms(z[:, :Q_LORA], gqa_ref[...]).astype(BF16)
    qq = jnp.dot(qan, wq_ref[...], preferred_element_type=F32)
    for hd in range(N_HEADS):
        a = qq[:, hd * LANES:(hd + 1) * LANES]
        b = qq[:, HEAD_SLABS + hd * LANES:HEAD_SLABS + (hd + 1) * LANES]
        q_ref[:, hd * LANES:(hd + 1) * LANES] = ((a * cosv + b * sinv) * Q_SCALE).astype(BF16)

    kvn = _rms(z[:, Q_LORA:U_OFF], gkv_ref[...])
    kv_ref[...] = kvn
    kslab = z[:, KPE_OFF:KPE_OFF + LANES] * cosv + z[:, KPE_OFF + LANES:KPE_OFF + 2 * LANES] * sinv
    kpe_ref[...] = kslab
    _store_kv(kvn.astype(BF16), kslab, wk_ref, wvt_ref, k_ref, vt_ref)

    @pl.when(t == 0)
    def _():
        ubuf[0:HIST_ROWS, :] = hist_ref[...]

    @pl.when(t > 0)
    def _():
        ubuf[0:HIST_ROWS, :] = ubuf[tm:tm + HIST_ROWS, :]

    ubuf[HIST_ROWS:HIST_ROWS + tm, :] = z[:, U_OFF:KPE_OFF]
    pos = pos0 + t * tm + lax.broadcasted_iota(jnp.int32, (tm, 1), 0)
    for g, w in enumerate(POOL_WINDOWS):
        sl = slice(g * POOL_GROUP_DIM, (g + 1) * POOL_GROUP_DIM)
        u = ubuf[HIST_ROWS:HIST_ROWS + tm, sl]
        acc = u
        for j in range(1, w):
            acc = acc + ubuf[HIST_ROWS - j:HIST_ROWS - j + tm, sl]
        cnt = jnp.minimum(pos + 1, w).astype(F32)
        d = (acc / cnt - u).astype(BF16)
        y = jnp.dot(d, wpool_ref[g], preferred_element_type=F32) * pscale_ref[:, sl]
        pooled_ref[:, sl] = y.astype(BF16)
    poolnew_ref[...] = ubuf[tm:tm + HIST_ROWS, :]


def _premix(x, mod, cos_t, sin_t, hist, w, *, tm, pos0):
    nb, nt, _ = x.shape
    grid = (nb, nt // tm)
    tok = lambda width: pl.BlockSpec((None, tm, width), lambda b, t: (b, t, 0))
    full = lambda a: pl.BlockSpec(a.shape, lambda b, t: (0,) * a.ndim)
    tab = pl.BlockSpec((tm, LANES), lambda b, t: (t, 0))
    perb = lambda rows, width: pl.BlockSpec((None, rows, width), lambda b, t: (b, 0, 0))
    outs = [
        jax.ShapeDtypeStruct((nb, nt, HEAD_SLABS), BF16),
        jax.ShapeDtypeStruct((nb, nt, HEAD_SLABS), BF16),
        jax.ShapeDtypeStruct((nb, HEAD_SLABS, nt), BF16),
        jax.ShapeDtypeStruct((nb, nt, KV_LORA), F32),
        jax.ShapeDtypeStruct((nb, nt, LANES), F32),
        jax.ShapeDtypeStruct((nb, nt, POOL_WIDTH), BF16),
        jax.ShapeDtypeStruct((nb, HIST_ROWS, POOL_WIDTH), F32),
    ]
    return pl.pallas_call(
        functools.partial(_premix_kernel, tm=tm, pos0=pos0),
        out_shape=outs,
        grid=grid,
        in_specs=[tok(D_MODEL), perb(N_MOD, D_MODEL), tab, tab, perb(HIST_ROWS, POOL_WIDTH),
                  full(w["g_pre_mix"]), full(w["g_q_a"]), full(w["g_kv_a"]), full(w["pool_scale"]),
                  full(w["w_in"]), full(w["w_q"]), full(w["w_k"]), full(w["w_vt"]), full(w["w_pool"])],
        out_specs=[tok(HEAD_SLABS), tok(HEAD_SLABS),
                   pl.BlockSpec((None, HEAD_SLABS, tm), lambda b, t: (b, 0, t)),
                   tok(KV_LORA), tok(LANES), tok(POOL_WIDTH), perb(HIST_ROWS, POOL_WIDTH)],
        scratch_shapes=[pltpu.VMEM((HIST_ROWS + tm, POOL_WIDTH), F32)],
        compiler_params=_cparams(2),
        name="premix",
    )(x, mod, cos_t, sin_t, hist, w["g_pre_mix"], w["g_q_a"], w["g_kv_a"], w["pool_scale"],
      w["w_in"], w["w_q"], w["w_k"], w["w_vt"], w["w_pool"])


def _kvproj_kernel(kv_ref, kpe_ref, wk_ref, wvt_ref, k_ref, vt_ref):
    _store_kv(kv_ref[...].astype(BF16), kpe_ref[...], wk_ref, wvt_ref, k_ref, vt_ref)


def _kvproj(kv, kpe_slab, w_k, w_vt, *, tm):
    nb, nt, _ = kv.shape
    tok = lambda width: pl.BlockSpec((None, tm, width), lambda b, t: (b, t, 0))
    full = lambda a: pl.BlockSpec(a.shape, lambda b, t: (0,) * a.ndim)
    return pl.pallas_call(
        _kvproj_kernel,
        out_shape=[jax.ShapeDtypeStruct((nb, nt, HEAD_SLABS), BF16),
                   jax.ShapeDtypeStruct((nb, HEAD_SLABS, nt), BF16)],
        grid=(nb, nt // tm),
        in_specs=[tok(KV_LORA), tok(LANES), full(w_k), full(w_vt)],
        out_specs=[tok(HEAD_SLABS), pl.BlockSpec((None, HEAD_SLABS, tm), lambda b, t: (b, 0, t))],
        compiler_params=_cparams(2),
        name="kvproj",
    )(kv, kpe_slab, w_k, w_vt)


def _attn_kernel(q_ref, k_ref, vt_ref, o_ref, m_ref, acc_ref, *, tq, tk, nk, causal, kv_len):
    qi = pl.program_id(1)
    ki = pl.program_id(2)
    last = ((qi + 1) * tq - 1) // tk if causal else nk - 1
    need_len_mask = kv_len < nk * tk

    @pl.when(ki == 0)
    def _():
        m_ref[...] = jnp.full(m_ref.shape, NEG_INF, F32)
        acc_ref[...] = jnp.zeros(acc_ref.shape, F32)

    def step(masked, nkeys=tk):
        if masked:
            kpos = ki * tk + lax.broadcasted_iota(jnp.int32, (nkeys, tq), 0)
            vis = None
            if causal:
                qpos = qi * tq + lax.broadcasted_iota(jnp.int32, (nkeys, tq), 1)
                vis = (kpos // CHUNK) <= (qpos // CHUNK)
            if need_len_mask:
                lm = kpos < kv_len
                vis = lm if vis is None else (vis & lm)

        def scores(hd):
            sl = slice(hd * LANES, (hd + 1) * LANES)
            return lax.dot_general(k_ref[:nkeys, sl], q_ref[:, sl], (((1,), (1,)), ((), ())),
                                   preferred_element_type=F32)

        def accumulate(hd, alpha, p):
            sl = slice(hd * LANES, (hd + 1) * LANES)
            acc_ref[hd] = alpha * acc_ref[hd] + jnp.dot(vt_ref[sl, :nkeys], p, preferred_element_type=F32)

        s_next = scores(0)
        pending = None
        for hd in range(N_HEADS):
            s = s_next
            if hd + 1 < N_HEADS:
                s_next = scores(hd + 1)
            if pending is not None:
                accumulate(*pending)
            if masked:
                s = jnp.where(vis, s, NEG_INF)
            m_prev = m_ref[hd:hd + 1, :]
            m_new = jnp.maximum(m_prev, jnp.max(s, axis=0, keepdims=True))
            m_ref[hd:hd + 1, :] = m_new
            pending = (hd, jnp.exp2(m_prev - m_new), jnp.exp2(s - m_new).astype(BF16))
        accumulate(*pending)

    if causal and tk == 2 * tq and not need_len_mask:
        @pl.when(ki < last)
        def _():
            step(False)

        @pl.when((ki == last) & (qi % 2 == 0))
        def _():
            step(True, tq)

        @pl.when((ki == last) & (qi % 2 == 1))
        def _():
            step(True)
    elif causal or need_len_mask:
        @pl.when(ki < last)
        def _():
            step(False)

        @pl.when(ki == last)
        def _():
            step(True)
    else:
        step(False)

    @pl.when(ki == nk - 1)
    def _():
        for hd in range(N_HEADS):
            acc = acc_ref[hd]
            out_t = acc / acc[ONE_LANE:ONE_LANE + 1, :]
            o_ref[:, hd * LANES:(hd + 1) * LANES] = out_t.T.astype(BF16)


def _attention(q, k, vt, *, tq, tk, causal, kv_len):
    nb, nq_tot, _ = q.shape
    nk = k.shape[1] // tk
    nq = nq_tot // tq
    if causal:
        last = lambda i: ((i + 1) * tq - 1) // tk
        kmap = lambda b, i, j: (b, jnp.minimum(j, last(i)), 0)
        vmap = lambda b, i, j: (b, 0, jnp.minimum(j, last(i)))
    else:
        kmap = lambda b, i, j: (b, j, 0)
        vmap = lambda b, i, j: (b, 0, j)
    return pl.pallas_call(
        functools.partial(_attn_kernel, tq=tq, tk=tk, nk=nk, causal=causal, kv_len=kv_len),
        out_shape=jax.ShapeDtypeStruct((nb, nq_tot, HEAD_SLABS), BF16),
        grid=(nb, nq, nk),
        in_specs=[pl.BlockSpec((None, tq, HEAD_SLABS), lambda b, i, j: (b, i, 0)),
                  pl.BlockSpec((None, tk, HEAD_SLABS), kmap),
                  pl.BlockSpec((None, HEAD_SLABS, tk), vmap)],
        out_specs=pl.BlockSpec((None, tq, HEAD_SLABS), lambda b, i, j: (b, i, 0)),
        scratch_shapes=[pltpu.VMEM((N_HEADS, tq), F32), pltpu.VMEM((N_HEADS, LANES, tq), F32)],
        compiler_params=_cparams(3),
        name="attention",
    )(q, k, vt)


def _postmix_kernel(attn_ref, pooled_ref, x_ref, mod_ref, cnt0_ref, gpost_ref, gffn_ref,
                    woa_ref, wop_ref, wr_ref, br_ref,
                    x1_ref, h2_ref, ri_ref, rg_ref, cnt_ref,
                    carry, ltri, *, tm):
    first = (pl.program_id(0) == 0) & (pl.program_id(1) == 0)

    @pl.when(first)
    def _():
        carry[...] = cnt0_ref[...]
        r = lax.broadcasted_iota(jnp.int32, (tm, tm), 0)
        c = lax.broadcasted_iota(jnp.int32, (tm, tm), 1)
        ltri[...] = (r > c).astype(BF16)

    mix = (jnp.dot(attn_ref[...], woa_ref[...], preferred_element_type=F32)
           + jnp.dot(pooled_ref[...], wop_ref[...], preferred_element_type=F32))
    x1 = x_ref[...] + mod_ref[2:3, :] * _rms(mix, gpost_ref[...])
    x1_ref[...] = x1
    h2 = _rms(x1, gffn_ref[...]) * (1.0 + mod_ref[4:5, :]) + mod_ref[3:4, :]
    h2_ref[...] = h2.reshape(tm, *ROW_TILE)

    lane = lax.broadcasted_iota(jnp.int32, (tm, LANES), 1).astype(F32)
    logits = jnp.dot(h2.astype(BF16), wr_ref[...], preferred_element_type=F32) + br_ref[...]
    logits = jnp.where(lane < N_EXPERTS, logits, NEG_INF)
    sel = jnp.zeros((tm, LANES), F32)
    ids, vals, hots = [], [], []
    for _ in range(TOP_K):
        mk = jnp.max(logits, axis=1, keepdims=True)
        idx = jnp.min(jnp.where(logits == mk, lane, float(LANES)), axis=1, keepdims=True)
        hot = lane == idx
        logits = jnp.where(hot, NEG_INF, logits)
        sel = sel + hot.astype(F32)
        ids.append(idx)
        vals.append(mk)
        hots.append(hot)
    ex = [jnp.exp(vk - vals[0]) for vk in vals]
    denom = ex[0] + ex[1] + ex[2] + ex[3]

    before = jnp.dot(ltri[...], sel.astype(BF16), preferred_element_type=F32) + carry[0:1, :]
    ri = jnp.zeros((tm, LANES), F32)
    rg = jnp.zeros((tm, LANES), F32)
    for kk in range(TOP_K):
        rank = jnp.sum(jnp.where(hots[kk], before, 0.0), axis=1, keepdims=True)
        ri = jnp.where(lane == kk, ids[kk], ri)
        ri = jnp.where(lane == TOP_K + kk, rank, ri)
        rg = jnp.where(lane == kk, ex[kk] / denom, rg)
    ri_ref[...] = ri.astype(jnp.int32)
    rg_ref[...] = rg
    carry[0:1, :] = carry[0:1, :] + jnp.sum(sel, axis=0, keepdims=True)
    cnt_ref[...] = carry[...]


def _postmix(attn, pooled, x, mod, cnt0, w, *, tm):
    nb, nt, _ = x.shape
    tok = lambda width: pl.BlockSpec((None, tm, width), lambda b, t: (b, t, 0))
    full = lambda a: pl.BlockSpec(a.shape, lambda b, t: (0,) * a.ndim)
    outs = [
        jax.ShapeDtypeStruct((nb, nt, D_MODEL), F32),
        jax.ShapeDtypeStruct((nb, nt) + ROW_TILE, F32),
        jax.ShapeDtypeStruct((nb, nt, LANES), jnp.int32),
        jax.ShapeDtypeStruct((nb, nt, LANES), F32),
        jax.ShapeDtypeStruct((8, LANES), F32),
    ]
    return pl.pallas_call(
        functools.partial(_postmix_kernel, tm=tm),
        out_shape=outs,
        grid=(nb, nt // tm),
        in_specs=[tok(HEAD_SLABS), tok(POOL_WIDTH), tok(D_MODEL),
                  pl.BlockSpec((None, N_MOD, D_MODEL), lambda b, t: (b, 0, 0)),
                  full(cnt0), full(w["g_post_mix"]), full(w["g_pre_ffn"]),
                  full(w["w_o_attn"]), full(w["w_o_pool"]), full(w["w_router"]), full(w["b_router"])],
        out_specs=[tok(D_MODEL), pl.BlockSpec((None, tm) + ROW_TILE, lambda b, t: (b, t, 0, 0)), tok(LANES), tok(LANES),
                   pl.BlockSpec((8, LANES), lambda b, t: (0, 0))],
        scratch_shapes=[pltpu.VMEM((8, LANES), F32), pltpu.VMEM((tm, tm), BF16)],
        compiler_params=_cparams(2),
        name="postmix",
    )(attn, pooled, x, mod, cnt0, w["g_post_mix"], w["g_pre_ffn"],
      w["w_o_attn"], w["w_o_pool"], w["w_router"], w["b_router"])


def _dispatch_rows(dest_ref, h_ref, xs_ref, sem, tm):
    def issue(j, carry):
        for s in range(SUBLANES):
            r = j * SUBLANES + s
            for kk in range(TOP_K):
                d = dest_ref[0, 0, j * (SUBLANES * TOP_K) + s * TOP_K + kk]
                pltpu.make_async_copy(h_ref.at[r], xs_ref.at[d], sem).start()
        return carry

    lax.fori_loop(0, tm // SUBLANES, issue, 0)
    for _ in range(TOP_K):
        pltpu.make_async_copy(xs_ref.at[pl.ds(0, tm)], xs_ref.at[pl.ds(0, tm)], sem).wait()


def _dispatch_first_kernel(zfrom_ref, dest_ref, h_ref, xs_ref, zbuf, sem, zsem, *, tm):
    @pl.when(pl.program_id(0) == 0)
    def _():
        zbuf[...] = jnp.zeros(zbuf.shape, F32)
        for e in range(N_EXPERTS):
            start = pl.multiple_of(zfrom_ref[e], SUBLANES)
            cp = pltpu.make_async_copy(zbuf, xs_ref.at[pl.ds(start, ZERO_ROWS)], zsem)
            cp.start()
            cp.wait()

    _dispatch_rows(dest_ref, h_ref, xs_ref, sem, tm)


def _dispatch_more_kernel(dest_ref, h_ref, xs_in_ref, xs_ref, sem, *, tm):
    del xs_in_ref
    _dispatch_rows(dest_ref, h_ref, xs_ref, sem, tm)


def _dispatch_first(zfrom, dest, h, n_rows, *, tm):
    nt = h.shape[0] // tm
    return pl.pallas_call(
        functools.partial(_dispatch_first_kernel, tm=tm),
        out_shape=jax.ShapeDtypeStruct((n_rows,) + ROW_TILE, F32),
        grid_spec=pltpu.PrefetchScalarGridSpec(
            num_scalar_prefetch=1,
            grid=(nt,),
            in_specs=[pl.BlockSpec((1, 1, tm * TOP_K), lambda i, z: (i, 0, 0), memory_space=pltpu.SMEM),
                      pl.BlockSpec((tm,) + ROW_TILE, lambda i, z: (i, 0, 0))],
            out_specs=pl.BlockSpec(memory_space=pl.ANY),
            scratch_shapes=[pltpu.VMEM((ZERO_ROWS,) + ROW_TILE, F32),
                            pltpu.SemaphoreType.DMA(()), pltpu.SemaphoreType.DMA(())],
        ),
        compiler_params=_cparams(1),
        name="dispatch_first",
    )(zfrom, dest.reshape(nt, 1, tm * TOP_K), h)


def _dispatch_more(dest, h, xs, *, tm):
    nt = h.shape[0] // tm
    return pl.pallas_call(
        functools.partial(_dispatch_more_kernel, tm=tm),
        out_shape=jax.ShapeDtypeStruct(xs.shape, xs.dtype),
        grid=(nt,),
        in_specs=[pl.BlockSpec((1, 1, tm * TOP_K), lambda i: (i, 0, 0), memory_space=pltpu.SMEM),
                  pl.BlockSpec((tm,) + ROW_TILE, lambda i: (i, 0, 0)),
                  pl.BlockSpec(memory_space=pl.ANY)],
        out_specs=pl.BlockSpec(memory_space=pl.ANY),
        scratch_shapes=[pltpu.SemaphoreType.DMA(())],
        input_output_aliases={2: 0},
        compiler_params=_cparams(1),
        name="dispatch_more",
    )(dest.reshape(nt, 1, tm * TOP_K), h, xs)


def _expert_kernel(be_ref, nu_ref, xs_ref, wgu_ref, bgu_ref, wd_ref, bd_ref, ys_ref, wgu_b, wd_b):
    i = pl.program_id(0)

    @pl.when(i < nu_ref[0])
    def _():
        @pl.when((i == 0) | (be_ref[i] != be_ref[jnp.maximum(i - 1, 0)]))
        def _():
            wgu_b[...] = wgu_ref[...].astype(BF16)
            wd_b[...] = wd_ref[...].astype(BF16)

        x = xs_ref[...].reshape(MOE_BLOCK, D_MODEL).astype(BF16)
        gu = jnp.dot(x, wgu_b[...], preferred_element_type=F32) + bgu_ref[...]
        g = jnp.minimum(gu[:, :D_FF], SWIGLU_LIMIT)
        u = jnp.clip(gu[:, D_FF:], -SWIGLU_LIMIT, SWIGLU_LIMIT)
        a = (u + 1.0) * (g * jax.nn.sigmoid(SWIGLU_ALPHA * g))
        y = jnp.dot(a.astype(BF16), wd_b[...], preferred_element_type=F32) + bd_ref[...]
        ys_ref[...] = y.reshape(MOE_BLOCK, *ROW_TILE)


def _experts(block_e, n_used, xs, w):
    nblk = xs.shape[0] // MOE_BLOCK
    row = lambda i, be, nu: (jnp.minimum(i, nu[0] - 1), 0, 0)
    per_e = lambda i, be, nu: (be[i], 0, 0)
    return pl.pallas_call(
        _expert_kernel,
        out_shape=jax.ShapeDtypeStruct(xs.shape, F32),
        grid_spec=pltpu.PrefetchScalarGridSpec(
            num_scalar_prefetch=2,
            grid=(nblk,),
            in_specs=[pl.BlockSpec((MOE_BLOCK,) + ROW_TILE, row),
                      pl.BlockSpec((None, D_MODEL, 2 * D_FF), per_e),
                      pl.BlockSpec((None, 1, 2 * D_FF), per_e),
                      pl.BlockSpec((None, D_FF, D_MODEL), per_e),
                      pl.BlockSpec((None, 1, D_MODEL), per_e)],
            out_specs=pl.BlockSpec((MOE_BLOCK,) + ROW_TILE, row),
            scratch_shapes=[pltpu.VMEM((D_MODEL, 2 * D_FF), BF16), pltpu.VMEM((D_FF, D_MODEL), BF16)],
        ),
        compiler_params=_cparams(1),
        name="experts",
    )(block_e, n_used, xs, w["w_gu"], w["b_gu"], w["w_down"], w["b_down"])


def _combine_kernel(dcur_ref, dnext_ref, x1_ref, rg_ref, mod_ref, gpost_ref, ys_ref, o_ref, gbuf, sem, *, tm, n):
    i = pl.program_id(0)
    slot = i % 2

    def gather(dest_ref, sl):
        def issue(j, carry):
            for s in range(SUBLANES):
                r = j * SUBLANES + s
                for kk in range(TOP_K):
                    d = dest_ref[0, 0, j * (SUBLANES * TOP_K) + s * TOP_K + kk]
                    pltpu.make_async_copy(ys_ref.at[d], gbuf.at[sl, kk, r], sem.at[sl]).start()
            return carry

        lax.fori_loop(0, tm // SUBLANES, issue, 0)

    @pl.when(i == 0)
    def _():
        gather(dcur_ref, 0)

    @pl.when(i + 1 < n)
    def _():
        gather(dnext_ref, 1 - slot)

    for kk in range(TOP_K):
        pltpu.make_async_copy(gbuf.at[1 - slot, kk], gbuf.at[slot, kk], sem.at[slot]).wait()

    rg = rg_ref[...]
    y = rg[:, 0:1] * gbuf[slot, 0].reshape(tm, D_MODEL)
    for kk in range(1, TOP_K):
        y = y + rg[:, kk:kk + 1] * gbuf[slot, kk].reshape(tm, D_MODEL)
    o_ref[...] = x1_ref[...] + mod_ref[5:6, :] * _rms(y, gpost_ref[...])


def _combine(dest, x1, rg, mod, g_post, ys, *, tm):
    nb, nt, _ = x1.shape
    ntile = nt // tm
    n = nb * ntile
    tok = lambda width: pl.BlockSpec((tm, width), lambda i: (i, 0))
    dest = dest.reshape(n, 1, tm * TOP_K)
    dspec = lambda f: pl.BlockSpec((1, 1, tm * TOP_K), lambda i: (f(i), 0, 0), memory_space=pltpu.SMEM)
    out = pl.pallas_call(
        functools.partial(_combine_kernel, tm=tm, n=n),
        out_shape=jax.ShapeDtypeStruct((nb * nt, D_MODEL), F32),
        grid=(n,),
        in_specs=[dspec(lambda i: i), dspec(lambda i: jnp.minimum(i + 1, n - 1)),
                  tok(D_MODEL), tok(LANES),
                  pl.BlockSpec((None, N_MOD, D_MODEL), lambda i: (i // ntile, 0, 0)),
                  pl.BlockSpec(g_post.shape, lambda i: (0, 0)),
                  pl.BlockSpec(memory_space=pl.ANY)],
        out_specs=tok(D_MODEL),
        scratch_shapes=[pltpu.VMEM((2, TOP_K, tm) + ROW_TILE, F32),
                        pltpu.SemaphoreType.DMA((2,))],
        compiler_params=_cparams(1),
        name="combine",
    )(dest, dest, x1.reshape(nb * nt, D_MODEL), rg.reshape(nb * nt, LANES), mod, g_post, ys)
    return out.reshape(x1.shape)


def _rot_swap(w):
    half = QK_ROPE // 2
    return jnp.concatenate([-w[..., half:], w[..., :half]], axis=-1)


def _prep_weights(w_in, g_q_a, w_q_b, g_kv_a, w_uk, w_uv, w_pool, pool_scale, w_o, g_pre_mix, g_post_mix,
                  g_pre_ffn, w_router, b_router, w_gu, b_gu, w_down, b_down, g_post_ffn):
    row = lambda a: a.reshape(1, -1).astype(F32)
    w_kpe = w_in[:, U_OFF:U_OFF + QK_ROPE]
    zeros = lambda *s: jnp.zeros(s, F32)
    d = D_MODEL
    slab = lambda a: jnp.concatenate([zeros(d, QK_NOPE), a, zeros(d, LANES - QK_HEAD)], axis=1)
    w_in_ext = jnp.concatenate([w_in[:, :U_OFF], w_in[:, U_OFF + QK_ROPE:], slab(w_kpe), slab(_rot_swap(w_kpe))],
                               axis=1)
    pad_q = zeros(Q_LORA, N_HEADS, LANES - QK_HEAD)
    wq_plain = jnp.concatenate([w_q_b, pad_q], axis=2).reshape(Q_LORA, HEAD_SLABS)
    wq_swap = jnp.concatenate([zeros(Q_LORA, N_HEADS, QK_NOPE), _rot_swap(w_q_b[..., QK_NOPE:]), pad_q],
                              axis=2).reshape(Q_LORA, HEAD_SLABS)
    pad_kv = zeros(KV_LORA, N_HEADS, LANES - QK_NOPE)
    wk = jnp.concatenate([w_uk, pad_kv], axis=2).reshape(KV_LORA, HEAD_SLABS)
    wv = jnp.concatenate([w_uv, pad_kv], axis=2).reshape(KV_LORA, HEAD_SLABS)
    mla_w = N_HEADS * V_HEAD
    woa = jnp.concatenate([w_o[:mla_w].reshape(N_HEADS, V_HEAD, d), zeros(N_HEADS, LANES - V_HEAD, d)],
                          axis=1).reshape(HEAD_SLABS, d)
    return {
        "g_pre_mix": row(g_pre_mix), "g_q_a": row(g_q_a), "g_kv_a": row(g_kv_a), "pool_scale": row(pool_scale),
        "w_in": w_in_ext.astype(BF16),
        "w_q": jnp.concatenate([wq_plain, wq_swap], axis=1).astype(BF16),
        "w_k": wk.astype(BF16), "w_vt": wv.T.astype(BF16),
        "w_pool": w_pool.astype(BF16),
        "g_post_mix": row(g_post_mix), "g_pre_ffn": row(g_pre_ffn), "g_post_ffn": row(g_post_ffn),
        "w_o_attn": woa.astype(BF16), "w_o_pool": w_o[mla_w:].astype(BF16),
        "w_router": jnp.pad(w_router, ((0, 0), (0, LANES - N_EXPERTS))).astype(BF16),
        "b_router": jnp.pad(b_router, (0, LANES - N_EXPERTS)).reshape(1, LANES).astype(F32),
        "w_gu": w_gu, "b_gu": b_gu.reshape(N_EXPERTS, 1, 2 * D_FF).astype(F32),
        "w_down": w_down, "b_down": b_down.reshape(N_EXPERTS, 1, D_MODEL).astype(F32),
    }


def _rope_tables(pos):
    half = QK_ROPE // 2
    inv = ROPE_THETA ** (-jnp.arange(half, dtype=F32) / half)
    ang = pos.astype(F32)[:, None] * inv[None, :]
    cos, sin = jnp.cos(ang), jnp.sin(ang)
    n = pos.shape[0]
    cos_t = jnp.concatenate([jnp.ones((n, QK_NOPE), F32), cos, cos, jnp.zeros((n, LANES - QK_HEAD), F32)], axis=1)
    sin_t = jnp.concatenate([jnp.zeros((n, QK_NOPE), F32), sin, sin, jnp.zeros((n, LANES - QK_HEAD), F32)], axis=1)
    return cos_t, sin_t


def _tile(n, pref):
    return pref if n % pref == 0 else n


def _mixer_path(x, mod, pos0, hist, cache, w, cnt0):
    nb, nt, _ = x.shape
    tm = _tile(nt, 512)
    cos_t, sin_t = _rope_tables(pos0 + jnp.arange(nt, dtype=jnp.int32))
    q, k, vt, kv_new, kslab, pooled, pool_tail = _premix(x, mod, cos_t, sin_t, hist, w, tm=tm, pos0=pos0)
    if cache is None:
        attn = _attention(q, k, vt, tq=tm, tk=_tile(nt, 2 * tm), causal=True, kv_len=nt)
    else:
        ckv, ckpe = cache
        past = ckv.shape[1]
        kv_len = past + nt
        tk = -(-kv_len // 256) * 256
        ckpe_slab = jnp.pad(ckpe, ((0, 0), (0, 0), (QK_NOPE, LANES - QK_HEAD)))
        lat_all = jnp.concatenate([ckv, kv_new, jnp.zeros((nb, tk - kv_len, KV_LORA), F32)], axis=1)
        kpe_all = jnp.concatenate([ckpe_slab, kslab, jnp.zeros((nb, tk - kv_len, LANES), F32)], axis=1)
        k_all, vt_all = _kvproj(lat_all, kpe_all, w["w_k"], w["w_vt"], tm=_tile(tk, 768))
        tq = -(-nt // LANES) * LANES
        q_pad = jnp.pad(q, ((0, 0), (0, tq - nt), (0, 0)))
        attn = _attention(q_pad, k_all, vt_all, tq=tq, tk=tk, causal=False, kv_len=kv_len)[:, :nt]
    x1, h2, ri, rg, cnt = _postmix(attn, pooled, x, mod, cnt0, w, tm=tm)
    return x1, h2, ri, rg, cnt, kv_new, kslab[..., QK_NOPE:QK_HEAD], pool_tail[:, 1:]


def kernel(x_prompt, x_sample, c_prompt, c_sample, cache_kv_latent, cache_k_rope, state_pool, w_ada, b_ada,
           g_pre_mix, w_in, g_q_a, w_q_b, g_kv_a, w_uk, w_uv, w_pool, pool_scale, w_o, g_post_mix, g_pre_ffn,
           w_router, b_router, w_gu, b_gu, w_down, b_down, g_post_ffn):
    assert w_ada.shape[0] == 1, "single-layer step"
    bp, sp, _ = x_prompt.shape
    bs, ss, _ = x_sample.shape
    past = cache_kv_latent.shape[2]
    w = _prep_weights(w_in[0], g_q_a[0], w_q_b[0], g_kv_a[0], w_uk[0], w_uv[0], w_pool[0], pool_scale[0], w_o[0],
                      g_pre_mix[0], g_post_mix[0], g_pre_ffn[0], w_router[0], b_router[0], w_gu[0], b_gu[0],
                      w_down[0], b_down[0], g_post_ffn[0])

    mod = _ada(jnp.concatenate([c_prompt, c_sample], axis=0), w_ada[0], b_ada[0])
    mod = mod.reshape(bp + bs, N_MOD, D_MODEL)
    mod_p, mod_s = mod[:bp], mod[bp:]

    hist_p = jnp.zeros((bp, HIST_ROWS, POOL_WIDTH), F32)
    hist_s = jnp.pad(state_pool[0], ((0, 0), (1, 0), (0, 0)))
    cnt0 = jnp.zeros((8, LANES), F32)
    x1p, h2p, rip, rgp, cntp, kv_p, kpe_p, pool_p = _mixer_path(x_prompt, mod_p, 0, hist_p, None, w, cnt0)
    x1s, h2s, ris, rgs, cnts, kv_s, kpe_s, pool_s = _mixer_path(
        x_sample, mod_s, past, hist_s, (cache_kv_latent[0], cache_k_rope[0]), w, cntp)

    counts = cnts[0, :N_EXPERTS].astype(jnp.int32)
    padded = (counts + MOE_BLOCK - 1) // MOE_BLOCK * MOE_BLOCK
    pad_end = jnp.cumsum(padded)
    pad_start = pad_end - padded
    n_tok = bp * sp + bs * ss
    n_blocks = -(-(n_tok * TOP_K) // MOE_BLOCK) + N_EXPERTS
    block_row = jnp.arange(n_blocks, dtype=jnp.int32) * MOE_BLOCK
    block_e = jnp.minimum(jnp.sum((pad_end[None, :] <= block_row[:, None]).astype(jnp.int32), axis=1),
                          N_EXPERTS - 1)
    n_used = (pad_end[-1:] // MOE_BLOCK).astype(jnp.int32)

    def dest_of(ri):
        ids = ri[..., :TOP_K]
        hot = ids[..., None] == jnp.arange(N_EXPERTS, dtype=jnp.int32)
        return ri[..., TOP_K:2 * TOP_K] + jnp.sum(jnp.where(hot, pad_start, 0), axis=-1)

    dest_p = dest_of(rip)
    dest_s = dest_of(ris)
    n_rows = n_blocks * MOE_BLOCK
    zfrom = jnp.minimum((pad_start + counts) // SUBLANES * SUBLANES, n_rows - ZERO_ROWS).astype(jnp.int32)
    xs = _dispatch_first(zfrom, dest_p, h2p.reshape((bp * sp,) + ROW_TILE), n_rows, tm=_tile(bp * sp, 1024))
    xs = _dispatch_more(dest_s, h2s.reshape((bs * ss,) + ROW_TILE), xs, tm=_tile(bs * ss, 512))
    ys = _experts(block_e, n_used, xs, w)
    y_p = _combine(dest_p, x1p, rgp, mod_p, w["g_post_ffn"], ys, tm=_tile(sp, 512))
    y_s = _combine(dest_s, x1s, rgs, mod_s, w["g_post_ffn"], ys, tm=_tile(ss, 256))
    return (y_p, y_s, kv_p[None], kpe_p[None], pool_p[None], kv_s[None], kpe_s[None], pool_s[None])
```

```python
import functools

import jax
import jax.numpy as jnp
from jax import lax
from jax.experimental import pallas as pl
from jax.experimental.pallas import tpu as pltpu

F32 = jnp.float32
BF16 = jnp.bfloat16

D_MODEL = 1024
CHUNK = 64
N_HEADS = 8
QK_NOPE = 64
QK_ROPE = 32
QK_HEAD = QK_NOPE + QK_ROPE
V_HEAD = 64
Q_LORA = 384
KV_LORA = 256
ROPE_THETA = 10000.0
POOL_WINDOWS = (2, 4, 8, 16)
POOL_GROUP_DIM = 128
POOL_WIDTH = POOL_GROUP_DIM * len(POOL_WINDOWS)
POOL_HIST = max(POOL_WINDOWS) - 1
HIST_ROWS = POOL_HIST + 1
N_EXPERTS = 32
TOP_K = 4
D_FF = 1024
SWIGLU_LIMIT = 7.0
SWIGLU_ALPHA = 1.702
N_MOD = 6
EPS = 1e-6

LANES = 128
HEAD_SLABS = N_HEADS * LANES
ONE_LANE = V_HEAD
IN_EXT = Q_LORA + KV_LORA + POOL_WIDTH + 2 * LANES
U_OFF = Q_LORA + KV_LORA
KPE_OFF = U_OFF + POOL_WIDTH
SM_SCALE = QK_HEAD ** -0.5
LOG2_E = 1.4426950408889634
Q_SCALE = SM_SCALE * LOG2_E
NEG_INF = float("-inf")

SUBLANES = 8
ROW_TILE = (SUBLANES, D_MODEL // SUBLANES)
MOE_BLOCK = 512
VMEM_LIMIT = 56 * 1024 * 1024


def _cparams(n_axes, vmem=VMEM_LIMIT):
    return pltpu.CompilerParams(dimension_semantics=("arbitrary",) * n_axes, vmem_limit_bytes=vmem)


def _rms(x, g):
    return x * lax.rsqrt(jnp.mean(x * x, axis=-1, keepdims=True) + EPS) * g


def _ada_kernel(c_ref, w_ref, b_ref, o_ref):
    c = c_ref[...]
    s = (c * jax.nn.sigmoid(c)).astype(BF16)
    o_ref[...] = jnp.dot(s, w_ref[...].astype(BF16), preferred_element_type=F32) + b_ref[...]


def _ada(c, w_ada, b_ada):
    nb = c.shape[0]
    return pl.pallas_call(
        _ada_kernel,
        out_shape=jax.ShapeDtypeStruct((nb, N_MOD * D_MODEL), F32),
        grid=(N_MOD,),
        in_specs=[pl.BlockSpec((nb, D_MODEL), lambda j: (0, 0)),
                  pl.BlockSpec((D_MODEL, D_MODEL), lambda j: (0, j)),
                  pl.BlockSpec((1, D_MODEL), lambda j: (0, j))],
        out_specs=pl.BlockSpec((nb, D_MODEL), lambda j: (0, j)),
        compiler_params=_cparams(1),
        name="ada",
    )(c, w_ada, b_ada.reshape(1, -1))


def _store_kv(kvb, kslab, wk_ref, wvt_ref, k_ref, vt_ref):
    kk = jnp.dot(kvb, wk_ref[...], preferred_element_type=F32)
    for hd in range(N_HEADS):
        sl = slice(hd * LANES, (hd + 1) * LANES)
        k_ref[:, sl] = (kk[:, sl] + kslab).astype(BF16)
    vt = lax.dot_general(wvt_ref[...], kvb, (((1,), (1,)), ((), ())), preferred_element_type=F32)
    row = lax.broadcasted_iota(jnp.int32, (HEAD_SLABS, 1), 0)
    vt_ref[...] = (vt + (row % LANES == ONE_LANE).astype(F32)).astype(BF16)


def _premix_kernel(x_ref, mod_ref, cos_ref, sin_ref, hist_ref, gpre_ref, gqa_ref, gkv_ref, pscale_ref,
                   win_ref, wq_ref, wk_ref, wvt_ref, wpool_ref,
                   q_ref, k_ref, vt_ref, kv_ref, kpe_ref, pooled_ref, poolnew_ref,
                   ubuf, *, tm, pos0):
    t = pl.program_id(1)
    x = x_ref[...]
    h = (_rms(x, gpre_ref[...]) * (1.0 + mod_ref[1:2, :]) + mod_ref[0:1, :]).astype(BF16)
    z = jnp.dot(h, win_ref[...], preferred_element_type=F32)
    cosv = cos_ref[...]
    sinv = sin_ref[...]

    qan = _rms(z[:, :Q_LORA], gqa_ref[...]).astype(BF16)
    qq = jnp.dot(qan, wq_ref[...], preferred_element_type=F32)
    for hd in range(N_HEADS):
        a = qq[:, hd * LANES:(hd + 1) * LANES]
        b = qq[:, HEAD_SLABS + hd * LANES:HEAD_SLABS + (hd + 1) * LANES]
        q_ref[:, hd * LANES:(hd + 1) * LANES] = ((a * cosv + b * sinv) * Q_SCALE).astype(BF16)

    kvn = _rms(z[:, Q_LORA:U_OFF], gkv_ref[...])
    kv_ref[...] = kvn
    kslab = z[:, KPE_OFF:KPE_OFF + LANES] * cosv + z[:, KPE_OFF + LANES:KPE_OFF + 2 * LANES] * sinv
    kpe_ref[...] = kslab
    _store_kv(kvn.astype(BF16), kslab, wk_ref, wvt_ref, k_ref, vt_ref)

    @pl.when(t == 0)
    def _():
        ubuf[0:HIST_ROWS, :] = hist_ref[...]

    @pl.when(t > 0)
    def _():
        ubuf[0:HIST_ROWS, :] = ubuf[tm:tm + HIST_ROWS, :]

    ubuf[HIST_ROWS:HIST_ROWS + tm, :] = z[:, U_OFF:KPE_OFF]
    pos = pos0 + t * tm + lax.broadcasted_iota(jnp.int32, (tm, 1), 0)
    for g, w in enumerate(POOL_WINDOWS):
        sl = slice(g * POOL_GROUP_DIM, (g + 1) * POOL_GROUP_DIM)
        u = ubuf[HIST_ROWS:HIST_ROWS + tm, sl]
        acc = u
        for j in range(1, w):
            acc = acc + ubuf[HIST_ROWS - j:HIST_ROWS - j + tm, sl]
        cnt = jnp.minimum(pos + 1, w).astype(F32)
        d = (acc / cnt - u).astype(BF16)
        y = jnp.dot(d, wpool_ref[g], preferred_element_type=F32) * pscale_ref[:, sl]
        pooled_ref[:, sl] = y.astype(BF16)
    poolnew_ref[...] = ubuf[tm:tm + HIST_ROWS, :]


def _premix(x, mod, cos_t, sin_t, hist, w, *, tm, pos0):
    nb, nt, _ = x.shape
    grid = (nb, nt // tm)
    tok = lambda width: pl.BlockSpec((None, tm, width), lambda b, t: (b, t, 0))
    full = lambda a: pl.BlockSpec(a.shape, lambda b, t: (0,) * a.ndim)
    tab = pl.BlockSpec((tm, LANES), lambda b, t: (t, 0))
    perb = lambda rows, width: pl.BlockSpec((None, rows, width), lambda b, t: (b, 0, 0))
    outs = [
        jax.ShapeDtypeStruct((nb, nt, HEAD_SLABS), BF16),
        jax.ShapeDtypeStruct((nb, nt, HEAD_SLABS), BF16),
        jax.ShapeDtypeStruct((nb, HEAD_SLABS, nt), BF16),
        jax.ShapeDtypeStruct((nb, nt, KV_LORA), F32),
        jax.ShapeDtypeStruct((nb, nt, LANES), F32),
        jax.ShapeDtypeStruct((nb, nt, POOL_WIDTH), BF16),
        jax.ShapeDtypeStruct((nb, HIST_ROWS, POOL_WIDTH), F32),
    ]
    return pl.pallas_call(
        functools.partial(_premix_kernel, tm=tm, pos0=pos0),
        out_shape=outs,
        grid=grid,
        in_specs=[tok(D_MODEL), perb(N_MOD, D_MODEL), tab, tab, perb(HIST_ROWS, POOL_WIDTH),
                  full(w["g_pre_mix"]), full(w["g_q_a"]), full(w["g_kv_a"]), full(w["pool_scale"]),
                  full(w["w_in"]), full(w["w_q"]), full(w["w_k"]), full(w["w_vt"]), full(w["w_pool"])],
        out_specs=[tok(HEAD_SLABS), tok(HEAD_SLABS),
                   pl.BlockSpec((None, HEAD_SLABS, tm), lambda b, t: (b, 0, t)),
                   tok(KV_LORA), tok(LANES), tok(POOL_WIDTH), perb(HIST_ROWS, POOL_WIDTH)],
        scratch_shapes=[pltpu.VMEM((HIST_ROWS + tm, POOL_WIDTH), F32)],
        compiler_params=_cparams(2),
        name="premix",
    )(x, mod, cos_t, sin_t, hist, w["g_pre_mix"], w["g_q_a"], w["g_kv_a"], w["pool_scale"],
      w["w_in"], w["w_q"], w["w_k"], w["w_vt"], w["w_pool"])


def _kvproj_kernel(kv_ref, kpe_ref, wk_ref, wvt_ref, k_ref, vt_ref):
    _store_kv(kv_ref[...].astype(BF16), kpe_ref[...], wk_ref, wvt_ref, k_ref, vt_ref)


def _kvproj(kv, kpe_slab, w_k, w_vt, *, tm):
    nb, nt, _ = kv.shape
    tok = lambda width: pl.BlockSpec((None, tm, width), lambda b, t: (b, t, 0))
    full = lambda a: pl.BlockSpec(a.shape, lambda b, t: (0,) * a.ndim)
    return pl.pallas_call(
        _kvproj_kernel,
        out_shape=[jax.ShapeDtypeStruct((nb, nt, HEAD_SLABS), BF16),
                   jax.ShapeDtypeStruct((nb, HEAD_SLABS, nt), BF16)],
        grid=(nb, nt // tm),
        in_specs=[tok(KV_LORA), tok(LANES), full(w_k), full(w_vt)],
        out_specs=[tok(HEAD_SLABS), pl.BlockSpec((None, HEAD_SLABS, tm), lambda b, t: (b, 0, t))],
        compiler_params=_cparams(2),
        name="kvproj",
    )(kv, kpe_slab, w_k, w_vt)


def _attn_kernel(q_ref, k_ref, vt_ref, o_ref, m_ref, acc_ref, *, tq, tk, nk, causal, kv_len):
    qi = pl.program_id(1)
    ki = pl.program_id(2)
    last = ((qi + 1) * tq - 1) // tk if causal else nk - 1
    need_len_mask = kv_len < nk * tk

    @pl.when(ki == 0)
    def _():
        m_ref[...] = jnp.full(m_ref.shape, NEG_INF, F32)
        acc_ref[...] = jnp.zeros(acc_ref.shape, F32)

    def step(masked, nkeys=tk):
        if masked:
            kpos = ki * tk + lax.broadcasted_iota(jnp.int32, (nkeys, tq), 0)
            vis = None
            if causal:
                qpos = qi * tq + lax.broadcasted_iota(jnp.int32, (nkeys, tq), 1)
                vis = (kpos // CHUNK) <= (qpos // CHUNK)
            if need_len_mask:
                lm = kpos < kv_len
                vis = lm if vis is None else (vis & lm)

        def scores(hd):
            sl = slice(hd * LANES, (hd + 1) * LANES)
            return lax.dot_general(k_ref[:nkeys, sl], q_ref[:, sl], (((1,), (1,)), ((), ())),
                                   preferred_element_type=F32)

        def accumulate(hd, alpha, p):
            sl = slice(hd * LANES, (hd + 1) * LANES)
            acc_ref[hd] = alpha * acc_ref[hd] + jnp.dot(vt_ref[sl, :nkeys], p, preferred_element_type=F32)

        s_next = scores(0)
        pending = None
        for hd in range(N_HEADS):
            s = s_next
            if hd + 1 < N_HEADS:
                s_next = scores(hd + 1)
            if pending is not None:
                accumulate(*pending)
            if masked:
                s = jnp.where(vis, s, NEG_INF)
            m_prev = m_ref[hd:hd + 1, :]
            m_new = jnp.maximum(m_prev, jnp.max(s, axis=0, keepdims=True))
            m_ref[hd:hd + 1, :] = m_new
            pending = (hd, jnp.exp2(m_prev - m_new), jnp.exp2(s - m_new).astype(BF16))
        accumulate(*pending)

    if causal and tk == 2 * tq and not need_len_mask:
        @pl.when(ki < last)
        def _():
            step(False)

        @pl.when((ki == last) & (qi % 2 == 0))
        def _():
            step(True, tq)

        @pl.when((ki == last) & (qi % 2 == 1))
        def _():
            step(True)
    elif causal or need_len_mask:
        @pl.when(ki < last)
        def _():
            step(False)

        @pl.when(ki == last)
        def _():
            step(True)
    else:
        step(False)

    @pl.when(ki == nk - 1)
    def _():
        for hd in range(N_HEADS):
            acc = acc_ref[hd]
            out_t = acc / acc[ONE_LANE:ONE_LANE + 1, :]
            o_ref[:, hd * LANES:(hd + 1) * LANES] = out_t.T.astype(BF16)


def _attention(q, k, vt, *, tq, tk, causal, kv_len):
    nb, nq_tot, _ = q.shape
    nk = k.shape[1] // tk
    nq = nq_tot // tq
    if causal:
        last = lambda i: ((i + 1) * tq - 1) // tk
        kmap = lambda b, i, j: (b, jnp.minimum(j, last(i)), 0)
        vmap = lambda b, i, j: (b, 0, jnp.minimum(j, last(i)))
    else:
        kmap = lambda b, i, j: (b, j, 0)
        vmap = lambda b, i, j: (b, 0, j)
    return pl.pallas_call(
        functools.partial(_attn_kernel, tq=tq, tk=tk, nk=nk, causal=causal, kv_len=kv_len),
        out_shape=jax.ShapeDtypeStruct((nb, nq_tot, HEAD_SLABS), BF16),
        grid=(nb, nq, nk),
        in_specs=[pl.BlockSpec((None, tq, HEAD_SLABS), lambda b, i, j: (b, i, 0)),
                  pl.BlockSpec((None, tk, HEAD_SLABS), kmap),
                  pl.BlockSpec((None, HEAD_SLABS, tk), vmap)],
        out_specs=pl.BlockSpec((None, tq, HEAD_SLABS), lambda b, i, j: (b, i, 0)),
        scratch_shapes=[pltpu.VMEM((N_HEADS, tq), F32), pltpu.VMEM((N_HEADS, LANES, tq), F32)],
        compiler_params=_cparams(3),
        name="attention",
    )(q, k, vt)


def _postmix_kernel(*refs, tm, cap, first_call):
    (attn_ref, pooled_ref, x_ref, mod_ref, cnt0_ref, gpost_ref, gffn_ref, woa_ref, wop_ref, wr_ref, br_ref) = refs[:11]
    rest = refs[11:] if first_call else refs[12:]
    x1_ref, ri_ref, rg_ref, cnt_ref, xs_ref, carry, ltri, hbuf, dvm, dsm, dsem, csem = rest
    step = pl.program_id(0) * pl.num_programs(1) + pl.program_id(1)
    n_steps = pl.num_programs(0) * pl.num_programs(1)
    slot = step % 2

    @pl.when(step == 0)
    def _():
        carry[...] = cnt0_ref[...]
        r = lax.broadcasted_iota(jnp.int32, (tm, tm), 0)
        c = lax.broadcasted_iota(jnp.int32, (tm, tm), 1)
        ltri[...] = (r > c).astype(BF16)

    mix = (jnp.dot(attn_ref[...], woa_ref[...], preferred_element_type=F32)
           + jnp.dot(pooled_ref[...], wop_ref[...], preferred_element_type=F32))
    x1 = x_ref[...] + mod_ref[2:3, :] * _rms(mix, gpost_ref[...])
    x1_ref[...] = x1
    h2 = _rms(x1, gffn_ref[...]) * (1.0 + mod_ref[4:5, :]) + mod_ref[3:4, :]

    lane = lax.broadcasted_iota(jnp.int32, (tm, LANES), 1).astype(F32)
    logits = jnp.dot(h2.astype(BF16), wr_ref[...], preferred_element_type=F32) + br_ref[...]
    logits = jnp.where(lane < N_EXPERTS, logits, NEG_INF)
    sel = jnp.zeros((tm, LANES), F32)
    ids, vals, hots = [], [], []
    for _ in range(TOP_K):
        mk = jnp.max(logits, axis=1, keepdims=True)
        idx = jnp.min(jnp.where(logits == mk, lane, float(LANES)), axis=1, keepdims=True)
        hot = lane == idx
        logits = jnp.where(hot, NEG_INF, logits)
        sel = sel + hot.astype(F32)
        ids.append(idx)
        vals.append(mk)
        hots.append(hot)
    ex = [jnp.exp(vk - vals[0]) for vk in vals]
    denom = ex[0] + ex[1] + ex[2] + ex[3]

    before = jnp.dot(ltri[...], sel.astype(BF16), preferred_element_type=F32) + carry[0:1, :]
    ri = jnp.zeros((tm, LANES), F32)
    rg = jnp.zeros((tm, LANES), F32)
    for kk in range(TOP_K):
        rank = jnp.sum(jnp.where(hots[kk], before, 0.0), axis=1, keepdims=True)
        ri = jnp.where(lane == kk, ids[kk] * float(cap) + rank, ri)
        rg = jnp.where(lane == kk, ex[kk] / denom, rg)
    ri_ref[...] = ri.astype(jnp.int32)
    rg_ref[...] = rg
    carry[0:1, :] = carry[0:1, :] + jnp.sum(sel, axis=0, keepdims=True)
    cnt_ref[...] = carry[...]

    def drain(buf):
        for _ in range(TOP_K):
            pltpu.make_async_copy(xs_ref.at[pl.ds(0, tm)], xs_ref.at[pl.ds(0, tm)], dsem.at[buf]).wait()

    @pl.when(step >= 2)
    def _():
        drain(slot)

    hbuf[slot] = h2.reshape(tm, *ROW_TILE)
    dvm[...] = ri.T[0:SUBLANES, :].astype(jnp.int32)
    to_smem = pltpu.make_async_copy(dvm, dsm, csem)
    to_smem.start()
    to_smem.wait()

    def issue(j, c):
        for s in range(SUBLANES):
            r = j * SUBLANES + s
            for kk in range(TOP_K):
                pltpu.async_copy(hbuf.at[slot, r], xs_ref.at[dsm[kk, r]], dsem.at[slot], priority=1)
        return c

    lax.fori_loop(0, tm // SUBLANES, issue, 0)

    @pl.when(step == n_steps - 1)
    def _():
        drain(slot)

        @pl.when(step >= 1)
        def _():
            drain(1 - slot)


def _postmix(attn, pooled, x, mod, cnt0, w, xs, *, tm, cap):
    nb, nt, _ = x.shape
    first_call = xs is None
    tok = lambda width: pl.BlockSpec((None, tm, width), lambda b, t: (b, t, 0))
    full = lambda a: pl.BlockSpec(a.shape, lambda b, t: (0,) * a.ndim)
    outs = [
        jax.ShapeDtypeStruct((nb, nt, D_MODEL), F32),
        jax.ShapeDtypeStruct((nb, nt, LANES), jnp.int32),
        jax.ShapeDtypeStruct((nb, nt, LANES), F32),
        jax.ShapeDtypeStruct((8, LANES), F32),
        jax.ShapeDtypeStruct((N_EXPERTS * cap,) + ROW_TILE, F32),
    ]
    in_specs = [tok(HEAD_SLABS), tok(POOL_WIDTH), tok(D_MODEL),
                pl.BlockSpec((None, N_MOD, D_MODEL), lambda b, t: (b, 0, 0)),
                full(cnt0), full(w["g_post_mix"]), full(w["g_pre_ffn"]),
                full(w["w_o_attn"]), full(w["w_o_pool"]), full(w["w_router"]), full(w["b_router"])]
    args = [attn, pooled, x, mod, cnt0, w["g_post_mix"], w["g_pre_ffn"],
            w["w_o_attn"], w["w_o_pool"], w["w_router"], w["b_router"]]
    if not first_call:
        in_specs.append(pl.BlockSpec(memory_space=pl.ANY))
        args.append(xs)
    return pl.pallas_call(
        functools.partial(_postmix_kernel, tm=tm, cap=cap, first_call=first_call),
        out_shape=outs,
        grid=(nb, nt // tm),
        in_specs=in_specs,
        out_specs=[tok(D_MODEL), tok(LANES), tok(LANES), pl.BlockSpec((8, LANES), lambda b, t: (0, 0)),
                   pl.BlockSpec(memory_space=pl.ANY)],
        scratch_shapes=[pltpu.VMEM((8, LANES), F32), pltpu.VMEM((tm, tm), BF16),
                        pltpu.VMEM((2, tm) + ROW_TILE, F32), pltpu.VMEM((SUBLANES, tm), jnp.int32),
                        pltpu.SMEM((SUBLANES, tm), jnp.int32),
                        pltpu.SemaphoreType.DMA((2,)), pltpu.SemaphoreType.DMA(())],
        input_output_aliases={} if first_call else {11: 4},
        compiler_params=_cparams(2),
        name="postmix",
    )(*args)


def _zero_pads_kernel(zfrom_ref, xs_in_ref, xs_ref, zbuf, zsem):
    del xs_in_ref
    zbuf[...] = jnp.zeros(zbuf.shape, F32)
    for e in range(N_EXPERTS):
        cp = pltpu.make_async_copy(zbuf, xs_ref.at[pl.ds(zfrom_ref[e], MOE_BLOCK)], zsem)
        cp.start()
        cp.wait()


def _zero_pads(zfrom, xs):
    return pl.pallas_call(
        _zero_pads_kernel,
        out_shape=jax.ShapeDtypeStruct(xs.shape, xs.dtype),
        grid_spec=pltpu.PrefetchScalarGridSpec(
            num_scalar_prefetch=1,
            grid=(1,),
            in_specs=[pl.BlockSpec(memory_space=pl.ANY)],
            out_specs=pl.BlockSpec(memory_space=pl.ANY),
            scratch_shapes=[pltpu.VMEM((MOE_BLOCK,) + ROW_TILE, F32), pltpu.SemaphoreType.DMA(())],
        ),
        input_output_aliases={1: 0},
        compiler_params=_cparams(1),
        name="zero_pads",
    )(zfrom, xs)


def _expert_kernel(be_ref, bi_ref, nu_ref, xs_ref, wgu_ref, bgu_ref, wd_ref, bd_ref, ys_ref, wgu_b, wd_b):
    del bi_ref
    i = pl.program_id(0)

    @pl.when(i < nu_ref[0])
    def _():
        @pl.when((i == 0) | (be_ref[i] != be_ref[jnp.maximum(i - 1, 0)]))
        def _():
            wgu_b[...] = wgu_ref[...].astype(BF16)
            wd_b[...] = wd_ref[...].astype(BF16)

        x = xs_ref[...].reshape(MOE_BLOCK, D_MODEL).astype(BF16)
        gu = jnp.dot(x, wgu_b[...], preferred_element_type=F32) + bgu_ref[...]
        g = jnp.minimum(gu[:, :D_FF], SWIGLU_LIMIT)
        u = jnp.clip(gu[:, D_FF:], -SWIGLU_LIMIT, SWIGLU_LIMIT)
        a = (u + 1.0) * (g * jax.nn.sigmoid(SWIGLU_ALPHA * g))
        y = jnp.dot(a.astype(BF16), wd_b[...], preferred_element_type=F32) + bd_ref[...]
        ys_ref[...] = y.reshape(MOE_BLOCK, *ROW_TILE)


def _experts(block_e, block_idx, n_used, xs, w):
    nblk = block_e.shape[0]
    row = lambda i, be, bi, nu: (bi[i], 0, 0)
    per_e = lambda i, be, bi, nu: (be[i], 0, 0)
    return pl.pallas_call(
        _expert_kernel,
        out_shape=jax.ShapeDtypeStruct(xs.shape, F32),
        grid_spec=pltpu.PrefetchScalarGridSpec(
            num_scalar_prefetch=3,
            grid=(nblk,),
            in_specs=[pl.BlockSpec((MOE_BLOCK,) + ROW_TILE, row),
                      pl.BlockSpec((None, D_MODEL, 2 * D_FF), per_e),
                      pl.BlockSpec((None, 1, 2 * D_FF), per_e),
                      pl.BlockSpec((None, D_FF, D_MODEL), per_e),
                      pl.BlockSpec((None, 1, D_MODEL), per_e)],
            out_specs=pl.BlockSpec((MOE_BLOCK,) + ROW_TILE, row),
            scratch_shapes=[pltpu.VMEM((D_MODEL, 2 * D_FF), BF16), pltpu.VMEM((D_FF, D_MODEL), BF16)],
        ),
        compiler_params=_cparams(1),
        name="experts",
    )(block_e, block_idx, n_used, xs, w["w_gu"], w["b_gu"], w["w_down"], w["b_down"])


def _combine_kernel(dcur_ref, dnext_ref, x1_ref, rg_ref, mod_ref, gpost_ref, ys_ref, o_ref, gbuf, sem, *, tm, n):
    i = pl.program_id(0)
    slot = i % 2

    def gather(dest_ref, sl):
        def issue(j, carry):
            for s in range(SUBLANES):
                r = j * SUBLANES + s
                for kk in range(TOP_K):
                    d = dest_ref[0, 0, j * (SUBLANES * TOP_K) + s * TOP_K + kk]
                    pltpu.make_async_copy(ys_ref.at[d], gbuf.at[sl, kk, r], sem.at[sl]).start()
            return carry

        lax.fori_loop(0, tm // SUBLANES, issue, 0)

    @pl.when(i == 0)
    def _():
        gather(dcur_ref, 0)

    @pl.when(i + 1 < n)
    def _():
        gather(dnext_ref, 1 - slot)

    for kk in range(TOP_K):
        pltpu.make_async_copy(gbuf.at[1 - slot, kk], gbuf.at[slot, kk], sem.at[slot]).wait()

    rg = rg_ref[...]
    y = rg[:, 0:1] * gbuf[slot, 0].reshape(tm, D_MODEL)
    for kk in range(1, TOP_K):
        y = y + rg[:, kk:kk + 1] * gbuf[slot, kk].reshape(tm, D_MODEL)
    o_ref[...] = x1_ref[...] + mod_ref[5:6, :] * _rms(y, gpost_ref[...])


def _combine(dest, x1, rg, mod, g_post, ys, *, tm):
    nb, nt, _ = x1.shape
    ntile = nt // tm
    n = nb * ntile
    tok = lambda width: pl.BlockSpec((tm, width), lambda i: (i, 0))
    dest = dest.reshape(n, 1, tm * TOP_K)
    dspec = lambda f: pl.BlockSpec((1, 1, tm * TOP_K), lambda i: (f(i), 0, 0), memory_space=pltpu.SMEM)
    out = pl.pallas_call(
        functools.partial(_combine_kernel, tm=tm, n=n),
        out_shape=jax.ShapeDtypeStruct((nb * nt, D_MODEL), F32),
        grid=(n,),
        in_specs=[dspec(lambda i: i), dspec(lambda i: jnp.minimum(i + 1, n - 1)),
                  tok(D_MODEL), tok(LANES),
                  pl.BlockSpec((None, N_MOD, D_MODEL), lambda i: (i // ntile, 0, 0)),
                  pl.BlockSpec(g_post.shape, lambda i: (0, 0)),
                  pl.BlockSpec(memory_space=pl.ANY)],
        out_specs=tok(D_MODEL),
        scratch_shapes=[pltpu.VMEM((2, TOP_K, tm) + ROW_TILE, F32),
                        pltpu.SemaphoreType.DMA((2,))],
        compiler_params=_cparams(1),
        name="combine",
    )(dest, dest, x1.reshape(nb * nt, D_MODEL), rg.reshape(nb * nt, LANES), mod, g_post, ys)
    return out.reshape(x1.shape)


def _rot_swap(w):
    half = QK_ROPE // 2
    return jnp.concatenate([-w[..., half:], w[..., :half]], axis=-1)


def _prep_weights(w_in, g_q_a, w_q_b, g_kv_a, w_uk, w_uv, w_pool, pool_scale, w_o, g_pre_mix, g_post_mix,
                  g_pre_ffn, w_router, b_router, w_gu, b_gu, w_down, b_down, g_post_ffn):
    row = lambda a: a.reshape(1, -1).astype(F32)
    w_kpe = w_in[:, U_OFF:U_OFF + QK_ROPE]
    zeros = lambda *s: jnp.zeros(s, F32)
    d = D_MODEL
    slab = lambda a: jnp.concatenate([zeros(d, QK_NOPE), a, zeros(d, LANES - QK_HEAD)], axis=1)
    w_in_ext = jnp.concatenate([w_in[:, :U_OFF], w_in[:, U_OFF + QK_ROPE:], slab(w_kpe), slab(_rot_swap(w_kpe))],
                               axis=1)
    pad_q = zeros(Q_LORA, N_HEADS, LANES - QK_HEAD)
    wq_plain = jnp.concatenate([w_q_b, pad_q], axis=2).reshape(Q_LORA, HEAD_SLABS)
    wq_swap = jnp.concatenate([zeros(Q_LORA, N_HEADS, QK_NOPE), _rot_swap(w_q_b[..., QK_NOPE:]), pad_q],
                              axis=2).reshape(Q_LORA, HEAD_SLABS)
    pad_kv = zeros(KV_LORA, N_HEADS, LANES - QK_NOPE)
    wk = jnp.concatenate([w_uk, pad_kv], axis=2).reshape(KV_LORA, HEAD_SLABS)
    wv = jnp.concatenate([w_uv, pad_kv], axis=2).reshape(KV_LORA, HEAD_SLABS)
    mla_w = N_HEADS * V_HEAD
    woa = jnp.concatenate([w_o[:mla_w].reshape(N_HEADS, V_HEAD, d), zeros(N_HEADS, LANES - V_HEAD, d)],
                          axis=1).reshape(HEAD_SLABS, d)
    return {
        "g_pre_mix": row(g_pre_mix), "g_q_a": row(g_q_a), "g_kv_a": row(g_kv_a), "pool_scale": row(pool_scale),
        "w_in": w_in_ext.astype(BF16),
        "w_q": jnp.concatenate([wq_plain, wq_swap], axis=1).astype(BF16),
        "w_k": wk.astype(BF16), "w_vt": wv.T.astype(BF16),
        "w_pool": w_pool.astype(BF16),
        "g_post_mix": row(g_post_mix), "g_pre_ffn": row(g_pre_ffn), "g_post_ffn": row(g_post_ffn),
        "w_o_attn": woa.astype(BF16), "w_o_pool": w_o[mla_w:].astype(BF16),
        "w_router": jnp.pad(w_router, ((0, 0), (0, LANES - N_EXPERTS))).astype(BF16),
        "b_router": jnp.pad(b_router, (0, LANES - N_EXPERTS)).reshape(1, LANES).astype(F32),
        "w_gu": w_gu, "b_gu": b_gu.reshape(N_EXPERTS, 1, 2 * D_FF).astype(F32),
        "w_down": w_down, "b_down": b_down.reshape(N_EXPERTS, 1, D_MODEL).astype(F32),
    }


def _rope_tables(pos):
    half = QK_ROPE // 2
    inv = ROPE_THETA ** (-jnp.arange(half, dtype=F32) / half)
    ang = pos.astype(F32)[:, None] * inv[None, :]
    cos, sin = jnp.cos(ang), jnp.sin(ang)
    n = pos.shape[0]
    cos_t = jnp.concatenate([jnp.ones((n, QK_NOPE), F32), cos, cos, jnp.zeros((n, LANES - QK_HEAD), F32)], axis=1)
    sin_t = jnp.concatenate([jnp.zeros((n, QK_NOPE), F32), sin, sin, jnp.zeros((n, LANES - QK_HEAD), F32)], axis=1)
    return cos_t, sin_t


def _tile(n, pref):
    return pref if n % pref == 0 else n


def _mixer_path(x, mod, pos0, hist, cache, w, cnt0, xs, cap):
    nb, nt, _ = x.shape
    tm = _tile(nt, 512)
    cos_t, sin_t = _rope_tables(pos0 + jnp.arange(nt, dtype=jnp.int32))
    q, k, vt, kv_new, kslab, pooled, pool_tail = _premix(x, mod, cos_t, sin_t, hist, w, tm=tm, pos0=pos0)
    if cache is None:
        attn = _attention(q, k, vt, tq=tm, tk=_tile(nt, 2 * tm), causal=True, kv_len=nt)
    else:
        ckv, ckpe = cache
        past = ckv.shape[1]
        kv_len = past + nt
        tk = -(-kv_len // 256) * 256
        ckpe_slab = jnp.pad(ckpe, ((0, 0), (0, 0), (QK_NOPE, LANES - QK_HEAD)))
        lat_all = jnp.concatenate([ckv, kv_new, jnp.zeros((nb, tk - kv_len, KV_LORA), F32)], axis=1)
        kpe_all = jnp.concatenate([ckpe_slab, kslab, jnp.zeros((nb, tk - kv_len, LANES), F32)], axis=1)
        k_all, vt_all = _kvproj(lat_all, kpe_all, w["w_k"], w["w_vt"], tm=_tile(tk, 768))
        tq = -(-nt // LANES) * LANES
        q_pad = jnp.pad(q, ((0, 0), (0, tq - nt), (0, 0)))
        attn = _attention(q_pad, k_all, vt_all, tq=tq, tk=tk, causal=False, kv_len=kv_len)[:, :nt]
    x1, ri, rg, cnt, xs = _postmix(attn, pooled, x, mod, cnt0, w, xs, tm=tm, cap=cap)
    return x1, xs, ri, rg, cnt, kv_new, kslab[..., QK_NOPE:QK_HEAD], pool_tail[:, 1:]


def kernel(x_prompt, x_sample, c_prompt, c_sample, cache_kv_latent, cache_k_rope, state_pool, w_ada, b_ada,
           g_pre_mix, w_in, g_q_a, w_q_b, g_kv_a, w_uk, w_uv, w_pool, pool_scale, w_o, g_post_mix, g_pre_ffn,
           w_router, b_router, w_gu, b_gu, w_down, b_down, g_post_ffn):
    assert w_ada.shape[0] == 1, "single-layer step"
    bp, sp, _ = x_prompt.shape
    bs, ss, _ = x_sample.shape
    past = cache_kv_latent.shape[2]
    w = _prep_weights(w_in[0], g_q_a[0], w_q_b[0], g_kv_a[0], w_uk[0], w_uv[0], w_pool[0], pool_scale[0], w_o[0],
                      g_pre_mix[0], g_post_mix[0], g_pre_ffn[0], w_router[0], b_router[0], w_gu[0], b_gu[0],
                      w_down[0], b_down[0], g_post_ffn[0])

    mod = _ada(jnp.concatenate([c_prompt, c_sample], axis=0), w_ada[0], b_ada[0])
    mod = mod.reshape(bp + bs, N_MOD, D_MODEL)
    mod_p, mod_s = mod[:bp], mod[bp:]

    hist_p = jnp.zeros((bp, HIST_ROWS, POOL_WIDTH), F32)
    hist_s = jnp.pad(state_pool[0], ((0, 0), (1, 0), (0, 0)))
    n_tok = bp * sp + bs * ss
    cap = (n_tok // MOE_BLOCK + 1) * MOE_BLOCK
    cnt0 = jnp.zeros((8, LANES), F32)
    x1p, xs, rip, rgp, cntp, kv_p, kpe_p, pool_p = _mixer_path(x_prompt, mod_p, 0, hist_p, None, w, cnt0, None, cap)
    x1s, xs, ris, rgs, cnts, kv_s, kpe_s, pool_s = _mixer_path(
        x_sample, mod_s, past, hist_s, (cache_kv_latent[0], cache_k_rope[0]), w, cntp, xs, cap)

    counts = cnts[0, :N_EXPERTS].astype(jnp.int32)
    blocks = (counts + MOE_BLOCK - 1) // MOE_BLOCK
    blk_end = jnp.cumsum(blocks)
    n_blocks = -(-(n_tok * TOP_K) // MOE_BLOCK) + N_EXPERTS
    n_used = blk_end[-1:].astype(jnp.int32)
    grid_i = jnp.minimum(jnp.arange(n_blocks, dtype=jnp.int32), n_used[0] - 1)
    block_e = jnp.sum((blk_end[None, :] <= grid_i[:, None]).astype(jnp.int32), axis=1)
    blk_start = blk_end - blocks
    block_idx = (block_e * (cap // MOE_BLOCK) + grid_i
                 - jnp.sum(jnp.where(block_e[:, None] == jnp.arange(N_EXPERTS, dtype=jnp.int32), blk_start, 0), axis=1))
    zfrom = (jnp.arange(N_EXPERTS, dtype=jnp.int32) * cap + counts).astype(jnp.int32)
    xs = _zero_pads(zfrom, xs)
    ys = _experts(block_e, block_idx.astype(jnp.int32), n_used, xs, w)
    dest_p = rip[..., :TOP_K]
    dest_s = ris[..., :TOP_K]
    y_p = _combine(dest_p, x1p, rgp, mod_p, w["g_post_ffn"], ys, tm=_tile(sp, 512))
    y_s = _combine(dest_s, x1s, rgs, mod_s, w["g_post_ffn"], ys, tm=_tile(ss, 256))
    return (y_p, y_s, kv_p[None], kpe_p[None], pool_p[None], kv_s[None], kpe_s[None], pool_s[None])
```

```python
import functools

import jax
import jax.numpy as jnp
from jax import lax
from jax.experimental import pallas as pl
from jax.experimental.pallas import tpu as pltpu

F32 = jnp.float32
BF16 = jnp.bfloat16

D_MODEL = 1024
CHUNK = 64
N_HEADS = 8
QK_NOPE = 64
QK_ROPE = 32
QK_HEAD = QK_NOPE + QK_ROPE
V_HEAD = 64
Q_LORA = 384
KV_LORA = 256
ROPE_THETA = 10000.0
POOL_WINDOWS = (2, 4, 8, 16)
POOL_GROUP_DIM = 128
POOL_WIDTH = POOL_GROUP_DIM * len(POOL_WINDOWS)
POOL_HIST = max(POOL_WINDOWS) - 1
HIST_ROWS = POOL_HIST + 1
N_EXPERTS = 32
TOP_K = 4
D_FF = 1024
SWIGLU_LIMIT = 7.0
SWIGLU_ALPHA = 1.702
N_MOD = 6
EPS = 1e-6

LANES = 128
HEAD_SLABS = N_HEADS * LANES
ONE_LANE = V_HEAD
IN_EXT = Q_LORA + KV_LORA + POOL_WIDTH + 2 * LANES
U_OFF = Q_LORA + KV_LORA
KPE_OFF = U_OFF + POOL_WIDTH
SM_SCALE = QK_HEAD ** -0.5
LOG2_E = 1.4426950408889634
Q_SCALE = SM_SCALE * LOG2_E
NEG_INF = float("-inf")

SUBLANES = 8
ROW_TILE = (SUBLANES, D_MODEL // SUBLANES)
MOE_BLOCK = 512
VMEM_LIMIT = 56 * 1024 * 1024


def _cparams(n_axes, vmem=VMEM_LIMIT):
    return pltpu.CompilerParams(dimension_semantics=("arbitrary",) * n_axes, vmem_limit_bytes=vmem)


def _rms(x, g):
    return x * lax.rsqrt(jnp.mean(x * x, axis=-1, keepdims=True) + EPS) * g


def _ada_kernel(c_ref, w_ref, b_ref, o_ref):
    c = c_ref[...]
    s = (c * jax.nn.sigmoid(c)).astype(BF16)
    o_ref[...] = jnp.dot(s, w_ref[...].astype(BF16), preferred_element_type=F32) + b_ref[...]


def _ada(c, w_ada, b_ada):
    nb = c.shape[0]
    return pl.pallas_call(
        _ada_kernel,
        out_shape=jax.ShapeDtypeStruct((nb, N_MOD * D_MODEL), F32),
        grid=(N_MOD,),
        in_specs=[pl.BlockSpec((nb, D_MODEL), lambda j: (0, 0)),
                  pl.BlockSpec((D_MODEL, D_MODEL), lambda j: (0, j)),
                  pl.BlockSpec((1, D_MODEL), lambda j: (0, j))],
        out_specs=pl.BlockSpec((nb, D_MODEL), lambda j: (0, j)),
        compiler_params=_cparams(1),
        name="ada",
    )(c, w_ada, b_ada.reshape(1, -1))


def _store_kv(kvb, kslab, wk_ref, wvt_ref, k_ref, vt_ref):
    kk = jnp.dot(kvb, wk_ref[...], preferred_element_type=F32)
    for hd in range(N_HEADS):
        sl = slice(hd * LANES, (hd + 1) * LANES)
        k_ref[:, sl] = (kk[:, sl] + kslab).astype(BF16)
    vt = lax.dot_general(wvt_ref[...], kvb, (((1,), (1,)), ((), ())), preferred_element_type=F32)
    row = lax.broadcasted_iota(jnp.int32, (HEAD_SLABS, 1), 0)
    vt_ref[...] = (vt + (row % LANES == ONE_LANE).astype(F32)).astype(BF16)


def _premix_kernel(x_ref, mod_ref, cos_ref, sin_ref, hist_ref, gpre_ref, gqa_ref, gkv_ref, pscale_ref,
                   win_ref, wq_ref, wk_ref, wvt_ref, wpool_ref,
                   q_ref, k_ref, vt_ref, kv_ref, kpe_ref, pooled_ref, poolnew_ref,
                   ubuf, *, tm, pos0):
    t = pl.program_id(1)
    x = x_ref[...]
    h = (_rms(x, gpre_ref[...]) * (1.0 + mod_ref[1:2, :]) + mod_ref[0:1, :]).astype(BF16)
    z = jnp.dot(h, win_ref[...], preferred_element_type=F32)
    cosv = cos_ref[...]
    sinv = sin_ref[...]

    qan = _rms(z[:, :Q_LORA], gqa_ref[...]).astype(BF16)
    qq = jnp.dot(qan, wq_ref[...], preferred_element_type=F32)
    for hd in range(N_HEADS):
        a = qq[:, hd * LANES:(hd + 1) * LANES]
        b = qq[:, HEAD_SLABS + hd * LANES:HEAD_SLABS + (hd + 1) * LANES]
        q_ref[:, hd * LANES:(hd + 1) * LANES] = ((a * cosv + b * sinv) * Q_SCALE).astype(BF16)

    kvn = _rms(z[:, Q_LORA:U_OFF], gkv_ref[...])
    kv_ref[...] = kvn
    kslab = z[:, KPE_OFF:KPE_OFF + LANES] * cosv + z[:, KPE_OFF + LANES:KPE_OFF + 2 * LANES] * sinv
    kpe_ref[...] = kslab
    _store_kv(kvn.astype(BF16), kslab, wk_ref, wvt_ref, k_ref, vt_ref)

    @pl.when(t == 0)
    def _():
        ubuf[0:HIST_ROWS, :] = hist_ref[...]

    @pl.when(t > 0)
    def _():
        ubuf[0:HIST_ROWS, :] = ubuf[tm:tm + HIST_ROWS, :]

    ubuf[HIST_ROWS:HIST_ROWS + tm, :] = z[:, U_OFF:KPE_OFF]
    pos = pos0 + t * tm + lax.broadcasted_iota(jnp.int32, (tm, 1), 0)
    for g, w in enumerate(POOL_WINDOWS):
        sl = slice(g * POOL_GROUP_DIM, (g + 1) * POOL_GROUP_DIM)
        u = ubuf[HIST_ROWS:HIST_ROWS + tm, sl]
        acc = u
        for j in range(1, w):
            acc = acc + ubuf[HIST_ROWS - j:HIST_ROWS - j + tm, sl]
        cnt = jnp.minimum(pos + 1, w).astype(F32)
        d = (acc / cnt - u).astype(BF16)
        y = jnp.dot(d, wpool_ref[g], preferred_element_type=F32) * pscale_ref[:, sl]
        pooled_ref[:, sl] = y.astype(BF16)
    poolnew_ref[...] = ubuf[tm:tm + HIST_ROWS, :]


def _premix(x, mod, cos_t, sin_t, hist, w, *, tm, pos0):
    nb, nt, _ = x.shape
    grid = (nb, nt // tm)
    tok = lambda width: pl.BlockSpec((None, tm, width), lambda b, t: (b, t, 0))
    full = lambda a: pl.BlockSpec(a.shape, lambda b, t: (0,) * a.ndim)
    tab = pl.BlockSpec((tm, LANES), lambda b, t: (t, 0))
    perb = lambda rows, width: pl.BlockSpec((None, rows, width), lambda b, t: (b, 0, 0))
    outs = [
        jax.ShapeDtypeStruct((nb, nt, HEAD_SLABS), BF16),
        jax.ShapeDtypeStruct((nb, nt, HEAD_SLABS), BF16),
        jax.ShapeDtypeStruct((nb, HEAD_SLABS, nt), BF16),
        jax.ShapeDtypeStruct((nb, nt, KV_LORA), F32),
        jax.ShapeDtypeStruct((nb, nt, LANES), F32),
        jax.ShapeDtypeStruct((nb, nt, POOL_WIDTH), BF16),
        jax.ShapeDtypeStruct((nb, HIST_ROWS, POOL_WIDTH), F32),
    ]
    return pl.pallas_call(
        functools.partial(_premix_kernel, tm=tm, pos0=pos0),
        out_shape=outs,
        grid=grid,
        in_specs=[tok(D_MODEL), perb(N_MOD, D_MODEL), tab, tab, perb(HIST_ROWS, POOL_WIDTH),
                  full(w["g_pre_mix"]), full(w["g_q_a"]), full(w["g_kv_a"]), full(w["pool_scale"]),
                  full(w["w_in"]), full(w["w_q"]), full(w["w_k"]), full(w["w_vt"]), full(w["w_pool"])],
        out_specs=[tok(HEAD_SLABS), tok(HEAD_SLABS),
                   pl.BlockSpec((None, HEAD_SLABS, tm), lambda b, t: (b, 0, t)),
                   tok(KV_LORA), tok(LANES), tok(POOL_WIDTH), perb(HIST_ROWS, POOL_WIDTH)],
        scratch_shapes=[pltpu.VMEM((HIST_ROWS + tm, POOL_WIDTH), F32)],
        compiler_params=_cparams(2),
        name="premix",
    )(x, mod, cos_t, sin_t, hist, w["g_pre_mix"], w["g_q_a"], w["g_kv_a"], w["pool_scale"],
      w["w_in"], w["w_q"], w["w_k"], w["w_vt"], w["w_pool"])


def _kvproj_kernel(kv_ref, kpe_ref, wk_ref, wvt_ref, k_ref, vt_ref):
    _store_kv(kv_ref[...].astype(BF16), kpe_ref[...], wk_ref, wvt_ref, k_ref, vt_ref)


def _kvproj(kv, kpe_slab, w_k, w_vt, *, tm):
    nb, nt, _ = kv.shape
    tok = lambda width: pl.BlockSpec((None, tm, width), lambda b, t: (b, t, 0))
    full = lambda a: pl.BlockSpec(a.shape, lambda b, t: (0,) * a.ndim)
    return pl.pallas_call(
        _kvproj_kernel,
        out_shape=[jax.ShapeDtypeStruct((nb, nt, HEAD_SLABS), BF16),
                   jax.ShapeDtypeStruct((nb, HEAD_SLABS, nt), BF16)],
        grid=(nb, nt // tm),
        in_specs=[tok(KV_LORA), tok(LANES), full(w_k), full(w_vt)],
        out_specs=[tok(HEAD_SLABS), pl.BlockSpec((None, HEAD_SLABS, tm), lambda b, t: (b, 0, t))],
        compiler_params=_cparams(2),
        name="kvproj",
    )(kv, kpe_slab, w_k, w_vt)


def _attn_kernel(q_ref, k_ref, vt_ref, o_ref, m_ref, acc_ref, *, tq, tk, nk, causal, kv_len):
    qi = pl.program_id(1)
    ki = pl.program_id(2)
    last = ((qi + 1) * tq - 1) // tk if causal else nk - 1
    need_len_mask = kv_len < nk * tk

    @pl.when(ki == 0)
    def _():
        m_ref[...] = jnp.full(m_ref.shape, NEG_INF, F32)
        acc_ref[...] = jnp.zeros(acc_ref.shape, F32)

    def step(masked, nkeys=tk):
        if masked:
            kpos = ki * tk + lax.broadcasted_iota(jnp.int32, (nkeys, tq), 0)
            vis = None
            if causal:
                qpos = qi * tq + lax.broadcasted_iota(jnp.int32, (nkeys, tq), 1)
                vis = (kpos // CHUNK) <= (qpos // CHUNK)
            if need_len_mask:
                lm = kpos < kv_len
                vis = lm if vis is None else (vis & lm)

        def scores(hd):
            sl = slice(hd * LANES, (hd + 1) * LANES)
            return lax.dot_general(k_ref[:nkeys, sl], q_ref[:, sl], (((1,), (1,)), ((), ())),
                                   preferred_element_type=F32)

        def accumulate(hd, alpha, p):
            sl = slice(hd * LANES, (hd + 1) * LANES)
            acc_ref[hd] = alpha * acc_ref[hd] + jnp.dot(vt_ref[sl, :nkeys], p, preferred_element_type=F32)

        s_next = scores(0)
        pending = None
        for hd in range(N_HEADS):
            s = s_next
            if hd + 1 < N_HEADS:
                s_next = scores(hd + 1)
            if pending is not None:
                accumulate(*pending)
            if masked:
                s = jnp.where(vis, s, NEG_INF)
            m_prev = m_ref[hd:hd + 1, :]
            m_new = jnp.maximum(m_prev, jnp.max(s, axis=0, keepdims=True))
            m_ref[hd:hd + 1, :] = m_new
            pending = (hd, jnp.exp2(m_prev - m_new), jnp.exp2(s - m_new).astype(BF16))
        accumulate(*pending)

    if causal and tk == 2 * tq and not need_len_mask:
        @pl.when(ki < last)
        def _():
            step(False)

        @pl.when((ki == last) & (qi % 2 == 0))
        def _():
            step(True, tq)

        @pl.when((ki == last) & (qi % 2 == 1))
        def _():
            step(True)
    elif causal or need_len_mask:
        @pl.when(ki < last)
        def _():
            step(False)

        @pl.when(ki == last)
        def _():
            step(True)
    else:
        step(False)

    @pl.when(ki == nk - 1)
    def _():
        for hd in range(N_HEADS):
            acc = acc_ref[hd]
            out_t = acc / acc[ONE_LANE:ONE_LANE + 1, :]
            o_ref[:, hd * LANES:(hd + 1) * LANES] = out_t.T.astype(BF16)


def _attention(q, k, vt, *, tq, tk, causal, kv_len):
    nb, nq_tot, _ = q.shape
    nk = k.shape[1] // tk
    nq = nq_tot // tq
    if causal:
        last = lambda i: ((i + 1) * tq - 1) // tk
        kmap = lambda b, i, j: (b, jnp.minimum(j, last(i)), 0)
        vmap = lambda b, i, j: (b, 0, jnp.minimum(j, last(i)))
    else:
        kmap = lambda b, i, j: (b, j, 0)
        vmap = lambda b, i, j: (b, 0, j)
    return pl.pallas_call(
        functools.partial(_attn_kernel, tq=tq, tk=tk, nk=nk, causal=causal, kv_len=kv_len),
        out_shape=jax.ShapeDtypeStruct((nb, nq_tot, HEAD_SLABS), BF16),
        grid=(nb, nq, nk),
        in_specs=[pl.BlockSpec((None, tq, HEAD_SLABS), lambda b, i, j: (b, i, 0)),
                  pl.BlockSpec((None, tk, HEAD_SLABS), kmap),
                  pl.BlockSpec((None, HEAD_SLABS, tk), vmap)],
        out_specs=pl.BlockSpec((None, tq, HEAD_SLABS), lambda b, i, j: (b, i, 0)),
        scratch_shapes=[pltpu.VMEM((N_HEADS, tq), F32), pltpu.VMEM((N_HEADS, LANES, tq), F32)],
        compiler_params=_cparams(3),
        name="attention",
    )(q, k, vt)


def _postmix_kernel(*refs, tm, cap, first_call):
    (attn_ref, pooled_ref, x_ref, mod_ref, cnt0_ref, gpost_ref, gffn_ref, woa_ref, wop_ref, wr_ref, br_ref) = refs[:11]
    rest = refs[11:] if first_call else refs[12:]
    x1_ref, ri_ref, rg_ref, cnt_ref, xs_ref, carry, ltri, hbuf, dvm, dsm, dsem, csem = rest
    step = pl.program_id(0) * pl.num_programs(1) + pl.program_id(1)
    n_steps = pl.num_programs(0) * pl.num_programs(1)
    slot = step % 2

    @pl.when(step == 0)
    def _():
        carry[...] = cnt0_ref[...]
        r = lax.broadcasted_iota(jnp.int32, (tm, tm), 0)
        c = lax.broadcasted_iota(jnp.int32, (tm, tm), 1)
        ltri[...] = (r > c).astype(BF16)

    n_groups = tm // SUBLANES
    n_chunks = min(8, n_groups)
    per_chunk = n_groups // n_chunks

    def issue_rows(buf, j0, j1):
        def issue(j, c):
            for s in range(SUBLANES):
                r = j * SUBLANES + s
                for kk in range(TOP_K):
                    pltpu.async_copy(hbuf.at[buf, r], xs_ref.at[dsm[kk, r]], dsem.at[buf], priority=1)
            return c

        lax.fori_loop(j0, j1, issue, 0)

    chunks = iter(range(n_chunks))

    def issue_prev_chunk():
        c = next(chunks, None)
        if c is not None:
            @pl.when(step >= 1)
            def _():
                issue_rows(1 - slot, c * per_chunk, (c + 1) * per_chunk)

    issue_prev_chunk()
    mix = (jnp.dot(attn_ref[...], woa_ref[...], preferred_element_type=F32)
           + jnp.dot(pooled_ref[...], wop_ref[...], preferred_element_type=F32))
    issue_prev_chunk()
    x1 = x_ref[...] + mod_ref[2:3, :] * _rms(mix, gpost_ref[...])
    x1_ref[...] = x1
    issue_prev_chunk()
    h2 = _rms(x1, gffn_ref[...]) * (1.0 + mod_ref[4:5, :]) + mod_ref[3:4, :]

    lane = lax.broadcasted_iota(jnp.int32, (tm, LANES), 1).astype(F32)
    logits = jnp.dot(h2.astype(BF16), wr_ref[...], preferred_element_type=F32) + br_ref[...]
    logits = jnp.where(lane < N_EXPERTS, logits, NEG_INF)
    issue_prev_chunk()
    sel = jnp.zeros((tm, LANES), F32)
    ids, vals, hots = [], [], []
    for _ in range(TOP_K):
        mk = jnp.max(logits, axis=1, keepdims=True)
        idx = jnp.min(jnp.where(logits == mk, lane, float(LANES)), axis=1, keepdims=True)
        hot = lane == idx
        logits = jnp.where(hot, NEG_INF, logits)
        sel = sel + hot.astype(F32)
        ids.append(idx)
        vals.append(mk)
        hots.append(hot)
        issue_prev_chunk()
    assert next(chunks, None) is None
    ex = [jnp.exp(vk - vals[0]) for vk in vals]
    denom = ex[0] + ex[1] + ex[2] + ex[3]

    before = jnp.dot(ltri[...], sel.astype(BF16), preferred_element_type=F32) + carry[0:1, :]
    ri = jnp.zeros((tm, LANES), F32)
    rg = jnp.zeros((tm, LANES), F32)
    for kk in range(TOP_K):
        rank = jnp.sum(jnp.where(hots[kk], before, 0.0), axis=1, keepdims=True)
        ri = jnp.where(lane == kk, ids[kk] * float(cap) + rank, ri)
        rg = jnp.where(lane == kk, ex[kk] / denom, rg)
    ri_ref[...] = ri.astype(jnp.int32)
    rg_ref[...] = rg
    carry[0:1, :] = carry[0:1, :] + jnp.sum(sel, axis=0, keepdims=True)
    cnt_ref[...] = carry[...]

    def drain(buf):
        for _ in range(TOP_K):
            pltpu.make_async_copy(xs_ref.at[pl.ds(0, tm)], xs_ref.at[pl.ds(0, tm)], dsem.at[buf]).wait()

    @pl.when(step >= 2)
    def _():
        drain(slot)

    hbuf[slot] = h2.reshape(tm, *ROW_TILE)
    dvm[...] = ri.T[0:SUBLANES, :].astype(jnp.int32)
    to_smem = pltpu.make_async_copy(dvm, dsm, csem)
    to_smem.start()
    to_smem.wait()

    @pl.when(step == n_steps - 1)
    def _():
        issue_rows(slot, 0, n_groups)
        drain(slot)

        @pl.when(step >= 1)
        def _():
            drain(1 - slot)


def _postmix(attn, pooled, x, mod, cnt0, w, xs, *, tm, cap):
    nb, nt, _ = x.shape
    first_call = xs is None
    tok = lambda width: pl.BlockSpec((None, tm, width), lambda b, t: (b, t, 0))
    full = lambda a: pl.BlockSpec(a.shape, lambda b, t: (0,) * a.ndim)
    outs = [
        jax.ShapeDtypeStruct((nb, nt, D_MODEL), F32),
        jax.ShapeDtypeStruct((nb, nt, LANES), jnp.int32),
        jax.ShapeDtypeStruct((nb, nt, LANES), F32),
        jax.ShapeDtypeStruct((8, LANES), F32),
        jax.ShapeDtypeStruct((N_EXPERTS * cap,) + ROW_TILE, F32),
    ]
    in_specs = [tok(HEAD_SLABS), tok(POOL_WIDTH), tok(D_MODEL),
                pl.BlockSpec((None, N_MOD, D_MODEL), lambda b, t: (b, 0, 0)),
                full(cnt0), full(w["g_post_mix"]), full(w["g_pre_ffn"]),
                full(w["w_o_attn"]), full(w["w_o_pool"]), full(w["w_router"]), full(w["b_router"])]
    args = [attn, pooled, x, mod, cnt0, w["g_post_mix"], w["g_pre_ffn"],
            w["w_o_attn"], w["w_o_pool"], w["w_router"], w["b_router"]]
    if not first_call:
        in_specs.append(pl.BlockSpec(memory_space=pl.ANY))
        args.append(xs)
    return pl.pallas_call(
        functools.partial(_postmix_kernel, tm=tm, cap=cap, first_call=first_call),
        out_shape=outs,
        grid=(nb, nt // tm),
        in_specs=in_specs,
        out_specs=[tok(D_MODEL), tok(LANES), tok(LANES), pl.BlockSpec((8, LANES), lambda b, t: (0, 0)),
                   pl.BlockSpec(memory_space=pl.ANY)],
        scratch_shapes=[pltpu.VMEM((8, LANES), F32), pltpu.VMEM((tm, tm), BF16),
                        pltpu.VMEM((2, tm) + ROW_TILE, F32), pltpu.VMEM((SUBLANES, tm), jnp.int32),
                        pltpu.SMEM((SUBLANES, tm), jnp.int32),
                        pltpu.SemaphoreType.DMA((2,)), pltpu.SemaphoreType.DMA(())],
        input_output_aliases={} if first_call else {11: 4},
        compiler_params=_cparams(2),
        name="postmix",
    )(*args)


def _zero_pads_kernel(zfrom_ref, xs_in_ref, xs_ref, zbuf, zsem):
    del xs_in_ref
    zbuf[...] = jnp.zeros(zbuf.shape, F32)
    for e in range(N_EXPERTS):
        cp = pltpu.make_async_copy(zbuf, xs_ref.at[pl.ds(zfrom_ref[e], MOE_BLOCK)], zsem)
        cp.start()
        cp.wait()


def _zero_pads(zfrom, xs):
    return pl.pallas_call(
        _zero_pads_kernel,
        out_shape=jax.ShapeDtypeStruct(xs.shape, xs.dtype),
        grid_spec=pltpu.PrefetchScalarGridSpec(
            num_scalar_prefetch=1,
            grid=(1,),
            in_specs=[pl.BlockSpec(memory_space=pl.ANY)],
            out_specs=pl.BlockSpec(memory_space=pl.ANY),
            scratch_shapes=[pltpu.VMEM((MOE_BLOCK,) + ROW_TILE, F32), pltpu.SemaphoreType.DMA(())],
        ),
        input_output_aliases={1: 0},
        compiler_params=_cparams(1),
        name="zero_pads",
    )(zfrom, xs)


def _expert_kernel(be_ref, bi_ref, nu_ref, xs_ref, wgu_ref, bgu_ref, wd_ref, bd_ref, ys_ref, wgu_b, wd_b):
    del bi_ref
    i = pl.program_id(0)

    @pl.when(i < nu_ref[0])
    def _():
        @pl.when((i == 0) | (be_ref[i] != be_ref[jnp.maximum(i - 1, 0)]))
        def _():
            wgu_b[...] = wgu_ref[...].astype(BF16)
            wd_b[...] = wd_ref[...].astype(BF16)

        x = xs_ref[...].reshape(MOE_BLOCK, D_MODEL).astype(BF16)
        gu = jnp.dot(x, wgu_b[...], preferred_element_type=F32) + bgu_ref[...]
        g = jnp.minimum(gu[:, :D_FF], SWIGLU_LIMIT)
        u = jnp.clip(gu[:, D_FF:], -SWIGLU_LIMIT, SWIGLU_LIMIT)
        a = (u + 1.0) * (g * jax.nn.sigmoid(SWIGLU_ALPHA * g))
        y = jnp.dot(a.astype(BF16), wd_b[...], preferred_element_type=F32) + bd_ref[...]
        ys_ref[...] = y.reshape(MOE_BLOCK, *ROW_TILE)


def _experts(block_e, block_idx, n_used, xs, w):
    nblk = block_e.shape[0]
    row = lambda i, be, bi, nu: (bi[i], 0, 0)
    per_e = lambda i, be, bi, nu: (be[i], 0, 0)
    return pl.pallas_call(
        _expert_kernel,
        out_shape=jax.ShapeDtypeStruct(xs.shape, F32),
        grid_spec=pltpu.PrefetchScalarGridSpec(
            num_scalar_prefetch=3,
            grid=(nblk,),
            in_specs=[pl.BlockSpec((MOE_BLOCK,) + ROW_TILE, row),
                      pl.BlockSpec((None, D_MODEL, 2 * D_FF), per_e),
                      pl.BlockSpec((None, 1, 2 * D_FF), per_e),
                      pl.BlockSpec((None, D_FF, D_MODEL), per_e),
                      pl.BlockSpec((None, 1, D_MODEL), per_e)],
            out_specs=pl.BlockSpec((MOE_BLOCK,) + ROW_TILE, row),
            scratch_shapes=[pltpu.VMEM((D_MODEL, 2 * D_FF), BF16), pltpu.VMEM((D_FF, D_MODEL), BF16)],
        ),
        compiler_params=_cparams(1),
        name="experts",
    )(block_e, block_idx, n_used, xs, w["w_gu"], w["b_gu"], w["w_down"], w["b_down"])


def _combine_kernel(dcur_ref, dnext_ref, x1_ref, rg_ref, mod_ref, gpost_ref, ys_ref, o_ref, gbuf, sem, *, tm, n):
    i = pl.program_id(0)
    slot = i % 2

    def gather(dest_ref, sl):
        def issue(j, carry):
            for s in range(SUBLANES):
                r = j * SUBLANES + s
                for kk in range(TOP_K):
                    d = dest_ref[0, 0, j * (SUBLANES * TOP_K) + s * TOP_K + kk]
                    pltpu.make_async_copy(ys_ref.at[d], gbuf.at[sl, kk, r], sem.at[sl]).start()
            return carry

        lax.fori_loop(0, tm // SUBLANES, issue, 0)

    @pl.when(i == 0)
    def _():
        gather(dcur_ref, 0)

    @pl.when(i + 1 < n)
    def _():
        gather(dnext_ref, 1 - slot)

    for kk in range(TOP_K):
        pltpu.make_async_copy(gbuf.at[1 - slot, kk], gbuf.at[slot, kk], sem.at[slot]).wait()

    rg = rg_ref[...]
    y = rg[:, 0:1] * gbuf[slot, 0].reshape(tm, D_MODEL)
    for kk in range(1, TOP_K):
        y = y + rg[:, kk:kk + 1] * gbuf[slot, kk].reshape(tm, D_MODEL)
    o_ref[...] = x1_ref[...] + mod_ref[5:6, :] * _rms(y, gpost_ref[...])


def _combine(dest, x1, rg, mod, g_post, ys, *, tm):
    nb, nt, _ = x1.shape
    ntile = nt // tm
    n = nb * ntile
    tok = lambda width: pl.BlockSpec((tm, width), lambda i: (i, 0))
    dest = dest.reshape(n, 1, tm * TOP_K)
    dspec = lambda f: pl.BlockSpec((1, 1, tm * TOP_K), lambda i: (f(i), 0, 0), memory_space=pltpu.SMEM)
    out = pl.pallas_call(
        functools.partial(_combine_kernel, tm=tm, n=n),
        out_shape=jax.ShapeDtypeStruct((nb * nt, D_MODEL), F32),
        grid=(n,),
        in_specs=[dspec(lambda i: i), dspec(lambda i: jnp.minimum(i + 1, n - 1)),
                  tok(D_MODEL), tok(LANES),
                  pl.BlockSpec((None, N_MOD, D_MODEL), lambda i: (i // ntile, 0, 0)),
                  pl.BlockSpec(g_post.shape, lambda i: (0, 0)),
                  pl.BlockSpec(memory_space=pl.ANY)],
        out_specs=tok(D_MODEL),
        scratch_shapes=[pltpu.VMEM((2, TOP_K, tm) + ROW_TILE, F32),
                        pltpu.SemaphoreType.DMA((2,))],
        compiler_params=_cparams(1),
        name="combine",
    )(dest, dest, x1.reshape(nb * nt, D_MODEL), rg.reshape(nb * nt, LANES), mod, g_post, ys)
    return out.reshape(x1.shape)


def _rot_swap(w):
    half = QK_ROPE // 2
    return jnp.concatenate([-w[..., half:], w[..., :half]], axis=-1)


def _prep_weights(w_in, g_q_a, w_q_b, g_kv_a, w_uk, w_uv, w_pool, pool_scale, w_o, g_pre_mix, g_post_mix,
                  g_pre_ffn, w_router, b_router, w_gu, b_gu, w_down, b_down, g_post_ffn):
    row = lambda a: a.reshape(1, -1).astype(F32)
    w_kpe = w_in[:, U_OFF:U_OFF + QK_ROPE]
    zeros = lambda *s: jnp.zeros(s, F32)
    d = D_MODEL
    slab = lambda a: jnp.concatenate([zeros(d, QK_NOPE), a, zeros(d, LANES - QK_HEAD)], axis=1)
    w_in_ext = jnp.concatenate([w_in[:, :U_OFF], w_in[:, U_OFF + QK_ROPE:], slab(w_kpe), slab(_rot_swap(w_kpe))],
                               axis=1)
    pad_q = zeros(Q_LORA, N_HEADS, LANES - QK_HEAD)
    wq_plain = jnp.concatenate([w_q_b, pad_q], axis=2).reshape(Q_LORA, HEAD_SLABS)
    wq_swap = jnp.concatenate([zeros(Q_LORA, N_HEADS, QK_NOPE), _rot_swap(w_q_b[..., QK_NOPE:]), pad_q],
                              axis=2).reshape(Q_LORA, HEAD_SLABS)
    pad_kv = zeros(KV_LORA, N_HEADS, LANES - QK_NOPE)
    wk = jnp.concatenate([w_uk, pad_kv], axis=2).reshape(KV_LORA, HEAD_SLABS)
    wv = jnp.concatenate([w_uv, pad_kv], axis=2).reshape(KV_LORA, HEAD_SLABS)
    mla_w = N_HEADS * V_HEAD
    woa = jnp.concatenate([w_o[:mla_w].reshape(N_HEADS, V_HEAD, d), zeros(N_HEADS, LANES - V_HEAD, d)],
                          axis=1).reshape(HEAD_SLABS, d)
    return {
        "g_pre_mix": row(g_pre_mix), "g_q_a": row(g_q_a), "g_kv_a": row(g_kv_a), "pool_scale": row(pool_scale),
        "w_in": w_in_ext.astype(BF16),
        "w_q": jnp.concatenate([wq_plain, wq_swap], axis=1).astype(BF16),
        "w_k": wk.astype(BF16), "w_vt": wv.T.astype(BF16),
        "w_pool": w_pool.astype(BF16),
        "g_post_mix": row(g_post_mix), "g_pre_ffn": row(g_pre_ffn), "g_post_ffn": row(g_post_ffn),
        "w_o_attn": woa.astype(BF16), "w_o_pool": w_o[mla_w:].astype(BF16),
        "w_router": jnp.pad(w_router, ((0, 0), (0, LANES - N_EXPERTS))).astype(BF16),
        "b_router": jnp.pad(b_router, (0, LANES - N_EXPERTS)).reshape(1, LANES).astype(F32),
        "w_gu": w_gu, "b_gu": b_gu.reshape(N_EXPERTS, 1, 2 * D_FF).astype(F32),
        "w_down": w_down, "b_down": b_down.reshape(N_EXPERTS, 1, D_MODEL).astype(F32),
    }


def _rope_tables(pos):
    half = QK_ROPE // 2
    inv = ROPE_THETA ** (-jnp.arange(half, dtype=F32) / half)
    ang = pos.astype(F32)[:, None] * inv[None, :]
    cos, sin = jnp.cos(ang), jnp.sin(ang)
    n = pos.shape[0]
    cos_t = jnp.concatenate([jnp.ones((n, QK_NOPE), F32), cos, cos, jnp.zeros((n, LANES - QK_HEAD), F32)], axis=1)
    sin_t = jnp.concatenate([jnp.zeros((n, QK_NOPE), F32), sin, sin, jnp.zeros((n, LANES - QK_HEAD), F32)], axis=1)
    return cos_t, sin_t


def _tile(n, pref):
    return pref if n % pref == 0 else n


def _mixer_path(x, mod, pos0, hist, cache, w, cnt0, xs, cap):
    nb, nt, _ = x.shape
    tm = _tile(nt, 512)
    cos_t, sin_t = _rope_tables(pos0 + jnp.arange(nt, dtype=jnp.int32))
    q, k, vt, kv_new, kslab, pooled, pool_tail = _premix(x, mod, cos_t, sin_t, hist, w, tm=tm, pos0=pos0)
    if cache is None:
        attn = _attention(q, k, vt, tq=tm, tk=_tile(nt, 2 * tm), causal=True, kv_len=nt)
    else:
        ckv, ckpe = cache
        past = ckv.shape[1]
        kv_len = past + nt
        tk = -(-kv_len // 256) * 256
        ckpe_slab = jnp.pad(ckpe, ((0, 0), (0, 0), (QK_NOPE, LANES - QK_HEAD)))
        lat_all = jnp.concatenate([ckv, kv_new, jnp.zeros((nb, tk - kv_len, KV_LORA), F32)], axis=1)
        kpe_all = jnp.concatenate([ckpe_slab, kslab, jnp.zeros((nb, tk - kv_len, LANES), F32)], axis=1)
        k_all, vt_all = _kvproj(lat_all, kpe_all, w["w_k"], w["w_vt"], tm=_tile(tk, 768))
        tq = -(-nt // LANES) * LANES
        q_pad = jnp.pad(q, ((0, 0), (0, tq - nt), (0, 0)))
        attn = _attention(q_pad, k_all, vt_all, tq=tq, tk=tk, causal=False, kv_len=kv_len)[:, :nt]
    x1, ri, rg, cnt, xs = _postmix(attn, pooled, x, mod, cnt0, w, xs, tm=tm, cap=cap)
    return x1, xs, ri, rg, cnt, kv_new, kslab[..., QK_NOPE:QK_HEAD], pool_tail[:, 1:]


def kernel(x_prompt, x_sample, c_prompt, c_sample, cache_kv_latent, cache_k_rope, state_pool, w_ada, b_ada,
           g_pre_mix, w_in, g_q_a, w_q_b, g_kv_a, w_uk, w_uv, w_pool, pool_scale, w_o, g_post_mix, g_pre_ffn,
           w_router, b_router, w_gu, b_gu, w_down, b_down, g_post_ffn):
    assert w_ada.shape[0] == 1, "single-layer step"
    bp, sp, _ = x_prompt.shape
    bs, ss, _ = x_sample.shape
    past = cache_kv_latent.shape[2]
    w = _prep_weights(w_in[0], g_q_a[0], w_q_b[0], g_kv_a[0], w_uk[0], w_uv[0], w_pool[0], pool_scale[0], w_o[0],
                      g_pre_mix[0], g_post_mix[0], g_pre_ffn[0], w_router[0], b_router[0], w_gu[0], b_gu[0],
                      w_down[0], b_down[0], g_post_ffn[0])

    mod = _ada(jnp.concatenate([c_prompt, c_sample], axis=0), w_ada[0], b_ada[0])
    mod = mod.reshape(bp + bs, N_MOD, D_MODEL)
    mod_p, mod_s = mod[:bp], mod[bp:]

    hist_p = jnp.zeros((bp, HIST_ROWS, POOL_WIDTH), F32)
    hist_s = jnp.pad(state_pool[0], ((0, 0), (1, 0), (0, 0)))
    n_tok = bp * sp + bs * ss
    cap = (n_tok // MOE_BLOCK + 1) * MOE_BLOCK
    cnt0 = jnp.zeros((8, LANES), F32)
    x1p, xs, rip, rgp, cntp, kv_p, kpe_p, pool_p = _mixer_path(x_prompt, mod_p, 0, hist_p, None, w, cnt0, None, cap)
    x1s, xs, ris, rgs, cnts, kv_s, kpe_s, pool_s = _mixer_path(
        x_sample, mod_s, past, hist_s, (cache_kv_latent[0], cache_k_rope[0]), w, cntp, xs, cap)

    counts = cnts[0, :N_EXPERTS].astype(jnp.int32)
    blocks = (counts + MOE_BLOCK - 1) // MOE_BLOCK
    blk_end = jnp.cumsum(blocks)
    n_blocks = -(-(n_tok * TOP_K) // MOE_BLOCK) + N_EXPERTS
    n_used = blk_end[-1:].astype(jnp.int32)
    grid_i = jnp.minimum(jnp.arange(n_blocks, dtype=jnp.int32), n_used[0] - 1)
    block_e = jnp.sum((blk_end[None, :] <= grid_i[:, None]).astype(jnp.int32), axis=1)
    blk_start = blk_end - blocks
    block_idx = (block_e * (cap // MOE_BLOCK) + grid_i
                 - jnp.sum(jnp.where(block_e[:, None] == jnp.arange(N_EXPERTS, dtype=jnp.int32), blk_start, 0), axis=1))
    zfrom = (jnp.arange(N_EXPERTS, dtype=jnp.int32) * cap + counts).astype(jnp.int32)
    xs = _zero_pads(zfrom, xs)
    ys = _experts(block_e, block_idx.astype(jnp.int32), n_used, xs, w)
    dest_p = rip[..., :TOP_K]
    dest_s = ris[..., :TOP_K]
    y_p = _combine(dest_p, x1p, rgp, mod_p, w["g_post_ffn"], ys, tm=_tile(sp, 512))
    y_s = _combine(dest_s, x1s, rgs, mod_s, w["g_post_ffn"], ys, tm=_tile(ss, 256))
    return (y_p, y_s, kv_p[None], kpe_p[None], pool_p[None], kv_s[None], kpe_s[None], pool_s[None])
```

```python
import functools

import jax
import jax.numpy as jnp
from jax import lax
from jax.experimental import pallas as pl
from jax.experimental.pallas import tpu as pltpu

F32 = jnp.float32
BF16 = jnp.bfloat16

D_MODEL = 1024
CHUNK = 64
N_HEADS = 8
QK_NOPE = 64
QK_ROPE = 32
QK_HEAD = QK_NOPE + QK_ROPE
V_HEAD = 64
Q_LORA = 384
KV_LORA = 256
ROPE_THETA = 10000.0
POOL_WINDOWS = (2, 4, 8, 16)
POOL_GROUP_DIM = 128
POOL_WIDTH = POOL_GROUP_DIM * len(POOL_WINDOWS)
POOL_HIST = max(POOL_WINDOWS) - 1
HIST_ROWS = POOL_HIST + 1
N_EXPERTS = 32
TOP_K = 4
D_FF = 1024
SWIGLU_LIMIT = 7.0
SWIGLU_ALPHA = 1.702
N_MOD = 6
EPS = 1e-6

LANES = 128
HEAD_SLABS = N_HEADS * LANES
ONE_LANE = V_HEAD
IN_EXT = Q_LORA + KV_LORA + POOL_WIDTH + 2 * LANES
U_OFF = Q_LORA + KV_LORA
KPE_OFF = U_OFF + POOL_WIDTH
SM_SCALE = QK_HEAD ** -0.5
LOG2_E = 1.4426950408889634
Q_SCALE = SM_SCALE * LOG2_E
NEG_INF = float("-inf")

SUBLANES = 8
ROW_TILE = (SUBLANES, D_MODEL // SUBLANES)
MOE_BLOCK = 512
VMEM_LIMIT = 56 * 1024 * 1024


def _cparams(n_axes, vmem=VMEM_LIMIT):
    return pltpu.CompilerParams(dimension_semantics=("arbitrary",) * n_axes, vmem_limit_bytes=vmem)


def _rms(x, g):
    return x * lax.rsqrt(jnp.mean(x * x, axis=-1, keepdims=True) + EPS) * g


def _ada_kernel(c_ref, w_ref, b_ref, o_ref):
    c = c_ref[...]
    s = (c * jax.nn.sigmoid(c)).astype(BF16)
    o_ref[...] = jnp.dot(s, w_ref[...].astype(BF16), preferred_element_type=F32) + b_ref[...]


def _ada(c, w_ada, b_ada):
    nb = c.shape[0]
    return pl.pallas_call(
        _ada_kernel,
        out_shape=jax.ShapeDtypeStruct((nb, N_MOD * D_MODEL), F32),
        grid=(N_MOD,),
        in_specs=[pl.BlockSpec((nb, D_MODEL), lambda j: (0, 0)),
                  pl.BlockSpec((D_MODEL, D_MODEL), lambda j: (0, j)),
                  pl.BlockSpec((1, D_MODEL), lambda j: (0, j))],
        out_specs=pl.BlockSpec((nb, D_MODEL), lambda j: (0, j)),
        compiler_params=_cparams(1),
        name="ada",
    )(c, w_ada, b_ada.reshape(1, -1))


def _store_kv(kvb, kslab, wk_ref, wvt_ref, k_ref, vt_ref):
    kk = jnp.dot(kvb, wk_ref[...], preferred_element_type=F32)
    for hd in range(N_HEADS):
        sl = slice(hd * LANES, (hd + 1) * LANES)
        k_ref[:, sl] = (kk[:, sl] + kslab).astype(BF16)
    vt = lax.dot_general(wvt_ref[...], kvb, (((1,), (1,)), ((), ())), preferred_element_type=F32)
    row = lax.broadcasted_iota(jnp.int32, (HEAD_SLABS, 1), 0)
    vt_ref[...] = (vt + (row % LANES == ONE_LANE).astype(F32)).astype(BF16)


def _premix_kernel(x_ref, mod_ref, cos_ref, sin_ref, hist_ref, gpre_ref, gqa_ref, gkv_ref, pscale_ref,
                   win_ref, wq_ref, wk_ref, wvt_ref, wpool_ref,
                   q_ref, k_ref, vt_ref, kv_ref, kpe_ref, pooled_ref, poolnew_ref,
                   ubuf, *, tm, pos0):
    t = pl.program_id(1)
    x = x_ref[...]
    h = (_rms(x, gpre_ref[...]) * (1.0 + mod_ref[1:2, :]) + mod_ref[0:1, :]).astype(BF16)
    z = jnp.dot(h, win_ref[...], preferred_element_type=F32)
    cosv = cos_ref[...]
    sinv = sin_ref[...]

    qan = _rms(z[:, :Q_LORA], gqa_ref[...]).astype(BF16)
    qq = jnp.dot(qan, wq_ref[...], preferred_element_type=F32)
    for hd in range(N_HEADS):
        a = qq[:, hd * LANES:(hd + 1) * LANES]
        b = qq[:, HEAD_SLABS + hd * LANES:HEAD_SLABS + (hd + 1) * LANES]
        q_ref[:, hd * LANES:(hd + 1) * LANES] = ((a * cosv + b * sinv) * Q_SCALE).astype(BF16)

    kvn = _rms(z[:, Q_LORA:U_OFF], gkv_ref[...])
    kv_ref[...] = kvn
    kslab = z[:, KPE_OFF:KPE_OFF + LANES] * cosv + z[:, KPE_OFF + LANES:KPE_OFF + 2 * LANES] * sinv
    kpe_ref[...] = kslab
    _store_kv(kvn.astype(BF16), kslab, wk_ref, wvt_ref, k_ref, vt_ref)

    @pl.when(t == 0)
    def _():
        ubuf[0:HIST_ROWS, :] = hist_ref[...]

    @pl.when(t > 0)
    def _():
        ubuf[0:HIST_ROWS, :] = ubuf[tm:tm + HIST_ROWS, :]

    ubuf[HIST_ROWS:HIST_ROWS + tm, :] = z[:, U_OFF:KPE_OFF]
    pos = pos0 + t * tm + lax.broadcasted_iota(jnp.int32, (tm, 1), 0)
    for g, w in enumerate(POOL_WINDOWS):
        sl = slice(g * POOL_GROUP_DIM, (g + 1) * POOL_GROUP_DIM)
        u = ubuf[HIST_ROWS:HIST_ROWS + tm, sl]
        acc = u
        for j in range(1, w):
            acc = acc + ubuf[HIST_ROWS - j:HIST_ROWS - j + tm, sl]
        cnt = jnp.minimum(pos + 1, w).astype(F32)
        d = (acc / cnt - u).astype(BF16)
        y = jnp.dot(d, wpool_ref[g], preferred_element_type=F32) * pscale_ref[:, sl]
        pooled_ref[:, sl] = y.astype(BF16)
    poolnew_ref[...] = ubuf[tm:tm + HIST_ROWS, :]


def _premix(x, mod, cos_t, sin_t, hist, w, *, tm, pos0):
    nb, nt, _ = x.shape
    grid = (nb, nt // tm)
    tok = lambda width: pl.BlockSpec((None, tm, width), lambda b, t: (b, t, 0))
    full = lambda a: pl.BlockSpec(a.shape, lambda b, t: (0,) * a.ndim)
    tab = pl.BlockSpec((tm, LANES), lambda b, t: (t, 0))
    perb = lambda rows, width: pl.BlockSpec((None, rows, width), lambda b, t: (b, 0, 0))
    outs = [
        jax.ShapeDtypeStruct((nb, nt, HEAD_SLABS), BF16),
        jax.ShapeDtypeStruct((nb, nt, HEAD_SLABS), BF16),
        jax.ShapeDtypeStruct((nb, HEAD_SLABS, nt), BF16),
        jax.ShapeDtypeStruct((nb, nt, KV_LORA), F32),
        jax.ShapeDtypeStruct((nb, nt, LANES), F32),
        jax.ShapeDtypeStruct((nb, nt, POOL_WIDTH), BF16),
        jax.ShapeDtypeStruct((nb, HIST_ROWS, POOL_WIDTH), F32),
    ]
    return pl.pallas_call(
        functools.partial(_premix_kernel, tm=tm, pos0=pos0),
        out_shape=outs,
        grid=grid,
        in_specs=[tok(D_MODEL), perb(N_MOD, D_MODEL), tab, tab, perb(HIST_ROWS, POOL_WIDTH),
                  full(w["g_pre_mix"]), full(w["g_q_a"]), full(w["g_kv_a"]), full(w["pool_scale"]),
                  full(w["w_in"]), full(w["w_q"]), full(w["w_k"]), full(w["w_vt"]), full(w["w_pool"])],
        out_specs=[tok(HEAD_SLABS), tok(HEAD_SLABS),
                   pl.BlockSpec((None, HEAD_SLABS, tm), lambda b, t: (b, 0, t)),
                   tok(KV_LORA), tok(LANES), tok(POOL_WIDTH), perb(HIST_ROWS, POOL_WIDTH)],
        scratch_shapes=[pltpu.VMEM((HIST_ROWS + tm, POOL_WIDTH), F32)],
        compiler_params=_cparams(2),
        name="premix",
    )(x, mod, cos_t, sin_t, hist, w["g_pre_mix"], w["g_q_a"], w["g_kv_a"], w["pool_scale"],
      w["w_in"], w["w_q"], w["w_k"], w["w_vt"], w["w_pool"])


def _kvproj_kernel(kv_ref, kpe_ref, wk_ref, wvt_ref, k_ref, vt_ref):
    _store_kv(kv_ref[...].astype(BF16), kpe_ref[...], wk_ref, wvt_ref, k_ref, vt_ref)


def _kvproj(kv, kpe_slab, w_k, w_vt, *, tm):
    nb, nt, _ = kv.shape
    tok = lambda width: pl.BlockSpec((None, tm, width), lambda b, t: (b, t, 0))
    full = lambda a: pl.BlockSpec(a.shape, lambda b, t: (0,) * a.ndim)
    return pl.pallas_call(
        _kvproj_kernel,
        out_shape=[jax.ShapeDtypeStruct((nb, nt, HEAD_SLABS), BF16),
                   jax.ShapeDtypeStruct((nb, HEAD_SLABS, nt), BF16)],
        grid=(nb, nt // tm),
        in_specs=[tok(KV_LORA), tok(LANES), full(w_k), full(w_vt)],
        out_specs=[tok(HEAD_SLABS), pl.BlockSpec((None, HEAD_SLABS, tm), lambda b, t: (b, 0, t))],
        compiler_params=_cparams(2),
        name="kvproj",
    )(kv, kpe_slab, w_k, w_vt)


def _attn_kernel(q_ref, k_ref, vt_ref, o_ref, m_ref, acc_ref, *, tq, tk, nk, causal, kv_len):
    qi = pl.program_id(1)
    ki = pl.program_id(2)
    last = ((qi + 1) * tq - 1) // tk if causal else nk - 1
    need_len_mask = kv_len < nk * tk

    @pl.when(ki == 0)
    def _():
        m_ref[...] = jnp.full(m_ref.shape, NEG_INF, F32)
        acc_ref[...] = jnp.zeros(acc_ref.shape, F32)

    def step(masked, nkeys=tk):
        if masked:
            kpos = ki * tk + lax.broadcasted_iota(jnp.int32, (nkeys, tq), 0)
            vis = None
            if causal:
                qpos = qi * tq + lax.broadcasted_iota(jnp.int32, (nkeys, tq), 1)
                vis = (kpos // CHUNK) <= (qpos // CHUNK)
            if need_len_mask:
                lm = kpos < kv_len
                vis = lm if vis is None else (vis & lm)

        def scores(hd):
            sl = slice(hd * LANES, (hd + 1) * LANES)
            return lax.dot_general(k_ref[:nkeys, sl], q_ref[:, sl], (((1,), (1,)), ((), ())),
                                   preferred_element_type=F32)

        def accumulate(hd, alpha, p):
            sl = slice(hd * LANES, (hd + 1) * LANES)
            acc_ref[hd] = alpha * acc_ref[hd] + jnp.dot(vt_ref[sl, :nkeys], p, preferred_element_type=F32)

        s_next = scores(0)
        pending = None
        for hd in range(N_HEADS):
            s = s_next
            if hd + 1 < N_HEADS:
                s_next = scores(hd + 1)
            if pending is not None:
                accumulate(*pending)
            if masked:
                s = jnp.where(vis, s, NEG_INF)
            m_prev = m_ref[hd:hd + 1, :]
            m_new = jnp.maximum(m_prev, jnp.max(s, axis=0, keepdims=True))
            m_ref[hd:hd + 1, :] = m_new
            pending = (hd, jnp.exp2(m_prev - m_new), jnp.exp2(s - m_new).astype(BF16))
        accumulate(*pending)

    if causal and tk == 2 * tq and not need_len_mask:
        @pl.when(ki < last)
        def _():
            step(False)

        @pl.when((ki == last) & (qi % 2 == 0))
        def _():
            step(True, tq)

        @pl.when((ki == last) & (qi % 2 == 1))
        def _():
            step(True)
    elif causal or need_len_mask:
        @pl.when(ki < last)
        def _():
            step(False)

        @pl.when(ki == last)
        def _():
            step(True)
    else:
        step(False)

    @pl.when(ki == nk - 1)
    def _():
        for hd in range(N_HEADS):
            acc = acc_ref[hd]
            out_t = acc / acc[ONE_LANE:ONE_LANE + 1, :]
            o_ref[:, hd * LANES:(hd + 1) * LANES] = out_t.T.astype(BF16)


def _attention(q, k, vt, *, tq, tk, causal, kv_len):
    nb, nq_tot, _ = q.shape
    nk = k.shape[1] // tk
    nq = nq_tot // tq
    if causal:
        last = lambda i: ((i + 1) * tq - 1) // tk
        kmap = lambda b, i, j: (b, jnp.minimum(j, last(i)), 0)
        vmap = lambda b, i, j: (b, 0, jnp.minimum(j, last(i)))
    else:
        kmap = lambda b, i, j: (b, j, 0)
        vmap = lambda b, i, j: (b, 0, j)
    return pl.pallas_call(
        functools.partial(_attn_kernel, tq=tq, tk=tk, nk=nk, causal=causal, kv_len=kv_len),
        out_shape=jax.ShapeDtypeStruct((nb, nq_tot, HEAD_SLABS), BF16),
        grid=(nb, nq, nk),
        in_specs=[pl.BlockSpec((None, tq, HEAD_SLABS), lambda b, i, j: (b, i, 0)),
                  pl.BlockSpec((None, tk, HEAD_SLABS), kmap),
                  pl.BlockSpec((None, HEAD_SLABS, tk), vmap)],
        out_specs=pl.BlockSpec((None, tq, HEAD_SLABS), lambda b, i, j: (b, i, 0)),
        scratch_shapes=[pltpu.VMEM((N_HEADS, tq), F32), pltpu.VMEM((N_HEADS, LANES, tq), F32)],
        compiler_params=_cparams(3),
        name="attention",
    )(q, k, vt)


def _postmix_kernel(*refs, tm, cap, first_call):
    (attn_ref, pooled_ref, x_ref, mod_ref, cnt0_ref, gpost_ref, gffn_ref, woa_ref, wop_ref, wr_ref, br_ref) = refs[:11]
    rest = refs[11:] if first_call else refs[12:]
    x1_ref, ri_ref, rg_ref, cnt_ref, xs_ref, carry, ltri, hbuf, dvm, dsm, dsem, csem = rest
    step = pl.program_id(0) * pl.num_programs(1) + pl.program_id(1)
    n_steps = pl.num_programs(0) * pl.num_programs(1)
    slot = step % 2

    @pl.when(step == 0)
    def _():
        carry[...] = cnt0_ref[...]
        r = lax.broadcasted_iota(jnp.int32, (tm, tm), 0)
        c = lax.broadcasted_iota(jnp.int32, (tm, tm), 1)
        ltri[...] = (r > c).astype(BF16)

    n_groups = tm // SUBLANES
    n_chunks = min(16, n_groups)
    per_chunk = n_groups // n_chunks

    def issue_rows(buf, j0, j1):
        def issue(j, c):
            for s in range(SUBLANES):
                r = j * SUBLANES + s
                for kk in range(TOP_K):
                    pltpu.async_copy(hbuf.at[buf, r], xs_ref.at[dsm[kk, r]], dsem.at[buf], priority=1)
            return c

        lax.fori_loop(j0, j1, issue, 0)

    chunks = iter(range(n_chunks))

    def issue_prev_chunk():
        c = next(chunks, None)
        if c is not None:
            @pl.when(step >= 1)
            def _():
                issue_rows(1 - slot, c * per_chunk, (c + 1) * per_chunk)

    issue_prev_chunk()
    mix = (jnp.dot(attn_ref[...], woa_ref[...], preferred_element_type=F32)
           + jnp.dot(pooled_ref[...], wop_ref[...], preferred_element_type=F32))
    issue_prev_chunk()
    x1 = x_ref[...] + mod_ref[2:3, :] * _rms(mix, gpost_ref[...])
    x1_ref[...] = x1
    issue_prev_chunk()
    h2 = _rms(x1, gffn_ref[...]) * (1.0 + mod_ref[4:5, :]) + mod_ref[3:4, :]

    lane = lax.broadcasted_iota(jnp.int32, (tm, LANES), 1).astype(F32)
    logits = jnp.dot(h2.astype(BF16), wr_ref[...], preferred_element_type=F32) + br_ref[...]
    logits = jnp.where(lane < N_EXPERTS, logits, NEG_INF)
    issue_prev_chunk()
    sel = jnp.zeros((tm, LANES), F32)
    ids, vals, hots = [], [], []
    for _ in range(TOP_K):
        mk = jnp.max(logits, axis=1, keepdims=True)
        idx = jnp.min(jnp.where(logits == mk, lane, float(LANES)), axis=1, keepdims=True)
        hot = lane == idx
        logits = jnp.where(hot, NEG_INF, logits)
        sel = sel + hot.astype(F32)
        ids.append(idx)
        vals.append(mk)
        hots.append(hot)
        issue_prev_chunk()
    ex = [jnp.exp(vk - vals[0]) for vk in vals]
    denom = ex[0] + ex[1] + ex[2] + ex[3]
    issue_prev_chunk()

    before = jnp.dot(ltri[...], sel.astype(BF16), preferred_element_type=F32) + carry[0:1, :]
    issue_prev_chunk()
    ri = jnp.zeros((tm, LANES), F32)
    rg = jnp.zeros((tm, LANES), F32)
    for kk in range(TOP_K):
        rank = jnp.sum(jnp.where(hots[kk], before, 0.0), axis=1, keepdims=True)
        ri = jnp.where(lane == kk, ids[kk] * float(cap) + rank, ri)
        rg = jnp.where(lane == kk, ex[kk] / denom, rg)
        issue_prev_chunk()
    ri_ref[...] = ri.astype(jnp.int32)
    rg_ref[...] = rg
    issue_prev_chunk()
    carry[0:1, :] = carry[0:1, :] + jnp.sum(sel, axis=0, keepdims=True)
    cnt_ref[...] = carry[...]
    issue_prev_chunk()
    assert next(chunks, None) is None

    def drain(buf):
        for _ in range(TOP_K):
            pltpu.make_async_copy(xs_ref.at[pl.ds(0, tm)], xs_ref.at[pl.ds(0, tm)], dsem.at[buf]).wait()

    @pl.when(step >= 2)
    def _():
        drain(slot)

    hbuf[slot] = h2.reshape(tm, *ROW_TILE)
    dvm[...] = ri.T[0:SUBLANES, :].astype(jnp.int32)
    to_smem = pltpu.make_async_copy(dvm, dsm, csem)
    to_smem.start()
    to_smem.wait()

    @pl.when(step == n_steps - 1)
    def _():
        issue_rows(slot, 0, n_groups)
        drain(slot)

        @pl.when(step >= 1)
        def _():
            drain(1 - slot)


def _postmix(attn, pooled, x, mod, cnt0, w, xs, *, tm, cap):
    nb, nt, _ = x.shape
    first_call = xs is None
    tok = lambda width: pl.BlockSpec((None, tm, width), lambda b, t: (b, t, 0))
    full = lambda a: pl.BlockSpec(a.shape, lambda b, t: (0,) * a.ndim)
    outs = [
        jax.ShapeDtypeStruct((nb, nt, D_MODEL), F32),
        jax.ShapeDtypeStruct((nb, nt, LANES), jnp.int32),
        jax.ShapeDtypeStruct((nb, nt, LANES), F32),
        jax.ShapeDtypeStruct((8, LANES), F32),
        jax.ShapeDtypeStruct((N_EXPERTS * cap,) + ROW_TILE, F32),
    ]
    in_specs = [tok(HEAD_SLABS), tok(POOL_WIDTH), tok(D_MODEL),
                pl.BlockSpec((None, N_MOD, D_MODEL), lambda b, t: (b, 0, 0)),
                full(cnt0), full(w["g_post_mix"]), full(w["g_pre_ffn"]),
                full(w["w_o_attn"]), full(w["w_o_pool"]), full(w["w_router"]), full(w["b_router"])]
    args = [attn, pooled, x, mod, cnt0, w["g_post_mix"], w["g_pre_ffn"],
            w["w_o_attn"], w["w_o_pool"], w["w_router"], w["b_router"]]
    if not first_call:
        in_specs.append(pl.BlockSpec(memory_space=pl.ANY))
        args.append(xs)
    return pl.pallas_call(
        functools.partial(_postmix_kernel, tm=tm, cap=cap, first_call=first_call),
        out_shape=outs,
        grid=(nb, nt // tm),
        in_specs=in_specs,
        out_specs=[tok(D_MODEL), tok(LANES), tok(LANES), pl.BlockSpec((8, LANES), lambda b, t: (0, 0)),
                   pl.BlockSpec(memory_space=pl.ANY)],
        scratch_shapes=[pltpu.VMEM((8, LANES), F32), pltpu.VMEM((tm, tm), BF16),
                        pltpu.VMEM((2, tm) + ROW_TILE, F32), pltpu.VMEM((SUBLANES, tm), jnp.int32),
                        pltpu.SMEM((SUBLANES, tm), jnp.int32),
                        pltpu.SemaphoreType.DMA((2,)), pltpu.SemaphoreType.DMA(())],
        input_output_aliases={} if first_call else {11: 4},
        compiler_params=_cparams(2),
        name="postmix",
    )(*args)


def _zero_pads_kernel(zfrom_ref, xs_in_ref, xs_ref, zbuf, zsem):
    del xs_in_ref
    zbuf[...] = jnp.zeros(zbuf.shape, F32)
    copies = [pltpu.make_async_copy(zbuf, xs_ref.at[pl.ds(zfrom_ref[e], MOE_BLOCK)], zsem)
              for e in range(N_EXPERTS)]
    for cp in copies:
        cp.start()
    for cp in copies:
        cp.wait()


def _zero_pads(zfrom, xs):
    return pl.pallas_call(
        _zero_pads_kernel,
        out_shape=jax.ShapeDtypeStruct(xs.shape, xs.dtype),
        grid_spec=pltpu.PrefetchScalarGridSpec(
            num_scalar_prefetch=1,
            grid=(1,),
            in_specs=[pl.BlockSpec(memory_space=pl.ANY)],
            out_specs=pl.BlockSpec(memory_space=pl.ANY),
            scratch_shapes=[pltpu.VMEM((MOE_BLOCK,) + ROW_TILE, F32), pltpu.SemaphoreType.DMA(())],
        ),
        input_output_aliases={1: 0},
        compiler_params=_cparams(1),
        name="zero_pads",
    )(zfrom, xs)


def _expert_kernel(be_ref, bi_ref, nu_ref, xs_ref, wgu_ref, bgu_ref, wd_ref, bd_ref, ys_ref, wgu_b, wd_b):
    del bi_ref
    i = pl.program_id(0)

    @pl.when(i < nu_ref[0])
    def _():
        @pl.when((i == 0) | (be_ref[i] != be_ref[jnp.maximum(i - 1, 0)]))
        def _():
            wgu_b[...] = wgu_ref[...].astype(BF16)
            wd_b[...] = wd_ref[...].astype(BF16)

        x = xs_ref[...].reshape(MOE_BLOCK, D_MODEL).astype(BF16)
        gu = jnp.dot(x, wgu_b[...], preferred_element_type=F32) + bgu_ref[...]
        g = jnp.minimum(gu[:, :D_FF], SWIGLU_LIMIT)
        u = jnp.clip(gu[:, D_FF:], -SWIGLU_LIMIT, SWIGLU_LIMIT)
        a = (u + 1.0) * (g * jax.nn.sigmoid(SWIGLU_ALPHA * g))
        y = jnp.dot(a.astype(BF16), wd_b[...], preferred_element_type=F32) + bd_ref[...]
        ys_ref[...] = y.reshape(MOE_BLOCK, *ROW_TILE)


def _experts(block_e, block_idx, n_used, xs, w):
    nblk = block_e.shape[0]
    row = lambda i, be, bi, nu: (bi[i], 0, 0)
    per_e = lambda i, be, bi, nu: (be[i], 0, 0)
    return pl.pallas_call(
        _expert_kernel,
        out_shape=jax.ShapeDtypeStruct(xs.shape, F32),
        grid_spec=pltpu.PrefetchScalarGridSpec(
            num_scalar_prefetch=3,
            grid=(nblk,),
            in_specs=[pl.BlockSpec((MOE_BLOCK,) + ROW_TILE, row),
                      pl.BlockSpec((None, D_MODEL, 2 * D_FF), per_e),
                      pl.BlockSpec((None, 1, 2 * D_FF), per_e),
                      pl.BlockSpec((None, D_FF, D_MODEL), per_e),
                      pl.BlockSpec((None, 1, D_MODEL), per_e)],
            out_specs=pl.BlockSpec((MOE_BLOCK,) + ROW_TILE, row),
            scratch_shapes=[pltpu.VMEM((D_MODEL, 2 * D_FF), BF16), pltpu.VMEM((D_FF, D_MODEL), BF16)],
        ),
        compiler_params=_cparams(1),
        name="experts",
    )(block_e, block_idx, n_used, xs, w["w_gu"], w["b_gu"], w["w_down"], w["b_down"])


def _combine_kernel(dcur_ref, dnext_ref, x1_ref, rg_ref, mod_ref, gpost_ref, ys_ref, o_ref, gbuf, sem, *, tm, n):
    i = pl.program_id(0)
    slot = i % 2

    def gather(dest_ref, sl):
        def issue(j, carry):
            for s in range(SUBLANES):
                r = j * SUBLANES + s
                for kk in range(TOP_K):
                    d = dest_ref[0, 0, j * (SUBLANES * TOP_K) + s * TOP_K + kk]
                    pltpu.make_async_copy(ys_ref.at[d], gbuf.at[sl, kk, r], sem.at[sl]).start()
            return carry

        lax.fori_loop(0, tm // SUBLANES, issue, 0)

    @pl.when(i == 0)
    def _():
        gather(dcur_ref, 0)

    @pl.when(i + 1 < n)
    def _():
        gather(dnext_ref, 1 - slot)

    for kk in range(TOP_K):
        pltpu.make_async_copy(gbuf.at[1 - slot, kk], gbuf.at[slot, kk], sem.at[slot]).wait()

    rg = rg_ref[...]
    y = rg[:, 0:1] * gbuf[slot, 0].reshape(tm, D_MODEL)
    for kk in range(1, TOP_K):
        y = y + rg[:, kk:kk + 1] * gbuf[slot, kk].reshape(tm, D_MODEL)
    o_ref[...] = x1_ref[...] + mod_ref[5:6, :] * _rms(y, gpost_ref[...])


def _combine(dest, x1, rg, mod, g_post, ys, *, tm):
    nb, nt, _ = x1.shape
    ntile = nt // tm
    n = nb * ntile
    tok = lambda width: pl.BlockSpec((tm, width), lambda i: (i, 0))
    dest = dest.reshape(n, 1, tm * TOP_K)
    dspec = lambda f: pl.BlockSpec((1, 1, tm * TOP_K), lambda i: (f(i), 0, 0), memory_space=pltpu.SMEM)
    out = pl.pallas_call(
        functools.partial(_combine_kernel, tm=tm, n=n),
        out_shape=jax.ShapeDtypeStruct((nb * nt, D_MODEL), F32),
        grid=(n,),
        in_specs=[dspec(lambda i: i), dspec(lambda i: jnp.minimum(i + 1, n - 1)),
                  tok(D_MODEL), tok(LANES),
                  pl.BlockSpec((None, N_MOD, D_MODEL), lambda i: (i // ntile, 0, 0)),
                  pl.BlockSpec(g_post.shape, lambda i: (0, 0)),
                  pl.BlockSpec(memory_space=pl.ANY)],
        out_specs=tok(D_MODEL),
        scratch_shapes=[pltpu.VMEM((2, TOP_K, tm) + ROW_TILE, F32),
                        pltpu.SemaphoreType.DMA((2,))],
        compiler_params=_cparams(1),
        name="combine",
    )(dest, dest, x1.reshape(nb * nt, D_MODEL), rg.reshape(nb * nt, LANES), mod, g_post, ys)
    return out.reshape(x1.shape)


def _rot_swap(w):
    half = QK_ROPE // 2
    return jnp.concatenate([-w[..., half:], w[..., :half]], axis=-1)


def _prep_weights(w_in, g_q_a, w_q_b, g_kv_a, w_uk, w_uv, w_pool, pool_scale, w_o, g_pre_mix, g_post_mix,
                  g_pre_ffn, w_router, b_router, w_gu, b_gu, w_down, b_down, g_post_ffn):
    row = lambda a: a.reshape(1, -1).astype(F32)
    w_kpe = w_in[:, U_OFF:U_OFF + QK_ROPE]
    zeros = lambda *s: jnp.zeros(s, F32)
    d = D_MODEL
    slab = lambda a: jnp.concatenate([zeros(d, QK_NOPE), a, zeros(d, LANES - QK_HEAD)], axis=1)
    w_in_ext = jnp.concatenate([w_in[:, :U_OFF], w_in[:, U_OFF + QK_ROPE:], slab(w_kpe), slab(_rot_swap(w_kpe))],
                               axis=1)
    pad_q = zeros(Q_LORA, N_HEADS, LANES - QK_HEAD)
    wq_plain = jnp.concatenate([w_q_b, pad_q], axis=2).reshape(Q_LORA, HEAD_SLABS)
    wq_swap = jnp.concatenate([zeros(Q_LORA, N_HEADS, QK_NOPE), _rot_swap(w_q_b[..., QK_NOPE:]), pad_q],
                              axis=2).reshape(Q_LORA, HEAD_SLABS)
    pad_kv = zeros(KV_LORA, N_HEADS, LANES - QK_NOPE)
    wk = jnp.concatenate([w_uk, pad_kv], axis=2).reshape(KV_LORA, HEAD_SLABS)
    wv = jnp.concatenate([w_uv, pad_kv], axis=2).reshape(KV_LORA, HEAD_SLABS)
    mla_w = N_HEADS * V_HEAD
    woa = jnp.concatenate([w_o[:mla_w].reshape(N_HEADS, V_HEAD, d), zeros(N_HEADS, LANES - V_HEAD, d)],
                          axis=1).reshape(HEAD_SLABS, d)
    return {
        "g_pre_mix": row(g_pre_mix), "g_q_a": row(g_q_a), "g_kv_a": row(g_kv_a), "pool_scale": row(pool_scale),
        "w_in": w_in_ext.astype(BF16),
        "w_q": jnp.concatenate([wq_plain, wq_swap], axis=1).astype(BF16),
        "w_k": wk.astype(BF16), "w_vt": wv.T.astype(BF16),
        "w_pool": w_pool.astype(BF16),
        "g_post_mix": row(g_post_mix), "g_pre_ffn": row(g_pre_ffn), "g_post_ffn": row(g_post_ffn),
        "w_o_attn": woa.astype(BF16), "w_o_pool": w_o[mla_w:].astype(BF16),
        "w_router": jnp.pad(w_router, ((0, 0), (0, LANES - N_EXPERTS))).astype(BF16),
        "b_router": jnp.pad(b_router, (0, LANES - N_EXPERTS)).reshape(1, LANES).astype(F32),
        "w_gu": w_gu, "b_gu": b_gu.reshape(N_EXPERTS, 1, 2 * D_FF).astype(F32),
        "w_down": w_down, "b_down": b_down.reshape(N_EXPERTS, 1, D_MODEL).astype(F32),
    }


def _rope_tables(pos):
    half = QK_ROPE // 2
    inv = ROPE_THETA ** (-jnp.arange(half, dtype=F32) / half)
    ang = pos.astype(F32)[:, None] * inv[None, :]
    cos, sin = jnp.cos(ang), jnp.sin(ang)
    n = pos.shape[0]
    cos_t = jnp.concatenate([jnp.ones((n, QK_NOPE), F32), cos, cos, jnp.zeros((n, LANES - QK_HEAD), F32)], axis=1)
    sin_t = jnp.concatenate([jnp.zeros((n, QK_NOPE), F32), sin, sin, jnp.zeros((n, LANES - QK_HEAD), F32)], axis=1)
    return cos_t, sin_t


def _tile(n, pref):
    return pref if n % pref == 0 else n


def _mixer_path(x, mod, pos0, hist, cache, w, cnt0, xs, cap):
    nb, nt, _ = x.shape
    tm = _tile(nt, 512)
    cos_t, sin_t = _rope_tables(pos0 + jnp.arange(nt, dtype=jnp.int32))
    q, k, vt, kv_new, kslab, pooled, pool_tail = _premix(x, mod, cos_t, sin_t, hist, w, tm=tm, pos0=pos0)
    if cache is None:
        attn = _attention(q, k, vt, tq=tm, tk=_tile(nt, 2 * tm), causal=True, kv_len=nt)
    else:
        ckv, ckpe = cache
        past = ckv.shape[1]
        kv_len = past + nt
        tk = -(-kv_len // 256) * 256
        ckpe_slab = jnp.pad(ckpe, ((0, 0), (0, 0), (QK_NOPE, LANES - QK_HEAD)))
        lat_all = jnp.concatenate([ckv, kv_new, jnp.zeros((nb, tk - kv_len, KV_LORA), F32)], axis=1)
        kpe_all = jnp.concatenate([ckpe_slab, kslab, jnp.zeros((nb, tk - kv_len, LANES), F32)], axis=1)
        k_all, vt_all = _kvproj(lat_all, kpe_all, w["w_k"], w["w_vt"], tm=_tile(tk, 768))
        tq = -(-nt // LANES) * LANES
        q_pad = jnp.pad(q, ((0, 0), (0, tq - nt), (0, 0)))
        attn = _attention(q_pad, k_all, vt_all, tq=tq, tk=tk, causal=False, kv_len=kv_len)[:, :nt]
    x1, ri, rg, cnt, xs = _postmix(attn, pooled, x, mod, cnt0, w, xs, tm=tm, cap=cap)
    return x1, xs, ri, rg, cnt, kv_new, kslab[..., QK_NOPE:QK_HEAD], pool_tail[:, 1:]


def kernel(x_prompt, x_sample, c_prompt, c_sample, cache_kv_latent, cache_k_rope, state_pool, w_ada, b_ada,
           g_pre_mix, w_in, g_q_a, w_q_b, g_kv_a, w_uk, w_uv, w_pool, pool_scale, w_o, g_post_mix, g_pre_ffn,
           w_router, b_router, w_gu, b_gu, w_down, b_down, g_post_ffn):
    assert w_ada.shape[0] == 1, "single-layer step"
    bp, sp, _ = x_prompt.shape
    bs, ss, _ = x_sample.shape
    past = cache_kv_latent.shape[2]
    w = _prep_weights(w_in[0], g_q_a[0], w_q_b[0], g_kv_a[0], w_uk[0], w_uv[0], w_pool[0], pool_scale[0], w_o[0],
                      g_pre_mix[0], g_post_mix[0], g_pre_ffn[0], w_router[0], b_router[0], w_gu[0], b_gu[0],
                      w_down[0], b_down[0], g_post_ffn[0])

    mod = _ada(jnp.concatenate([c_prompt, c_sample], axis=0), w_ada[0], b_ada[0])
    mod = mod.reshape(bp + bs, N_MOD, D_MODEL)
    mod_p, mod_s = mod[:bp], mod[bp:]

    hist_p = jnp.zeros((bp, HIST_ROWS, POOL_WIDTH), F32)
    hist_s = jnp.pad(state_pool[0], ((0, 0), (1, 0), (0, 0)))
    n_tok = bp * sp + bs * ss
    cap = (n_tok // MOE_BLOCK + 1) * MOE_BLOCK
    cnt0 = jnp.zeros((8, LANES), F32)
    x1p, xs, rip, rgp, cntp, kv_p, kpe_p, pool_p = _mixer_path(x_prompt, mod_p, 0, hist_p, None, w, cnt0, None, cap)
    x1s, xs, ris, rgs, cnts, kv_s, kpe_s, pool_s = _mixer_path(
        x_sample, mod_s, past, hist_s, (cache_kv_latent[0], cache_k_rope[0]), w, cntp, xs, cap)

    counts = cnts[0, :N_EXPERTS].astype(jnp.int32)
    blocks = (counts + MOE_BLOCK - 1) // MOE_BLOCK
    blk_end = jnp.cumsum(blocks)
    n_blocks = -(-(n_tok * TOP_K) // MOE_BLOCK) + N_EXPERTS
    n_used = blk_end[-1:].astype(jnp.int32)
    grid_i = jnp.minimum(jnp.arange(n_blocks, dtype=jnp.int32), n_used[0] - 1)
    block_e = jnp.sum((blk_end[None, :] <= grid_i[:, None]).astype(jnp.int32), axis=1)
    blk_start = blk_end - blocks
    block_idx = (block_e * (cap // MOE_BLOCK) + grid_i
                 - jnp.sum(jnp.where(block_e[:, None] == jnp.arange(N_EXPERTS, dtype=jnp.int32), blk_start, 0), axis=1))
    zfrom = (jnp.arange(N_EXPERTS, dtype=jnp.int32) * cap + counts).astype(jnp.int32)
    xs = _zero_pads(zfrom, xs)
    ys = _experts(block_e, block_idx.astype(jnp.int32), n_used, xs, w)
    dest_p = rip[..., :TOP_K]
    dest_s = ris[..., :TOP_K]
    y_p = _combine(dest_p, x1p, rgp, mod_p, w["g_post_ffn"], ys, tm=_tile(sp, 512))
    y_s = _combine(dest_s, x1s, rgs, mod_s, w["g_post_ffn"], ys, tm=_tile(ss, 256))
    return (y_p, y_s, kv_p[None], kpe_p[None], pool_p[None], kv_s[None], kpe_s[None], pool_s[None])
```

```python
import functools

import jax
import jax.numpy as jnp
from jax import lax
from jax.experimental import pallas as pl
from jax.experimental.pallas import tpu as pltpu

F32 = jnp.float32
BF16 = jnp.bfloat16

D_MODEL = 1024
CHUNK = 64
N_HEADS = 8
QK_NOPE = 64
QK_ROPE = 32
QK_HEAD = QK_NOPE + QK_ROPE
V_HEAD = 64
Q_LORA = 384
KV_LORA = 256
ROPE_THETA = 10000.0
POOL_WINDOWS = (2, 4, 8, 16)
POOL_GROUP_DIM = 128
POOL_WIDTH = POOL_GROUP_DIM * len(POOL_WINDOWS)
POOL_HIST = max(POOL_WINDOWS) - 1
HIST_ROWS = POOL_HIST + 1
N_EXPERTS = 32
TOP_K = 4
D_FF = 1024
SWIGLU_LIMIT = 7.0
SWIGLU_ALPHA = 1.702
N_MOD = 6
EPS = 1e-6

LANES = 128
HEAD_SLABS = N_HEADS * LANES
ONE_LANE = V_HEAD
IN_EXT = Q_LORA + KV_LORA + POOL_WIDTH + 2 * LANES
U_OFF = Q_LORA + KV_LORA
KPE_OFF = U_OFF + POOL_WIDTH
SM_SCALE = QK_HEAD ** -0.5
LOG2_E = 1.4426950408889634
Q_SCALE = SM_SCALE * LOG2_E
NEG_INF = float("-inf")

SUBLANES = 8
ROW_TILE = (SUBLANES, D_MODEL // SUBLANES)
MOE_BLOCK = 512
VMEM_LIMIT = 56 * 1024 * 1024


def _cparams(n_axes, vmem=VMEM_LIMIT):
    return pltpu.CompilerParams(dimension_semantics=("arbitrary",) * n_axes, vmem_limit_bytes=vmem)


def _rms(x, g):
    return x * lax.rsqrt(jnp.mean(x * x, axis=-1, keepdims=True) + EPS) * g


def _ada_kernel(c_ref, w_ref, b_ref, o_ref):
    c = c_ref[...]
    s = (c * jax.nn.sigmoid(c)).astype(BF16)
    o_ref[...] = jnp.dot(s, w_ref[...].astype(BF16), preferred_element_type=F32) + b_ref[...]


def _ada(c, w_ada, b_ada):
    nb = c.shape[0]
    return pl.pallas_call(
        _ada_kernel,
        out_shape=jax.ShapeDtypeStruct((nb, N_MOD * D_MODEL), F32),
        grid=(N_MOD,),
        in_specs=[pl.BlockSpec((nb, D_MODEL), lambda j: (0, 0)),
                  pl.BlockSpec((D_MODEL, D_MODEL), lambda j: (0, j)),
                  pl.BlockSpec((1, D_MODEL), lambda j: (0, j))],
        out_specs=pl.BlockSpec((nb, D_MODEL), lambda j: (0, j)),
        compiler_params=_cparams(1),
        name="ada",
    )(c, w_ada, b_ada.reshape(1, -1))


def _store_kv(kvb, kslab, wk_ref, wvt_ref, k_ref, vt_ref):
    kk = jnp.dot(kvb, wk_ref[...], preferred_element_type=F32)
    for hd in range(N_HEADS):
        sl = slice(hd * LANES, (hd + 1) * LANES)
        k_ref[:, sl] = (kk[:, sl] + kslab).astype(BF16)
    vt = lax.dot_general(wvt_ref[...], kvb, (((1,), (1,)), ((), ())), preferred_element_type=F32)
    row = lax.broadcasted_iota(jnp.int32, (HEAD_SLABS, 1), 0)
    vt_ref[...] = (vt + (row % LANES == ONE_LANE).astype(F32)).astype(BF16)


def _premix_kernel(x_ref, mod_ref, cos_ref, sin_ref, hist_ref, gpre_ref, gqa_ref, gkv_ref, pscale_ref,
                   win_ref, wq_ref, wk_ref, wvt_ref, wpool_ref,
                   q_ref, k_ref, vt_ref, kv_ref, kpe_ref, pooled_ref, poolnew_ref,
                   ubuf, *, tm, pos0):
    t = pl.program_id(1)
    x = x_ref[...]
    h = (_rms(x, gpre_ref[...]) * (1.0 + mod_ref[1:2, :]) + mod_ref[0:1, :]).astype(BF16)
    z = jnp.dot(h, win_ref[...], preferred_element_type=F32)
    cosv = cos_ref[...]
    sinv = sin_ref[...]

    qan = _rms(z[:, :Q_LORA], gqa_ref[...]).astype(BF16)
    qq = jnp.dot(qan, wq_ref[...], preferred_element_type=F32)
    for hd in range(N_HEADS):
        a = qq[:, hd * LANES:(hd + 1) * LANES]
        b = qq[:, HEAD_SLABS + hd * LANES:HEAD_SLABS + (hd + 1) * LANES]
        q_ref[:, hd * LANES:(hd + 1) * LANES] = ((a * cosv + b * sinv) * Q_SCALE).astype(BF16)

    kvn = _rms(z[:, Q_LORA:U_OFF], gkv_ref[...])
    kv_ref[...] = kvn
    kslab = z[:, KPE_OFF:KPE_OFF + LANES] * cosv + z[:, KPE_OFF + LANES:KPE_OFF + 2 * LANES] * sinv
    kpe_ref[...] = kslab
    _store_kv(kvn.astype(BF16), kslab, wk_ref, wvt_ref, k_ref, vt_ref)

    @pl.when(t == 0)
    def _():
        ubuf[0:HIST_ROWS, :] = hist_ref[...]

    @pl.when(t > 0)
    def _():
        ubuf[0:HIST_ROWS, :] = ubuf[tm:tm + HIST_ROWS, :]

    ubuf[HIST_ROWS:HIST_ROWS + tm, :] = z[:, U_OFF:KPE_OFF]
    pos = pos0 + t * tm + lax.broadcasted_iota(jnp.int32, (tm, 1), 0)
    for g, w in enumerate(POOL_WINDOWS):
        sl = slice(g * POOL_GROUP_DIM, (g + 1) * POOL_GROUP_DIM)
        u = ubuf[HIST_ROWS:HIST_ROWS + tm, sl]
        acc = u
        for j in range(1, w):
            acc = acc + ubuf[HIST_ROWS - j:HIST_ROWS - j + tm, sl]
        cnt = jnp.minimum(pos + 1, w).astype(F32)
        d = (acc / cnt - u).astype(BF16)
        y = jnp.dot(d, wpool_ref[g], preferred_element_type=F32) * pscale_ref[:, sl]
        pooled_ref[:, sl] = y.astype(BF16)
    poolnew_ref[...] = ubuf[tm:tm + HIST_ROWS, :]


def _premix(x, mod, cos_t, sin_t, hist, w, *, tm, pos0):
    nb, nt, _ = x.shape
    grid = (nb, nt // tm)
    tok = lambda width: pl.BlockSpec((None, tm, width), lambda b, t: (b, t, 0))
    full = lambda a: pl.BlockSpec(a.shape, lambda b, t: (0,) * a.ndim)
    tab = pl.BlockSpec((tm, LANES), lambda b, t: (t, 0))
    perb = lambda rows, width: pl.BlockSpec((None, rows, width), lambda b, t: (b, 0, 0))
    outs = [
        jax.ShapeDtypeStruct((nb, nt, HEAD_SLABS), BF16),
        jax.ShapeDtypeStruct((nb, nt, HEAD_SLABS), BF16),
        jax.ShapeDtypeStruct((nb, HEAD_SLABS, nt), BF16),
        jax.ShapeDtypeStruct((nb, nt, KV_LORA), F32),
        jax.ShapeDtypeStruct((nb, nt, LANES), F32),
        jax.ShapeDtypeStruct((nb, nt, POOL_WIDTH), BF16),
        jax.ShapeDtypeStruct((nb, HIST_ROWS, POOL_WIDTH), F32),
    ]
    return pl.pallas_call(
        functools.partial(_premix_kernel, tm=tm, pos0=pos0),
        out_shape=outs,
        grid=grid,
        in_specs=[tok(D_MODEL), perb(N_MOD, D_MODEL), tab, tab, perb(HIST_ROWS, POOL_WIDTH),
                  full(w["g_pre_mix"]), full(w["g_q_a"]), full(w["g_kv_a"]), full(w["pool_scale"]),
                  full(w["w_in"]), full(w["w_q"]), full(w["w_k"]), full(w["w_vt"]), full(w["w_pool"])],
        out_specs=[tok(HEAD_SLABS), tok(HEAD_SLABS),
                   pl.BlockSpec((None, HEAD_SLABS, tm), lambda b, t: (b, 0, t)),
                   tok(KV_LORA), tok(LANES), tok(POOL_WIDTH), perb(HIST_ROWS, POOL_WIDTH)],
        scratch_shapes=[pltpu.VMEM((HIST_ROWS + tm, POOL_WIDTH), F32)],
        compiler_params=_cparams(2),
        name="premix",
    )(x, mod, cos_t, sin_t, hist, w["g_pre_mix"], w["g_q_a"], w["g_kv_a"], w["pool_scale"],
      w["w_in"], w["w_q"], w["w_k"], w["w_vt"], w["w_pool"])


def _kvproj_kernel(kv_ref, kpe_ref, wk_ref, wvt_ref, k_ref, vt_ref):
    _store_kv(kv_ref[...].astype(BF16), kpe_ref[...], wk_ref, wvt_ref, k_ref, vt_ref)


def _kvproj(kv, kpe_slab, w_k, w_vt, *, tm):
    nb, nt, _ = kv.shape
    tok = lambda width: pl.BlockSpec((None, tm, width), lambda b, t: (b, t, 0))
    full = lambda a: pl.BlockSpec(a.shape, lambda b, t: (0,) * a.ndim)
    return pl.pallas_call(
        _kvproj_kernel,
        out_shape=[jax.ShapeDtypeStruct((nb, nt, HEAD_SLABS), BF16),
                   jax.ShapeDtypeStruct((nb, HEAD_SLABS, nt), BF16)],
        grid=(nb, nt // tm),
        in_specs=[tok(KV_LORA), tok(LANES), full(w_k), full(w_vt)],
        out_specs=[tok(HEAD_SLABS), pl.BlockSpec((None, HEAD_SLABS, tm), lambda b, t: (b, 0, t))],
        compiler_params=_cparams(2),
        name="kvproj",
    )(kv, kpe_slab, w_k, w_vt)


def _attn_kernel(q_ref, k_ref, vt_ref, o_ref, m_ref, acc_ref, *, tq, tk, nk, causal, kv_len):
    qi = pl.program_id(1)
    ki = pl.program_id(2)
    last = ((qi + 1) * tq - 1) // tk if causal else nk - 1
    need_len_mask = kv_len < nk * tk

    @pl.when(ki == 0)
    def _():
        m_ref[...] = jnp.full(m_ref.shape, NEG_INF, F32)
        acc_ref[...] = jnp.zeros(acc_ref.shape, F32)

    def step(masked, nkeys=tk):
        if masked:
            kpos = ki * tk + lax.broadcasted_iota(jnp.int32, (nkeys, tq), 0)
            vis = None
            if causal:
                qpos = qi * tq + lax.broadcasted_iota(jnp.int32, (nkeys, tq), 1)
                vis = (kpos // CHUNK) <= (qpos // CHUNK)
            if need_len_mask:
                lm = kpos < kv_len
                vis = lm if vis is None else (vis & lm)

        def scores(hd):
            sl = slice(hd * LANES, (hd + 1) * LANES)
            return lax.dot_general(k_ref[:nkeys, sl], q_ref[:, sl], (((1,), (1,)), ((), ())),
                                   preferred_element_type=F32)

        def accumulate(hd, alpha, p):
            sl = slice(hd * LANES, (hd + 1) * LANES)
            acc_ref[hd] = alpha * acc_ref[hd] + jnp.dot(vt_ref[sl, :nkeys], p, preferred_element_type=F32)

        s_next = scores(0)
        pending = None
        for hd in range(N_HEADS):
            s = s_next
            if hd + 1 < N_HEADS:
                s_next = scores(hd + 1)
            if pending is not None:
                accumulate(*pending)
            if masked:
                s = jnp.where(vis, s, NEG_INF)
            m_prev = m_ref[hd:hd + 1, :]
            m_new = jnp.maximum(m_prev, jnp.max(s, axis=0, keepdims=True))
            m_ref[hd:hd + 1, :] = m_new
            pending = (hd, jnp.exp2(m_prev - m_new), jnp.exp2(s - m_new).astype(BF16))
        accumulate(*pending)

    if causal and tk == 2 * tq and not need_len_mask:
        @pl.when(ki < last)
        def _():
            step(False)

        @pl.when((ki == last) & (qi % 2 == 0))
        def _():
            step(True, tq)

        @pl.when((ki == last) & (qi % 2 == 1))
        def _():
            step(True)
    elif causal or need_len_mask:
        @pl.when(ki < last)
        def _():
            step(False)

        @pl.when(ki == last)
        def _():
            step(True)
    else:
        step(False)

    @pl.when(ki == nk - 1)
    def _():
        for hd in range(N_HEADS):
            acc = acc_ref[hd]
            out_t = acc / acc[ONE_LANE:ONE_LANE + 1, :]
            o_ref[:, hd * LANES:(hd + 1) * LANES] = out_t.T.astype(BF16)


def _attention(q, k, vt, *, tq, tk, causal, kv_len):
    nb, nq_tot, _ = q.shape
    nk = k.shape[1] // tk
    nq = nq_tot // tq
    if causal:
        last = lambda i: ((i + 1) * tq - 1) // tk
        kmap = lambda b, i, j: (b, jnp.minimum(j, last(i)), 0)
        vmap = lambda b, i, j: (b, 0, jnp.minimum(j, last(i)))
    else:
        kmap = lambda b, i, j: (b, j, 0)
        vmap = lambda b, i, j: (b, 0, j)
    return pl.pallas_call(
        functools.partial(_attn_kernel, tq=tq, tk=tk, nk=nk, causal=causal, kv_len=kv_len),
        out_shape=jax.ShapeDtypeStruct((nb, nq_tot, HEAD_SLABS), BF16),
        grid=(nb, nq, nk),
        in_specs=[pl.BlockSpec((None, tq, HEAD_SLABS), lambda b, i, j: (b, i, 0)),
                  pl.BlockSpec((None, tk, HEAD_SLABS), kmap),
                  pl.BlockSpec((None, HEAD_SLABS, tk), vmap)],
        out_specs=pl.BlockSpec((None, tq, HEAD_SLABS), lambda b, i, j: (b, i, 0)),
        scratch_shapes=[pltpu.VMEM((N_HEADS, tq), F32), pltpu.VMEM((N_HEADS, LANES, tq), F32)],
        compiler_params=_cparams(3),
        name="attention",
    )(q, k, vt)


def _postmix_kernel(*refs, tm, cap, first_call):
    (attn_ref, pooled_ref, x_ref, mod_ref, cnt0_ref, gpost_ref, gffn_ref, woa_ref, wop_ref, wr_ref, br_ref) = refs[:11]
    rest = refs[11:] if first_call else refs[12:]
    x1_ref, ri_ref, rg_ref, cnt_ref, xs_ref, carry, ltri, hbuf, dvm, dsm, dsem, csem = rest
    step = pl.program_id(0) * pl.num_programs(1) + pl.program_id(1)
    n_steps = pl.num_programs(0) * pl.num_programs(1)
    slot = step % 2

    @pl.when(step == 0)
    def _():
        carry[...] = cnt0_ref[...]
        r = lax.broadcasted_iota(jnp.int32, (tm, tm), 0)
        c = lax.broadcasted_iota(jnp.int32, (tm, tm), 1)
        ltri[...] = (r > c).astype(BF16)

    n_groups = tm // SUBLANES
    n_chunks = min(8, n_groups)
    per_chunk = n_groups // n_chunks

    def issue_rows(buf, j0, j1):
        def issue(j, c):
            for s in range(SUBLANES):
                r = j * SUBLANES + s
                for kk in range(TOP_K):
                    pltpu.async_copy(hbuf.at[buf, r], xs_ref.at[dsm[kk, r]], dsem.at[buf], priority=1)
            return c

        lax.fori_loop(j0, j1, issue, 0)

    chunks = iter(range(n_chunks))

    def issue_prev_chunk():
        c = next(chunks, None)
        if c is not None:
            @pl.when(step >= 1)
            def _():
                issue_rows(1 - slot, c * per_chunk, (c + 1) * per_chunk)

    issue_prev_chunk()
    mix = (jnp.dot(attn_ref[...], woa_ref[...], preferred_element_type=F32)
           + jnp.dot(pooled_ref[...], wop_ref[...], preferred_element_type=F32))
    issue_prev_chunk()
    x1 = x_ref[...] + mod_ref[2:3, :] * _rms(mix, gpost_ref[...])
    x1_ref[...] = x1
    issue_prev_chunk()
    h2 = _rms(x1, gffn_ref[...]) * (1.0 + mod_ref[4:5, :]) + mod_ref[3:4, :]

    lane = lax.broadcasted_iota(jnp.int32, (tm, LANES), 1).astype(F32)
    logits = jnp.dot(h2.astype(BF16), wr_ref[...], preferred_element_type=F32) + br_ref[...]
    logits = jnp.where(lane < N_EXPERTS, logits, NEG_INF)
    issue_prev_chunk()
    sel = jnp.zeros((tm, LANES), F32)
    ids, vals, hots = [], [], []
    for _ in range(TOP_K):
        mk = jnp.max(logits, axis=1, keepdims=True)
        idx = jnp.min(jnp.where(logits == mk, lane, float(LANES)), axis=1, keepdims=True)
        hot = lane == idx
        logits = jnp.where(hot, NEG_INF, logits)
        sel = sel + hot.astype(F32)
        ids.append(idx)
        vals.append(mk)
        hots.append(hot)
        issue_prev_chunk()
    assert next(chunks, None) is None
    ex = [jnp.exp(vk - vals[0]) for vk in vals]
    denom = ex[0] + ex[1] + ex[2] + ex[3]

    before = jnp.dot(ltri[...], sel.astype(BF16), preferred_element_type=F32) + carry[0:1, :]
    ri = jnp.zeros((tm, LANES), F32)
    rg = jnp.zeros((tm, LANES), F32)
    for kk in range(TOP_K):
        rank = jnp.sum(jnp.where(hots[kk], before, 0.0), axis=1, keepdims=True)
        ri = jnp.where(lane == kk, ids[kk] * float(cap) + rank, ri)
        rg = jnp.where(lane == kk, ex[kk] / denom, rg)
    ri_ref[...] = ri.astype(jnp.int32)
    rg_ref[...] = rg
    carry[0:1, :] = carry[0:1, :] + jnp.sum(sel, axis=0, keepdims=True)
    cnt_ref[...] = carry[...]

    def drain(buf):
        for _ in range(TOP_K):
            pltpu.make_async_copy(xs_ref.at[pl.ds(0, tm)], xs_ref.at[pl.ds(0, tm)], dsem.at[buf]).wait()

    @pl.when(step >= 2)
    def _():
        drain(slot)

    hbuf[slot] = h2.reshape(tm, *ROW_TILE)
    dvm[...] = ri.T[0:SUBLANES, :].astype(jnp.int32)
    to_smem = pltpu.make_async_copy(dvm, dsm, csem)
    to_smem.start()
    to_smem.wait()

    @pl.when(step == n_steps - 1)
    def _():
        issue_rows(slot, 0, n_groups)
        drain(slot)

        @pl.when(step >= 1)
        def _():
            drain(1 - slot)


def _postmix(attn, pooled, x, mod, cnt0, w, xs, *, tm, cap):
    nb, nt, _ = x.shape
    first_call = xs is None
    tok = lambda width: pl.BlockSpec((None, tm, width), lambda b, t: (b, t, 0))
    full = lambda a: pl.BlockSpec(a.shape, lambda b, t: (0,) * a.ndim)
    outs = [
        jax.ShapeDtypeStruct((nb, nt, D_MODEL), F32),
        jax.ShapeDtypeStruct((nb, nt, LANES), jnp.int32),
        jax.ShapeDtypeStruct((nb, nt, LANES), F32),
        jax.ShapeDtypeStruct((8, LANES), F32),
        jax.ShapeDtypeStruct((N_EXPERTS * cap,) + ROW_TILE, F32),
    ]
    in_specs = [tok(HEAD_SLABS), tok(POOL_WIDTH), tok(D_MODEL),
                pl.BlockSpec((None, N_MOD, D_MODEL), lambda b, t: (b, 0, 0)),
                full(cnt0), full(w["g_post_mix"]), full(w["g_pre_ffn"]),
                full(w["w_o_attn"]), full(w["w_o_pool"]), full(w["w_router"]), full(w["b_router"])]
    args = [attn, pooled, x, mod, cnt0, w["g_post_mix"], w["g_pre_ffn"],
            w["w_o_attn"], w["w_o_pool"], w["w_router"], w["b_router"]]
    if not first_call:
        in_specs.append(pl.BlockSpec(memory_space=pl.ANY))
        args.append(xs)
    return pl.pallas_call(
        functools.partial(_postmix_kernel, tm=tm, cap=cap, first_call=first_call),
        out_shape=outs,
        grid=(nb, nt // tm),
        in_specs=in_specs,
        out_specs=[tok(D_MODEL), tok(LANES), tok(LANES), pl.BlockSpec((8, LANES), lambda b, t: (0, 0)),
                   pl.BlockSpec(memory_space=pl.ANY)],
        scratch_shapes=[pltpu.VMEM((8, LANES), F32), pltpu.VMEM((tm, tm), BF16),
                        pltpu.VMEM((2, tm) + ROW_TILE, F32), pltpu.VMEM((SUBLANES, tm), jnp.int32),
                        pltpu.SMEM((SUBLANES, tm), jnp.int32),
                        pltpu.SemaphoreType.DMA((2,)), pltpu.SemaphoreType.DMA(())],
        input_output_aliases={} if first_call else {11: 4},
        compiler_params=_cparams(2),
        name="postmix",
    )(*args)


def _zero_pads_kernel(zfrom_ref, xs_in_ref, xs_ref, zbuf, zsem):
    del xs_in_ref
    zbuf[...] = jnp.zeros(zbuf.shape, F32)
    copies = [pltpu.make_async_copy(zbuf, xs_ref.at[pl.ds(zfrom_ref[e], MOE_BLOCK)], zsem)
              for e in range(N_EXPERTS)]
    for cp in copies:
        cp.start()
    for cp in copies:
        cp.wait()


def _zero_pads(zfrom, xs):
    return pl.pallas_call(
        _zero_pads_kernel,
        out_shape=jax.ShapeDtypeStruct(xs.shape, xs.dtype),
        grid_spec=pltpu.PrefetchScalarGridSpec(
            num_scalar_prefetch=1,
            grid=(1,),
            in_specs=[pl.BlockSpec(memory_space=pl.ANY)],
            out_specs=pl.BlockSpec(memory_space=pl.ANY),
            scratch_shapes=[pltpu.VMEM((MOE_BLOCK,) + ROW_TILE, F32), pltpu.SemaphoreType.DMA(())],
        ),
        input_output_aliases={1: 0},
        compiler_params=_cparams(1),
        name="zero_pads",
    )(zfrom, xs)


def _expert_kernel(be_ref, bi_ref, nu_ref, xs_ref, wgu_ref, bgu_ref, wd_ref, bd_ref, ys_ref, wgu_b, wd_b):
    del bi_ref
    i = pl.program_id(0)

    @pl.when(i < nu_ref[0])
    def _():
        @pl.when((i == 0) | (be_ref[i] != be_ref[jnp.maximum(i - 1, 0)]))
        def _():
            wgu_b[...] = wgu_ref[...].astype(BF16)
            wd_b[...] = wd_ref[...].astype(BF16)

        x = xs_ref[...].reshape(MOE_BLOCK, D_MODEL).astype(BF16)
        gu = jnp.dot(x, wgu_b[...], preferred_element_type=F32) + bgu_ref[...]
        g = jnp.minimum(gu[:, :D_FF], SWIGLU_LIMIT)
        u = jnp.clip(gu[:, D_FF:], -SWIGLU_LIMIT, SWIGLU_LIMIT)
        a = (u + 1.0) * (g * jax.nn.sigmoid(SWIGLU_ALPHA * g))
        y = jnp.dot(a.astype(BF16), wd_b[...], preferred_element_type=F32) + bd_ref[...]
        ys_ref[...] = y.reshape(MOE_BLOCK, *ROW_TILE)


def _experts(block_e, block_idx, n_used, xs, w):
    nblk = block_e.shape[0]
    row = lambda i, be, bi, nu: (bi[i], 0, 0)
    per_e = lambda i, be, bi, nu: (be[i], 0, 0)
    return pl.pallas_call(
        _expert_kernel,
        out_shape=jax.ShapeDtypeStruct(xs.shape, F32),
        grid_spec=pltpu.PrefetchScalarGridSpec(
            num_scalar_prefetch=3,
            grid=(nblk,),
            in_specs=[pl.BlockSpec((MOE_BLOCK,) + ROW_TILE, row),
                      pl.BlockSpec((None, D_MODEL, 2 * D_FF), per_e),
                      pl.BlockSpec((None, 1, 2 * D_FF), per_e),
                      pl.BlockSpec((None, D_FF, D_MODEL), per_e),
                      pl.BlockSpec((None, 1, D_MODEL), per_e)],
            out_specs=pl.BlockSpec((MOE_BLOCK,) + ROW_TILE, row),
            scratch_shapes=[pltpu.VMEM((D_MODEL, 2 * D_FF), BF16), pltpu.VMEM((D_FF, D_MODEL), BF16)],
        ),
        compiler_params=_cparams(1),
        name="experts",
    )(block_e, block_idx, n_used, xs, w["w_gu"], w["b_gu"], w["w_down"], w["b_down"])


def _combine_kernel(dcur_ref, dnext_ref, x1_ref, rg_ref, mod_ref, gpost_ref, ys_ref, o_ref, gbuf, sem, *, tm, n):
    i = pl.program_id(0)
    slot = i % 2

    def gather(dest_ref, sl):
        def issue(j, carry):
            for s in range(SUBLANES):
                r = j * SUBLANES + s
                for kk in range(TOP_K):
                    d = dest_ref[0, 0, j * (SUBLANES * TOP_K) + s * TOP_K + kk]
                    pltpu.make_async_copy(ys_ref.at[d], gbuf.at[sl, kk, r], sem.at[sl]).start()
            return carry

        lax.fori_loop(0, tm // SUBLANES, issue, 0)

    @pl.when(i == 0)
    def _():
        gather(dcur_ref, 0)

    @pl.when(i + 1 < n)
    def _():
        gather(dnext_ref, 1 - slot)

    for kk in range(TOP_K):
        pltpu.make_async_copy(gbuf.at[1 - slot, kk], gbuf.at[slot, kk], sem.at[slot]).wait()

    rg = rg_ref[...]
    y = rg[:, 0:1] * gbuf[slot, 0].reshape(tm, D_MODEL)
    for kk in range(1, TOP_K):
        y = y + rg[:, kk:kk + 1] * gbuf[slot, kk].reshape(tm, D_MODEL)
    o_ref[...] = x1_ref[...] + mod_ref[5:6, :] * _rms(y, gpost_ref[...])


def _combine(dest, x1, rg, mod, g_post, ys, *, tm):
    nb, nt, _ = x1.shape
    ntile = nt // tm
    n = nb * ntile
    tok = lambda width: pl.BlockSpec((tm, width), lambda i: (i, 0))
    dest = dest.reshape(n, 1, tm * TOP_K)
    dspec = lambda f: pl.BlockSpec((1, 1, tm * TOP_K), lambda i: (f(i), 0, 0), memory_space=pltpu.SMEM)
    out = pl.pallas_call(
        functools.partial(_combine_kernel, tm=tm, n=n),
        out_shape=jax.ShapeDtypeStruct((nb * nt, D_MODEL), F32),
        grid=(n,),
        in_specs=[dspec(lambda i: i), dspec(lambda i: jnp.minimum(i + 1, n - 1)),
                  tok(D_MODEL), tok(LANES),
                  pl.BlockSpec((None, N_MOD, D_MODEL), lambda i: (i // ntile, 0, 0)),
                  pl.BlockSpec(g_post.shape, lambda i: (0, 0)),
                  pl.BlockSpec(memory_space=pl.ANY)],
        out_specs=tok(D_MODEL),
        scratch_shapes=[pltpu.VMEM((2, TOP_K, tm) + ROW_TILE, F32),
                        pltpu.SemaphoreType.DMA((2,))],
        compiler_params=_cparams(1),
        name="combine",
    )(dest, dest, x1.reshape(nb * nt, D_MODEL), rg.reshape(nb * nt, LANES), mod, g_post, ys)
    return out.reshape(x1.shape)


def _rot_swap(w):
    half = QK_ROPE // 2
    return jnp.concatenate([-w[..., half:], w[..., :half]], axis=-1)


def _prep_weights(w_in, g_q_a, w_q_b, g_kv_a, w_uk, w_uv, w_pool, pool_scale, w_o, g_pre_mix, g_post_mix,
                  g_pre_ffn, w_router, b_router, w_gu, b_gu, w_down, b_down, g_post_ffn):
    row = lambda a: a.reshape(1, -1).astype(F32)
    w_kpe = w_in[:, U_OFF:U_OFF + QK_ROPE]
    zeros = lambda *s: jnp.zeros(s, F32)
    d = D_MODEL
    slab = lambda a: jnp.concatenate([zeros(d, QK_NOPE), a, zeros(d, LANES - QK_HEAD)], axis=1)
    w_in_ext = jnp.concatenate([w_in[:, :U_OFF], w_in[:, U_OFF + QK_ROPE:], slab(w_kpe), slab(_rot_swap(w_kpe))],
                               axis=1)
    pad_q = zeros(Q_LORA, N_HEADS, LANES - QK_HEAD)
    wq_plain = jnp.concatenate([w_q_b, pad_q], axis=2).reshape(Q_LORA, HEAD_SLABS)
    wq_swap = jnp.concatenate([zeros(Q_LORA, N_HEADS, QK_NOPE), _rot_swap(w_q_b[..., QK_NOPE:]), pad_q],
                              axis=2).reshape(Q_LORA, HEAD_SLABS)
    pad_kv = zeros(KV_LORA, N_HEADS, LANES - QK_NOPE)
    wk = jnp.concatenate([w_uk, pad_kv], axis=2).reshape(KV_LORA, HEAD_SLABS)
    wv = jnp.concatenate([w_uv, pad_kv], axis=2).reshape(KV_LORA, HEAD_SLABS)
    mla_w = N_HEADS * V_HEAD
    woa = jnp.concatenate([w_o[:mla_w].reshape(N_HEADS, V_HEAD, d), zeros(N_HEADS, LANES - V_HEAD, d)],
                          axis=1).reshape(HEAD_SLABS, d)
    return {
        "g_pre_mix": row(g_pre_mix), "g_q_a": row(g_q_a), "g_kv_a": row(g_kv_a), "pool_scale": row(pool_scale),
        "w_in": w_in_ext.astype(BF16),
        "w_q": jnp.concatenate([wq_plain, wq_swap], axis=1).astype(BF16),
        "w_k": wk.astype(BF16), "w_vt": wv.T.astype(BF16),
        "w_pool": w_pool.astype(BF16),
        "g_post_mix": row(g_post_mix), "g_pre_ffn": row(g_pre_ffn), "g_post_ffn": row(g_post_ffn),
        "w_o_attn": woa.astype(BF16), "w_o_pool": w_o[mla_w:].astype(BF16),
        "w_router": jnp.pad(w_router, ((0, 0), (0, LANES - N_EXPERTS))).astype(BF16),
        "b_router": jnp.pad(b_router, (0, LANES - N_EXPERTS)).reshape(1, LANES).astype(F32),
        "w_gu": w_gu, "b_gu": b_gu.reshape(N_EXPERTS, 1, 2 * D_FF).astype(F32),
        "w_down": w_down, "b_down": b_down.reshape(N_EXPERTS, 1, D_MODEL).astype(F32),
    }


def _rope_tables(pos):
    half = QK_ROPE // 2
    inv = ROPE_THETA ** (-jnp.arange(half, dtype=F32) / half)
    ang = pos.astype(F32)[:, None] * inv[None, :]
    cos, sin = jnp.cos(ang), jnp.sin(ang)
    n = pos.shape[0]
    cos_t = jnp.concatenate([jnp.ones((n, QK_NOPE), F32), cos, cos, jnp.zeros((n, LANES - QK_HEAD), F32)], axis=1)
    sin_t = jnp.concatenate([jnp.zeros((n, QK_NOPE), F32), sin, sin, jnp.zeros((n, LANES - QK_HEAD), F32)], axis=1)
    return cos_t, sin_t


def _tile(n, pref):
    return pref if n % pref == 0 else n


def _mixer_path(x, mod, pos0, hist, cache, w, cnt0, xs, cap):
    nb, nt, _ = x.shape
    tm = _tile(nt, 512)
    cos_t, sin_t = _rope_tables(pos0 + jnp.arange(nt, dtype=jnp.int32))
    q, k, vt, kv_new, kslab, pooled, pool_tail = _premix(x, mod, cos_t, sin_t, hist, w, tm=tm, pos0=pos0)
    if cache is None:
        attn = _attention(q, k, vt, tq=tm, tk=_tile(nt, 2 * tm), causal=True, kv_len=nt)
    else:
        ckv, ckpe = cache
        past = ckv.shape[1]
        kv_len = past + nt
        tk = -(-kv_len // 256) * 256
        ckpe_slab = jnp.pad(ckpe, ((0, 0), (0, 0), (QK_NOPE, LANES - QK_HEAD)))
        lat_all = jnp.concatenate([ckv, kv_new, jnp.zeros((nb, tk - kv_len, KV_LORA), F32)], axis=1)
        kpe_all = jnp.concatenate([ckpe_slab, kslab, jnp.zeros((nb, tk - kv_len, LANES), F32)], axis=1)
        k_all, vt_all = _kvproj(lat_all, kpe_all, w["w_k"], w["w_vt"], tm=_tile(tk, 768))
        tq = -(-nt // LANES) * LANES
        q_pad = jnp.pad(q, ((0, 0), (0, tq - nt), (0, 0)))
        attn = _attention(q_pad, k_all, vt_all, tq=tq, tk=tk, causal=False, kv_len=kv_len)[:, :nt]
    x1, ri, rg, cnt, xs = _postmix(attn, pooled, x, mod, cnt0, w, xs, tm=tm, cap=cap)
    return x1, xs, ri, rg, cnt, kv_new, kslab[..., QK_NOPE:QK_HEAD], pool_tail[:, 1:]


def kernel(x_prompt, x_sample, c_prompt, c_sample, cache_kv_latent, cache_k_rope, state_pool, w_ada, b_ada,
           g_pre_mix, w_in, g_q_a, w_q_b, g_kv_a, w_uk, w_uv, w_pool, pool_scale, w_o, g_post_mix, g_pre_ffn,
           w_router, b_router, w_gu, b_gu, w_down, b_down, g_post_ffn):
    assert w_ada.shape[0] == 1, "single-layer step"
    bp, sp, _ = x_prompt.shape
    bs, ss, _ = x_sample.shape
    past = cache_kv_latent.shape[2]
    w = _prep_weights(w_in[0], g_q_a[0], w_q_b[0], g_kv_a[0], w_uk[0], w_uv[0], w_pool[0], pool_scale[0], w_o[0],
                      g_pre_mix[0], g_post_mix[0], g_pre_ffn[0], w_router[0], b_router[0], w_gu[0], b_gu[0],
                      w_down[0], b_down[0], g_post_ffn[0])

    mod = _ada(jnp.concatenate([c_prompt, c_sample], axis=0), w_ada[0], b_ada[0])
    mod = mod.reshape(bp + bs, N_MOD, D_MODEL)
    mod_p, mod_s = mod[:bp], mod[bp:]

    hist_p = jnp.zeros((bp, HIST_ROWS, POOL_WIDTH), F32)
    hist_s = jnp.pad(state_pool[0], ((0, 0), (1, 0), (0, 0)))
    n_tok = bp * sp + bs * ss
    cap = (n_tok // MOE_BLOCK + 1) * MOE_BLOCK
    cnt0 = jnp.zeros((8, LANES), F32)
    x1p, xs, rip, rgp, cntp, kv_p, kpe_p, pool_p = _mixer_path(x_prompt, mod_p, 0, hist_p, None, w, cnt0, None, cap)
    x1s, xs, ris, rgs, cnts, kv_s, kpe_s, pool_s = _mixer_path(
        x_sample, mod_s, past, hist_s, (cache_kv_latent[0], cache_k_rope[0]), w, cntp, xs, cap)

    counts = cnts[0, :N_EXPERTS].astype(jnp.int32)
    blocks = (counts + MOE_BLOCK - 1) // MOE_BLOCK
    blk_end = jnp.cumsum(blocks)
    n_blocks = -(-(n_tok * TOP_K) // MOE_BLOCK) + N_EXPERTS
    n_used = blk_end[-1:].astype(jnp.int32)
    grid_i = jnp.minimum(jnp.arange(n_blocks, dtype=jnp.int32), n_used[0] - 1)
    block_e = jnp.sum((blk_end[None, :] <= grid_i[:, None]).astype(jnp.int32), axis=1)
    blk_start = blk_end - blocks
    block_idx = (block_e * (cap // MOE_BLOCK) + grid_i
                 - jnp.sum(jnp.where(block_e[:, None] == jnp.arange(N_EXPERTS, dtype=jnp.int32), blk_start, 0), axis=1))
    zfrom = (jnp.arange(N_EXPERTS, dtype=jnp.int32) * cap + counts).astype(jnp.int32)
    xs = _zero_pads(zfrom, xs)
    ys = _experts(block_e, block_idx.astype(jnp.int32), n_used, xs, w)
    dest_p = rip[..., :TOP_K]
    dest_s = ris[..., :TOP_K]
    y_p = _combine(dest_p, x1p, rgp, mod_p, w["g_post_ffn"], ys, tm=_tile(sp, 512))
    y_s = _combine(dest_s, x1s, rgs, mod_s, w["g_post_ffn"], ys, tm=_tile(ss, 256))
    return (y_p, y_s, kv_p[None], kpe_p[None], pool_p[None], kv_s[None], kpe_s[None], pool_s[None])
```

```python
import functools

import jax
import jax.numpy as jnp
from jax import lax
from jax.experimental import pallas as pl
from jax.experimental.pallas import tpu as pltpu

F32 = jnp.float32
BF16 = jnp.bfloat16

D_MODEL = 1024
CHUNK = 64
N_HEADS = 8
QK_NOPE = 64
QK_ROPE = 32
QK_HEAD = QK_NOPE + QK_ROPE
V_HEAD = 64
Q_LORA = 384
KV_LORA = 256
ROPE_THETA = 10000.0
POOL_WINDOWS = (2, 4, 8, 16)
POOL_GROUP_DIM = 128
POOL_WIDTH = POOL_GROUP_DIM * len(POOL_WINDOWS)
POOL_HIST = max(POOL_WINDOWS) - 1
HIST_ROWS = POOL_HIST + 1
N_EXPERTS = 32
TOP_K = 4
D_FF = 1024
SWIGLU_LIMIT = 7.0
SWIGLU_ALPHA = 1.702
N_MOD = 6
EPS = 1e-6

LANES = 128
HEAD_SLABS = N_HEADS * LANES
ONE_LANE = V_HEAD
IN_EXT = Q_LORA + KV_LORA + POOL_WIDTH + 2 * LANES
U_OFF = Q_LORA + KV_LORA
KPE_OFF = U_OFF + POOL_WIDTH
SM_SCALE = QK_HEAD ** -0.5
LOG2_E = 1.4426950408889634
Q_SCALE = SM_SCALE * LOG2_E
NEG_INF = float("-inf")

SUBLANES = 8
ROW_TILE = (SUBLANES, D_MODEL // SUBLANES)
MOE_BLOCK = 512
VMEM_LIMIT = 56 * 1024 * 1024


def _cparams(n_axes, vmem=VMEM_LIMIT):
    return pltpu.CompilerParams(dimension_semantics=("arbitrary",) * n_axes, vmem_limit_bytes=vmem)


def _rms(x, g):
    return x * lax.rsqrt(jnp.mean(x * x, axis=-1, keepdims=True) + EPS) * g


def _ada_kernel(c_ref, w_ref, b_ref, o_ref):
    c = c_ref[...]
    s = (c * jax.nn.sigmoid(c)).astype(BF16)
    o_ref[...] = jnp.dot(s, w_ref[...].astype(BF16), preferred_element_type=F32) + b_ref[...]


def _ada(c, w_ada, b_ada):
    nb = c.shape[0]
    return pl.pallas_call(
        _ada_kernel,
        out_shape=jax.ShapeDtypeStruct((nb, N_MOD * D_MODEL), F32),
        grid=(N_MOD,),
        in_specs=[pl.BlockSpec((nb, D_MODEL), lambda j: (0, 0)),
                  pl.BlockSpec((D_MODEL, D_MODEL), lambda j: (0, j)),
                  pl.BlockSpec((1, D_MODEL), lambda j: (0, j))],
        out_specs=pl.BlockSpec((nb, D_MODEL), lambda j: (0, j)),
        compiler_params=_cparams(1),
        name="ada",
    )(c, w_ada, b_ada.reshape(1, -1))


def _store_kv(kvb, kslab, wk_ref, wvt_ref, k_ref, vt_ref):
    kk = jnp.dot(kvb, wk_ref[...], preferred_element_type=F32)
    for hd in range(N_HEADS):
        sl = slice(hd * LANES, (hd + 1) * LANES)
        k_ref[:, sl] = (kk[:, sl] + kslab).astype(BF16)
    vt = lax.dot_general(wvt_ref[...], kvb, (((1,), (1,)), ((), ())), preferred_element_type=F32)
    row = lax.broadcasted_iota(jnp.int32, (HEAD_SLABS, 1), 0)
    vt_ref[...] = (vt + (row % LANES == ONE_LANE).astype(F32)).astype(BF16)


def _premix_kernel(x_ref, mod_ref, cos_ref, sin_ref, hist_ref, gpre_ref, gqa_ref, gkv_ref, pscale_ref,
                   win_ref, wq_ref, wk_ref, wvt_ref, wpool_ref,
                   q_ref, k_ref, vt_ref, kv_ref, kpe_ref, pooled_ref, poolnew_ref,
                   ubuf, *, tm, pos0):
    t = pl.program_id(1)
    x = x_ref[...]
    h = (_rms(x, gpre_ref[...]) * (1.0 + mod_ref[1:2, :]) + mod_ref[0:1, :]).astype(BF16)
    z = jnp.dot(h, win_ref[...], preferred_element_type=F32)
    cosv = cos_ref[...]
    sinv = sin_ref[...]

    qan = _rms(z[:, :Q_LORA], gqa_ref[...]).astype(BF16)
    qq = jnp.dot(qan, wq_ref[...], preferred_element_type=F32)
    for hd in range(N_HEADS):
        a = qq[:, hd * LANES:(hd + 1) * LANES]
        b = qq[:, HEAD_SLABS + hd * LANES:HEAD_SLABS + (hd + 1) * LANES]
        q_ref[:, hd * LANES:(hd + 1) * LANES] = ((a * cosv + b * sinv) * Q_SCALE).astype(BF16)

    kvn = _rms(z[:, Q_LORA:U_OFF], gkv_ref[...])
    kv_ref[...] = kvn
    kslab = z[:, KPE_OFF:KPE_OFF + LANES] * cosv + z[:, KPE_OFF + LANES:KPE_OFF + 2 * LANES] * sinv
    kpe_ref[...] = kslab
    _store_kv(kvn.astype(BF16), kslab, wk_ref, wvt_ref, k_ref, vt_ref)

    @pl.when(t == 0)
    def _():
        ubuf[0:HIST_ROWS, :] = hist_ref[...]

    @pl.when(t > 0)
    def _():
        ubuf[0:HIST_ROWS, :] = ubuf[tm:tm + HIST_ROWS, :]

    ubuf[HIST_ROWS:HIST_ROWS + tm, :] = z[:, U_OFF:KPE_OFF]
    pos = pos0 + t * tm + lax.broadcasted_iota(jnp.int32, (tm, 1), 0)
    for g, w in enumerate(POOL_WINDOWS):
        sl = slice(g * POOL_GROUP_DIM, (g + 1) * POOL_GROUP_DIM)
        u = ubuf[HIST_ROWS:HIST_ROWS + tm, sl]
        acc = u
        for j in range(1, w):
            acc = acc + ubuf[HIST_ROWS - j:HIST_ROWS - j + tm, sl]
        cnt = jnp.minimum(pos + 1, w).astype(F32)
        d = (acc / cnt - u).astype(BF16)
        y = jnp.dot(d, wpool_ref[g], preferred_element_type=F32) * pscale_ref[:, sl]
        pooled_ref[:, sl] = y.astype(BF16)
    poolnew_ref[...] = ubuf[tm:tm + HIST_ROWS, :]


def _premix(x, mod, cos_t, sin_t, hist, w, *, tm, pos0):
    nb, nt, _ = x.shape
    grid = (nb, nt // tm)
    tok = lambda width: pl.BlockSpec((None, tm, width), lambda b, t: (b, t, 0))
    full = lambda a: pl.BlockSpec(a.shape, lambda b, t: (0,) * a.ndim)
    tab = pl.BlockSpec((tm, LANES), lambda b, t: (t, 0))
    perb = lambda rows, width: pl.BlockSpec((None, rows, width), lambda b, t: (b, 0, 0))
    outs = [
        jax.ShapeDtypeStruct((nb, nt, HEAD_SLABS), BF16),
        jax.ShapeDtypeStruct((nb, nt, HEAD_SLABS), BF16),
        jax.ShapeDtypeStruct((nb, HEAD_SLABS, nt), BF16),
        jax.ShapeDtypeStruct((nb, nt, KV_LORA), F32),
        jax.ShapeDtypeStruct((nb, nt, LANES), F32),
        jax.ShapeDtypeStruct((nb, nt, POOL_WIDTH), BF16),
        jax.ShapeDtypeStruct((nb, HIST_ROWS, POOL_WIDTH), F32),
    ]
    return pl.pallas_call(
        functools.partial(_premix_kernel, tm=tm, pos0=pos0),
        out_shape=outs,
        grid=grid,
        in_specs=[tok(D_MODEL), perb(N_MOD, D_MODEL), tab, tab, perb(HIST_ROWS, POOL_WIDTH),
                  full(w["g_pre_mix"]), full(w["g_q_a"]), full(w["g_kv_a"]), full(w["pool_scale"]),
                  full(w["w_in"]), full(w["w_q"]), full(w["w_k"]), full(w["w_vt"]), full(w["w_pool"])],
        out_specs=[tok(HEAD_SLABS), tok(HEAD_SLABS),
                   pl.BlockSpec((None, HEAD_SLABS, tm), lambda b, t: (b, 0, t)),
                   tok(KV_LORA), tok(LANES), tok(POOL_WIDTH), perb(HIST_ROWS, POOL_WIDTH)],
        scratch_shapes=[pltpu.VMEM((HIST_ROWS + tm, POOL_WIDTH), F32)],
        compiler_params=_cparams(2),
        name="premix",
    )(x, mod, cos_t, sin_t, hist, w["g_pre_mix"], w["g_q_a"], w["g_kv_a"], w["pool_scale"],
      w["w_in"], w["w_q"], w["w_k"], w["w_vt"], w["w_pool"])


def _kvproj_kernel(kv_ref, kpe_ref, wk_ref, wvt_ref, k_ref, vt_ref):
    _store_kv(kv_ref[...].astype(BF16), kpe_ref[...], wk_ref, wvt_ref, k_ref, vt_ref)


def _kvproj(kv, kpe_slab, w_k, w_vt, *, tm):
    nb, nt, _ = kv.shape
    tok = lambda width: pl.BlockSpec((None, tm, width), lambda b, t: (b, t, 0))
    full = lambda a: pl.BlockSpec(a.shape, lambda b, t: (0,) * a.ndim)
    return pl.pallas_call(
        _kvproj_kernel,
        out_shape=[jax.ShapeDtypeStruct((nb, nt, HEAD_SLABS), BF16),
                   jax.ShapeDtypeStruct((nb, HEAD_SLABS, nt), BF16)],
        grid=(nb, nt // tm),
        in_specs=[tok(KV_LORA), tok(LANES), full(w_k), full(w_vt)],
        out_specs=[tok(HEAD_SLABS), pl.BlockSpec((None, HEAD_SLABS, tm), lambda b, t: (b, 0, t))],
        compiler_params=_cparams(2),
        name="kvproj",
    )(kv, kpe_slab, w_k, w_vt)


def _attn_kernel(q_ref, k_ref, vt_ref, o_ref, m_ref, acc_ref, *, tq, tk, nk, causal, kv_len):
    qi = pl.program_id(1)
    ki = pl.program_id(2)
    last = ((qi + 1) * tq - 1) // tk if causal else nk - 1
    need_len_mask = kv_len < nk * tk

    @pl.when(ki == 0)
    def _():
        m_ref[...] = jnp.full(m_ref.shape, NEG_INF, F32)
        acc_ref[...] = jnp.zeros(acc_ref.shape, F32)

    def step(masked, nkeys=tk):
        if masked:
            kpos = ki * tk + lax.broadcasted_iota(jnp.int32, (nkeys, tq), 0)
            vis = None
            if causal:
                qpos = qi * tq + lax.broadcasted_iota(jnp.int32, (nkeys, tq), 1)
                vis = (kpos // CHUNK) <= (qpos // CHUNK)
            if need_len_mask:
                lm = kpos < kv_len
                vis = lm if vis is None else (vis & lm)

        def scores(hd):
            sl = slice(hd * LANES, (hd + 1) * LANES)
            return lax.dot_general(k_ref[:nkeys, sl], q_ref[:, sl], (((1,), (1,)), ((), ())),
                                   preferred_element_type=F32)

        def accumulate(hd, alpha, p):
            sl = slice(hd * LANES, (hd + 1) * LANES)
            acc_ref[hd] = alpha * acc_ref[hd] + jnp.dot(vt_ref[sl, :nkeys], p, preferred_element_type=F32)

        s_next = scores(0)
        pending = None
        for hd in range(N_HEADS):
            s = s_next
            if hd + 1 < N_HEADS:
                s_next = scores(hd + 1)
            if pending is not None:
                accumulate(*pending)
            if masked:
                s = jnp.where(vis, s, NEG_INF)
            m_prev = m_ref[hd:hd + 1, :]
            m_new = jnp.maximum(m_prev, jnp.max(s, axis=0, keepdims=True))
            m_ref[hd:hd + 1, :] = m_new
            pending = (hd, jnp.exp2(m_prev - m_new), jnp.exp2(s - m_new).astype(BF16))
        accumulate(*pending)

    if causal and tk == 2 * tq and not need_len_mask:
        @pl.when(ki < last)
        def _():
            step(False)

        @pl.when((ki == last) & (qi % 2 == 0))
        def _():
            step(True, tq)

        @pl.when((ki == last) & (qi % 2 == 1))
        def _():
            step(True)
    elif causal or need_len_mask:
        @pl.when(ki < last)
        def _():
            step(False)

        @pl.when(ki == last)
        def _():
            step(True)
    else:
        step(False)

    @pl.when(ki == nk - 1)
    def _():
        for hd in range(N_HEADS):
            acc = acc_ref[hd]
            out_t = acc / acc[ONE_LANE:ONE_LANE + 1, :]
            o_ref[:, hd * LANES:(hd + 1) * LANES] = out_t.T.astype(BF16)


def _attention(q, k, vt, *, tq, tk, causal, kv_len):
    nb, nq_tot, _ = q.shape
    nk = k.shape[1] // tk
    nq = nq_tot // tq
    if causal:
        last = lambda i: ((i + 1) * tq - 1) // tk
        kmap = lambda b, i, j: (b, jnp.minimum(j, last(i)), 0)
        vmap = lambda b, i, j: (b, 0, jnp.minimum(j, last(i)))
    else:
        kmap = lambda b, i, j: (b, j, 0)
        vmap = lambda b, i, j: (b, 0, j)
    return pl.pallas_call(
        functools.partial(_attn_kernel, tq=tq, tk=tk, nk=nk, causal=causal, kv_len=kv_len),
        out_shape=jax.ShapeDtypeStruct((nb, nq_tot, HEAD_SLABS), BF16),
        grid=(nb, nq, nk),
        in_specs=[pl.BlockSpec((None, tq, HEAD_SLABS), lambda b, i, j: (b, i, 0)),
                  pl.BlockSpec((None, tk, HEAD_SLABS), kmap),
                  pl.BlockSpec((None, HEAD_SLABS, tk), vmap)],
        out_specs=pl.BlockSpec((None, tq, HEAD_SLABS), lambda b, i, j: (b, i, 0)),
        scratch_shapes=[pltpu.VMEM((N_HEADS, tq), F32), pltpu.VMEM((N_HEADS, LANES, tq), F32)],
        compiler_params=_cparams(3),
        name="attention",
    )(q, k, vt)


def _postmix_kernel(*refs, tm, cap, first_call):
    (attn_ref, pooled_ref, x_ref, mod_ref, cnt0_ref, gpost_ref, gffn_ref, woa_ref, wop_ref, wr_ref, br_ref) = refs[:11]
    rest = refs[11:] if first_call else refs[12:]
    x1_ref, dk_ref, rg_ref, cnt_ref, xs_ref, carry, ltri, hbuf, dvm, dsm, dsem, csem = rest
    step = pl.program_id(0) * pl.num_programs(1) + pl.program_id(1)
    n_steps = pl.num_programs(0) * pl.num_programs(1)
    slot = step % 2

    @pl.when(step == 0)
    def _():
        carry[...] = cnt0_ref[...]
        r = lax.broadcasted_iota(jnp.int32, (tm, tm), 0)
        c = lax.broadcasted_iota(jnp.int32, (tm, tm), 1)
        ltri[...] = (r > c).astype(BF16)

    n_groups = tm // SUBLANES
    n_chunks = min(8, n_groups)
    per_chunk = n_groups // n_chunks

    def issue_rows(buf, j0, j1):
        def issue(j, c):
            for s in range(SUBLANES):
                r = j * SUBLANES + s
                for kk in range(TOP_K):
                    pltpu.async_copy(hbuf.at[buf, r], xs_ref.at[dsm[kk, r]], dsem.at[buf], priority=1)
            return c

        lax.fori_loop(j0, j1, issue, 0)

    chunks = iter(range(n_chunks))

    def issue_prev_chunk():
        c = next(chunks, None)
        if c is not None:
            @pl.when(step >= 1)
            def _():
                issue_rows(1 - slot, c * per_chunk, (c + 1) * per_chunk)

    issue_prev_chunk()
    mix = (jnp.dot(attn_ref[...], woa_ref[...], preferred_element_type=F32)
           + jnp.dot(pooled_ref[...], wop_ref[...], preferred_element_type=F32))
    issue_prev_chunk()
    x1 = x_ref[...] + mod_ref[2:3, :] * _rms(mix, gpost_ref[...])
    x1_ref[...] = x1
    issue_prev_chunk()
    h2 = _rms(x1, gffn_ref[...]) * (1.0 + mod_ref[4:5, :]) + mod_ref[3:4, :]

    lane = lax.broadcasted_iota(jnp.int32, (tm, LANES), 1).astype(F32)
    logits = jnp.dot(h2.astype(BF16), wr_ref[...], preferred_element_type=F32) + br_ref[...]
    logits = jnp.where(lane < N_EXPERTS, logits, NEG_INF)
    issue_prev_chunk()
    sel = jnp.zeros((tm, LANES), F32)
    ids, vals, hots = [], [], []
    for _ in range(TOP_K):
        mk = jnp.max(logits, axis=1, keepdims=True)
        idx = jnp.min(jnp.where(logits == mk, lane, float(LANES)), axis=1, keepdims=True)
        hot = lane == idx
        logits = jnp.where(hot, NEG_INF, logits)
        sel = sel + hot.astype(F32)
        ids.append(idx)
        vals.append(mk)
        hots.append(hot)
        issue_prev_chunk()
    assert next(chunks, None) is None
    ex = [jnp.exp(vk - vals[0]) for vk in vals]
    denom = ex[0] + ex[1] + ex[2] + ex[3]

    before = jnp.dot(ltri[...], sel.astype(BF16), preferred_element_type=F32) + carry[0:1, :]
    ri = jnp.zeros((tm, LANES), F32)
    rg = jnp.zeros((tm, LANES), F32)
    for kk in range(TOP_K):
        rank = jnp.sum(jnp.where(hots[kk], before, 0.0), axis=1, keepdims=True)
        ri = jnp.where(lane == kk, ids[kk] * float(cap) + rank, ri)
        rg = jnp.where(lane == kk, ex[kk] / denom, rg)
    dest_t = ri.T[0:SUBLANES, :].astype(jnp.int32)
    dk_ref[...] = dest_t
    rg_ref[...] = rg
    carry[0:1, :] = carry[0:1, :] + jnp.sum(sel, axis=0, keepdims=True)
    cnt_ref[...] = carry[...]

    def drain(buf):
        for _ in range(TOP_K):
            pltpu.make_async_copy(xs_ref.at[pl.ds(0, tm)], xs_ref.at[pl.ds(0, tm)], dsem.at[buf]).wait()

    @pl.when(step >= 2)
    def _():
        drain(slot)

    hbuf[slot] = h2.reshape(tm, *ROW_TILE)
    dvm[...] = dest_t
    to_smem = pltpu.make_async_copy(dvm, dsm, csem)
    to_smem.start()
    to_smem.wait()

    @pl.when(step == n_steps - 1)
    def _():
        issue_rows(slot, 0, n_groups)
        drain(slot)

        @pl.when(step >= 1)
        def _():
            drain(1 - slot)


def _postmix(attn, pooled, x, mod, cnt0, w, xs, *, tm, cap):
    nb, nt, _ = x.shape
    first_call = xs is None
    tok = lambda width: pl.BlockSpec((None, tm, width), lambda b, t: (b, t, 0))
    full = lambda a: pl.BlockSpec(a.shape, lambda b, t: (0,) * a.ndim)
    outs = [
        jax.ShapeDtypeStruct((nb, nt, D_MODEL), F32),
        jax.ShapeDtypeStruct((nb, SUBLANES, nt), jnp.int32),
        jax.ShapeDtypeStruct((nb, nt, LANES), F32),
        jax.ShapeDtypeStruct((8, LANES), F32),
        jax.ShapeDtypeStruct((N_EXPERTS * cap,) + ROW_TILE, F32),
    ]
    in_specs = [tok(HEAD_SLABS), tok(POOL_WIDTH), tok(D_MODEL),
                pl.BlockSpec((None, N_MOD, D_MODEL), lambda b, t: (b, 0, 0)),
                full(cnt0), full(w["g_post_mix"]), full(w["g_pre_ffn"]),
                full(w["w_o_attn"]), full(w["w_o_pool"]), full(w["w_router"]), full(w["b_router"])]
    args = [attn, pooled, x, mod, cnt0, w["g_post_mix"], w["g_pre_ffn"],
            w["w_o_attn"], w["w_o_pool"], w["w_router"], w["b_router"]]
    if not first_call:
        in_specs.append(pl.BlockSpec(memory_space=pl.ANY))
        args.append(xs)
    return pl.pallas_call(
        functools.partial(_postmix_kernel, tm=tm, cap=cap, first_call=first_call),
        out_shape=outs,
        grid=(nb, nt // tm),
        in_specs=in_specs,
        out_specs=[tok(D_MODEL), pl.BlockSpec((None, SUBLANES, tm), lambda b, t: (b, 0, t)), tok(LANES),
                   pl.BlockSpec((8, LANES), lambda b, t: (0, 0)), pl.BlockSpec(memory_space=pl.ANY)],
        scratch_shapes=[pltpu.VMEM((8, LANES), F32), pltpu.VMEM((tm, tm), BF16),
                        pltpu.VMEM((2, tm) + ROW_TILE, F32), pltpu.VMEM((SUBLANES, tm), jnp.int32),
                        pltpu.SMEM((SUBLANES, tm), jnp.int32),
                        pltpu.SemaphoreType.DMA((2,)), pltpu.SemaphoreType.DMA(())],
        input_output_aliases={} if first_call else {11: 4},
        compiler_params=_cparams(2),
        name="postmix",
    )(*args)


def _zero_pads_kernel(zfrom_ref, xs_in_ref, xs_ref, zbuf, zsem):
    del xs_in_ref
    zbuf[...] = jnp.zeros(zbuf.shape, F32)
    copies = [pltpu.make_async_copy(zbuf, xs_ref.at[pl.ds(zfrom_ref[e], MOE_BLOCK)], zsem)
              for e in range(N_EXPERTS)]
    for cp in copies:
        cp.start()
    for cp in copies:
        cp.wait()


def _zero_pads(zfrom, xs):
    return pl.pallas_call(
        _zero_pads_kernel,
        out_shape=jax.ShapeDtypeStruct(xs.shape, xs.dtype),
        grid_spec=pltpu.PrefetchScalarGridSpec(
            num_scalar_prefetch=1,
            grid=(1,),
            in_specs=[pl.BlockSpec(memory_space=pl.ANY)],
            out_specs=pl.BlockSpec(memory_space=pl.ANY),
            scratch_shapes=[pltpu.VMEM((MOE_BLOCK,) + ROW_TILE, F32), pltpu.SemaphoreType.DMA(())],
        ),
        input_output_aliases={1: 0},
        compiler_params=_cparams(1),
        name="zero_pads",
    )(zfrom, xs)


def _expert_kernel(be_ref, bi_ref, nu_ref, xs_ref, wgu_ref, bgu_ref, wd_ref, bd_ref, ys_ref, wgu_b, wd_b):
    del bi_ref
    i = pl.program_id(0)

    @pl.when(i < nu_ref[0])
    def _():
        @pl.when((i == 0) | (be_ref[i] != be_ref[jnp.maximum(i - 1, 0)]))
        def _():
            wgu_b[...] = wgu_ref[...].astype(BF16)
            wd_b[...] = wd_ref[...].astype(BF16)

        x = xs_ref[...].reshape(MOE_BLOCK, D_MODEL).astype(BF16)
        gu = jnp.dot(x, wgu_b[...], preferred_element_type=F32) + bgu_ref[...]
        g = jnp.minimum(gu[:, :D_FF], SWIGLU_LIMIT)
        u = jnp.clip(gu[:, D_FF:], -SWIGLU_LIMIT, SWIGLU_LIMIT)
        a = (u + 1.0) * (g * jax.nn.sigmoid(SWIGLU_ALPHA * g))
        y = jnp.dot(a.astype(BF16), wd_b[...], preferred_element_type=F32) + bd_ref[...]
        ys_ref[...] = y.reshape(MOE_BLOCK, *ROW_TILE)


def _experts(block_e, block_idx, n_used, xs, w):
    nblk = block_e.shape[0]
    row = lambda i, be, bi, nu: (bi[i], 0, 0)
    per_e = lambda i, be, bi, nu: (be[i], 0, 0)
    return pl.pallas_call(
        _expert_kernel,
        out_shape=jax.ShapeDtypeStruct(xs.shape, F32),
        grid_spec=pltpu.PrefetchScalarGridSpec(
            num_scalar_prefetch=3,
            grid=(nblk,),
            in_specs=[pl.BlockSpec((MOE_BLOCK,) + ROW_TILE, row),
                      pl.BlockSpec((None, D_MODEL, 2 * D_FF), per_e),
                      pl.BlockSpec((None, 1, 2 * D_FF), per_e),
                      pl.BlockSpec((None, D_FF, D_MODEL), per_e),
                      pl.BlockSpec((None, 1, D_MODEL), per_e)],
            out_specs=pl.BlockSpec((MOE_BLOCK,) + ROW_TILE, row),
            scratch_shapes=[pltpu.VMEM((D_MODEL, 2 * D_FF), BF16), pltpu.VMEM((D_FF, D_MODEL), BF16)],
        ),
        compiler_params=_cparams(1),
        name="experts",
    )(block_e, block_idx, n_used, xs, w["w_gu"], w["b_gu"], w["w_down"], w["b_down"])


def _combine_kernel(dcur_ref, dnext_ref, x1_ref, rg_ref, mod_ref, gpost_ref, ys_ref, o_ref, gbuf, sem, *, tm, n):
    i = pl.program_id(0)
    slot = i % 2

    def gather(dest_ref, sl):
        def issue(j, carry):
            for s in range(SUBLANES):
                r = j * SUBLANES + s
                for kk in range(TOP_K):
                    pltpu.async_copy(ys_ref.at[dest_ref[0, kk, r]], gbuf.at[sl, kk, r], sem.at[sl], priority=1)
            return carry

        lax.fori_loop(0, tm // SUBLANES, issue, 0)

    @pl.when(i == 0)
    def _():
        gather(dcur_ref, 0)

    @pl.when(i + 1 < n)
    def _():
        gather(dnext_ref, 1 - slot)

    for kk in range(TOP_K):
        pltpu.make_async_copy(gbuf.at[1 - slot, kk], gbuf.at[slot, kk], sem.at[slot]).wait()

    rg = rg_ref[...]
    y = rg[:, 0:1] * gbuf[slot, 0].reshape(tm, D_MODEL)
    for kk in range(1, TOP_K):
        y = y + rg[:, kk:kk + 1] * gbuf[slot, kk].reshape(tm, D_MODEL)
    o_ref[...] = x1_ref[...] + mod_ref[5:6, :] * _rms(y, gpost_ref[...])


def _combine(dest, x1, rg, mod, g_post, ys, *, tm):
    nb, nt, _ = x1.shape
    ntile = nt // tm
    n = nb * ntile
    tok = lambda width: pl.BlockSpec((tm, width), lambda i: (i, 0))
    dspec = lambda f: pl.BlockSpec((1, SUBLANES, tm), lambda i: (f(i) // ntile, 0, f(i) % ntile),
                                   memory_space=pltpu.SMEM)
    out = pl.pallas_call(
        functools.partial(_combine_kernel, tm=tm, n=n),
        out_shape=jax.ShapeDtypeStruct((nb * nt, D_MODEL), F32),
        grid=(n,),
        in_specs=[dspec(lambda i: i), dspec(lambda i: jnp.minimum(i + 1, n - 1)),
                  tok(D_MODEL), tok(LANES),
                  pl.BlockSpec((None, N_MOD, D_MODEL), lambda i: (i // ntile, 0, 0)),
                  pl.BlockSpec(g_post.shape, lambda i: (0, 0)),
                  pl.BlockSpec(memory_space=pl.ANY)],
        out_specs=tok(D_MODEL),
        scratch_shapes=[pltpu.VMEM((2, TOP_K, tm) + ROW_TILE, F32),
                        pltpu.SemaphoreType.DMA((2,))],
        compiler_params=_cparams(1),
        name="combine",
    )(dest, dest, x1.reshape(nb * nt, D_MODEL), rg.reshape(nb * nt, LANES), mod, g_post, ys)
    return out.reshape(x1.shape)


def _rot_swap(w):
    half = QK_ROPE // 2
    return jnp.concatenate([-w[..., half:], w[..., :half]], axis=-1)


def _prep_weights(w_in, g_q_a, w_q_b, g_kv_a, w_uk, w_uv, w_pool, pool_scale, w_o, g_pre_mix, g_post_mix,
                  g_pre_ffn, w_router, b_router, w_gu, b_gu, w_down, b_down, g_post_ffn):
    row = lambda a: a.reshape(1, -1).astype(F32)
    w_kpe = w_in[:, U_OFF:U_OFF + QK_ROPE]
    zeros = lambda *s: jnp.zeros(s, F32)
    d = D_MODEL
    slab = lambda a: jnp.concatenate([zeros(d, QK_NOPE), a, zeros(d, LANES - QK_HEAD)], axis=1)
    w_in_ext = jnp.concatenate([w_in[:, :U_OFF], w_in[:, U_OFF + QK_ROPE:], slab(w_kpe), slab(_rot_swap(w_kpe))],
                               axis=1)
    pad_q = zeros(Q_LORA, N_HEADS, LANES - QK_HEAD)
    wq_plain = jnp.concatenate([w_q_b, pad_q], axis=2).reshape(Q_LORA, HEAD_SLABS)
    wq_swap = jnp.concatenate([zeros(Q_LORA, N_HEADS, QK_NOPE), _rot_swap(w_q_b[..., QK_NOPE:]), pad_q],
                              axis=2).reshape(Q_LORA, HEAD_SLABS)
    pad_kv = zeros(KV_LORA, N_HEADS, LANES - QK_NOPE)
    wk = jnp.concatenate([w_uk, pad_kv], axis=2).reshape(KV_LORA, HEAD_SLABS)
    wv = jnp.concatenate([w_uv, pad_kv], axis=2).reshape(KV_LORA, HEAD_SLABS)
    mla_w = N_HEADS * V_HEAD
    woa = jnp.concatenate([w_o[:mla_w].reshape(N_HEADS, V_HEAD, d), zeros(N_HEADS, LANES - V_HEAD, d)],
                          axis=1).reshape(HEAD_SLABS, d)
    return {
        "g_pre_mix": row(g_pre_mix), "g_q_a": row(g_q_a), "g_kv_a": row(g_kv_a), "pool_scale": row(pool_scale),
        "w_in": w_in_ext.astype(BF16),
        "w_q": jnp.concatenate([wq_plain, wq_swap], axis=1).astype(BF16),
        "w_k": wk.astype(BF16), "w_vt": wv.T.astype(BF16),
        "w_pool": w_pool.astype(BF16),
        "g_post_mix": row(g_post_mix), "g_pre_ffn": row(g_pre_ffn), "g_post_ffn": row(g_post_ffn),
        "w_o_attn": woa.astype(BF16), "w_o_pool": w_o[mla_w:].astype(BF16),
        "w_router": jnp.pad(w_router, ((0, 0), (0, LANES - N_EXPERTS))).astype(BF16),
        "b_router": jnp.pad(b_router, (0, LANES - N_EXPERTS)).reshape(1, LANES).astype(F32),
        "w_gu": w_gu, "b_gu": b_gu.reshape(N_EXPERTS, 1, 2 * D_FF).astype(F32),
        "w_down": w_down, "b_down": b_down.reshape(N_EXPERTS, 1, D_MODEL).astype(F32),
    }


def _rope_tables(pos):
    half = QK_ROPE // 2
    inv = ROPE_THETA ** (-jnp.arange(half, dtype=F32) / half)
    ang = pos.astype(F32)[:, None] * inv[None, :]
    cos, sin = jnp.cos(ang), jnp.sin(ang)
    n = pos.shape[0]
    cos_t = jnp.concatenate([jnp.ones((n, QK_NOPE), F32), cos, cos, jnp.zeros((n, LANES - QK_HEAD), F32)], axis=1)
    sin_t = jnp.concatenate([jnp.zeros((n, QK_NOPE), F32), sin, sin, jnp.zeros((n, LANES - QK_HEAD), F32)], axis=1)
    return cos_t, sin_t


def _tile(n, pref):
    return pref if n % pref == 0 else n


def _mixer_path(x, mod, pos0, hist, cache, w, cnt0, xs, cap):
    nb, nt, _ = x.shape
    tm = _tile(nt, 512)
    cos_t, sin_t = _rope_tables(pos0 + jnp.arange(nt, dtype=jnp.int32))
    q, k, vt, kv_new, kslab, pooled, pool_tail = _premix(x, mod, cos_t, sin_t, hist, w, tm=tm, pos0=pos0)
    if cache is None:
        attn = _attention(q, k, vt, tq=tm, tk=_tile(nt, 2 * tm), causal=True, kv_len=nt)
    else:
        ckv, ckpe = cache
        past = ckv.shape[1]
        kv_len = past + nt
        tk = -(-kv_len // 256) * 256
        ckpe_slab = jnp.pad(ckpe, ((0, 0), (0, 0), (QK_NOPE, LANES - QK_HEAD)))
        lat_all = jnp.concatenate([ckv, kv_new, jnp.zeros((nb, tk - kv_len, KV_LORA), F32)], axis=1)
        kpe_all = jnp.concatenate([ckpe_slab, kslab, jnp.zeros((nb, tk - kv_len, LANES), F32)], axis=1)
        k_all, vt_all = _kvproj(lat_all, kpe_all, w["w_k"], w["w_vt"], tm=_tile(tk, 768))
        tq = -(-nt // LANES) * LANES
        q_pad = jnp.pad(q, ((0, 0), (0, tq - nt), (0, 0)))
        attn = _attention(q_pad, k_all, vt_all, tq=tq, tk=tk, causal=False, kv_len=kv_len)[:, :nt]
    x1, ri, rg, cnt, xs = _postmix(attn, pooled, x, mod, cnt0, w, xs, tm=tm, cap=cap)
    return x1, xs, ri, rg, cnt, kv_new, kslab[..., QK_NOPE:QK_HEAD], pool_tail[:, 1:]


def kernel(x_prompt, x_sample, c_prompt, c_sample, cache_kv_latent, cache_k_rope, state_pool, w_ada, b_ada,
           g_pre_mix, w_in, g_q_a, w_q_b, g_kv_a, w_uk, w_uv, w_pool, pool_scale, w_o, g_post_mix, g_pre_ffn,
           w_router, b_router, w_gu, b_gu, w_down, b_down, g_post_ffn):
    assert w_ada.shape[0] == 1, "single-layer step"
    bp, sp, _ = x_prompt.shape
    bs, ss, _ = x_sample.shape
    past = cache_kv_latent.shape[2]
    w = _prep_weights(w_in[0], g_q_a[0], w_q_b[0], g_kv_a[0], w_uk[0], w_uv[0], w_pool[0], pool_scale[0], w_o[0],
                      g_pre_mix[0], g_post_mix[0], g_pre_ffn[0], w_router[0], b_router[0], w_gu[0], b_gu[0],
                      w_down[0], b_down[0], g_post_ffn[0])

    mod = _ada(jnp.concatenate([c_prompt, c_sample], axis=0), w_ada[0], b_ada[0])
    mod = mod.reshape(bp + bs, N_MOD, D_MODEL)
    mod_p, mod_s = mod[:bp], mod[bp:]

    hist_p = jnp.zeros((bp, HIST_ROWS, POOL_WIDTH), F32)
    hist_s = jnp.pad(state_pool[0], ((0, 0), (1, 0), (0, 0)))
    n_tok = bp * sp + bs * ss
    cap = (n_tok // MOE_BLOCK + 1) * MOE_BLOCK
    cnt0 = jnp.zeros((8, LANES), F32)
    x1p, xs, rip, rgp, cntp, kv_p, kpe_p, pool_p = _mixer_path(x_prompt, mod_p, 0, hist_p, None, w, cnt0, None, cap)
    x1s, xs, ris, rgs, cnts, kv_s, kpe_s, pool_s = _mixer_path(
        x_sample, mod_s, past, hist_s, (cache_kv_latent[0], cache_k_rope[0]), w, cntp, xs, cap)

    counts = cnts[0, :N_EXPERTS].astype(jnp.int32)
    blocks = (counts + MOE_BLOCK - 1) // MOE_BLOCK
    blk_end = jnp.cumsum(blocks)
    n_blocks = -(-(n_tok * TOP_K) // MOE_BLOCK) + N_EXPERTS
    n_used = blk_end[-1:].astype(jnp.int32)
    grid_i = jnp.minimum(jnp.arange(n_blocks, dtype=jnp.int32), n_used[0] - 1)
    block_e = jnp.sum((blk_end[None, :] <= grid_i[:, None]).astype(jnp.int32), axis=1)
    blk_start = blk_end - blocks
    block_idx = (block_e * (cap // MOE_BLOCK) + grid_i
                 - jnp.sum(jnp.where(block_e[:, None] == jnp.arange(N_EXPERTS, dtype=jnp.int32), blk_start, 0), axis=1))
    zfrom = (jnp.arange(N_EXPERTS, dtype=jnp.int32) * cap + counts).astype(jnp.int32)
    xs = _zero_pads(zfrom, xs)
    ys = _experts(block_e, block_idx.astype(jnp.int32), n_used, xs, w)
    y_p = _combine(rip, x1p, rgp, mod_p, w["g_post_ffn"], ys, tm=_tile(sp, 512))
    y_s = _combine(ris, x1s, rgs, mod_s, w["g_post_ffn"], ys, tm=_tile(ss, 256))
    return (y_p, y_s, kv_p[None], kpe_p[None], pool_p[None], kv_s[None], kpe_s[None], pool_s[None])
```

```python
import functools

import jax
import jax.numpy as jnp
from jax import lax
from jax.experimental import pallas as pl
from jax.experimental.pallas import tpu as pltpu

F32 = jnp.float32
BF16 = jnp.bfloat16

D_MODEL = 1024
CHUNK = 64
N_HEADS = 8
QK_NOPE = 64
QK_ROPE = 32
QK_HEAD = QK_NOPE + QK_ROPE
V_HEAD = 64
Q_LORA = 384
KV_LORA = 256
ROPE_THETA = 10000.0
POOL_WINDOWS = (2, 4, 8, 16)
POOL_GROUP_DIM = 128
POOL_WIDTH = POOL_GROUP_DIM * len(POOL_WINDOWS)
POOL_HIST = max(POOL_WINDOWS) - 1
HIST_ROWS = POOL_HIST + 1
N_EXPERTS = 32
TOP_K = 4
D_FF = 1024
SWIGLU_LIMIT = 7.0
SWIGLU_ALPHA = 1.702
N_MOD = 6
EPS = 1e-6

LANES = 128
HEAD_SLABS = N_HEADS * LANES
ONE_LANE = V_HEAD
IN_EXT = Q_LORA + KV_LORA + POOL_WIDTH + 2 * LANES
U_OFF = Q_LORA + KV_LORA
KPE_OFF = U_OFF + POOL_WIDTH
SM_SCALE = QK_HEAD ** -0.5
LOG2_E = 1.4426950408889634
Q_SCALE = SM_SCALE * LOG2_E
NEG_INF = float("-inf")

SUBLANES = 8
ROW_TILE = (SUBLANES, D_MODEL // SUBLANES)
MOE_BLOCK = 512
VMEM_LIMIT = 56 * 1024 * 1024


def _cparams(n_axes, vmem=VMEM_LIMIT):
    return pltpu.CompilerParams(dimension_semantics=("arbitrary",) * n_axes, vmem_limit_bytes=vmem)


def _rms(x, g):
    return x * lax.rsqrt(jnp.mean(x * x, axis=-1, keepdims=True) + EPS) * g


def _ada_kernel(c_ref, w_ref, b_ref, o_ref):
    c = c_ref[...]
    s = (c * jax.nn.sigmoid(c)).astype(BF16)
    o_ref[...] = jnp.dot(s, w_ref[...].astype(BF16), preferred_element_type=F32) + b_ref[...]


def _ada(c, w_ada, b_ada):
    nb = c.shape[0]
    return pl.pallas_call(
        _ada_kernel,
        out_shape=jax.ShapeDtypeStruct((nb, N_MOD * D_MODEL), F32),
        grid=(N_MOD,),
        in_specs=[pl.BlockSpec((nb, D_MODEL), lambda j: (0, 0)),
                  pl.BlockSpec((D_MODEL, D_MODEL), lambda j: (0, j)),
                  pl.BlockSpec((1, D_MODEL), lambda j: (0, j))],
        out_specs=pl.BlockSpec((nb, D_MODEL), lambda j: (0, j)),
        compiler_params=_cparams(1),
        name="ada",
    )(c, w_ada, b_ada.reshape(1, -1))


def _store_kv(kvb, kslab, wk_ref, wvt_ref, k_ref, vt_ref):
    kk = jnp.dot(kvb, wk_ref[...], preferred_element_type=F32)
    for hd in range(N_HEADS):
        sl = slice(hd * LANES, (hd + 1) * LANES)
        k_ref[:, sl] = (kk[:, sl] + kslab).astype(BF16)
    vt = lax.dot_general(wvt_ref[...], kvb, (((1,), (1,)), ((), ())), preferred_element_type=F32)
    row = lax.broadcasted_iota(jnp.int32, (HEAD_SLABS, 1), 0)
    vt_ref[...] = (vt + (row % LANES == ONE_LANE).astype(F32)).astype(BF16)


def _premix_kernel(x_ref, mod_ref, cos_ref, sin_ref, hist_ref, gpre_ref, gqa_ref, gkv_ref, pscale_ref,
                   win_ref, wq_ref, wk_ref, wvt_ref, wpool_ref,
                   q_ref, k_ref, vt_ref, kv_ref, kpe_ref, pooled_ref, poolnew_ref,
                   ubuf, *, tm, pos0):
    t = pl.program_id(1)
    x = x_ref[...]
    h = (_rms(x, gpre_ref[...]) * (1.0 + mod_ref[1:2, :]) + mod_ref[0:1, :]).astype(BF16)
    z = jnp.dot(h, win_ref[...], preferred_element_type=F32)
    cosv = cos_ref[...]
    sinv = sin_ref[...]

    qan = _rms(z[:, :Q_LORA], gqa_ref[...]).astype(BF16)
    qq = jnp.dot(qan, wq_ref[...], preferred_element_type=F32)
    for hd in range(N_HEADS):
        a = qq[:, hd * LANES:(hd + 1) * LANES]
        b = qq[:, HEAD_SLABS + hd * LANES:HEAD_SLABS + (hd + 1) * LANES]
        q_ref[:, hd * LANES:(hd + 1) * LANES] = ((a * cosv + b * sinv) * Q_SCALE).astype(BF16)

    kvn = _rms(z[:, Q_LORA:U_OFF], gkv_ref[...])
    kv_ref[...] = kvn
    kslab = z[:, KPE_OFF:KPE_OFF + LANES] * cosv + z[:, KPE_OFF + LANES:KPE_OFF + 2 * LANES] * sinv
    kpe_ref[...] = kslab
    _store_kv(kvn.astype(BF16), kslab, wk_ref, wvt_ref, k_ref, vt_ref)

    @pl.when(t == 0)
    def _():
        ubuf[0:HIST_ROWS, :] = hist_ref[...]

    @pl.when(t > 0)
    def _():
        ubuf[0:HIST_ROWS, :] = ubuf[tm:tm + HIST_ROWS, :]

    ubuf[HIST_ROWS:HIST_ROWS + tm, :] = z[:, U_OFF:KPE_OFF]
    pos = pos0 + t * tm + lax.broadcasted_iota(jnp.int32, (tm, 1), 0)
    for g, w in enumerate(POOL_WINDOWS):
        sl = slice(g * POOL_GROUP_DIM, (g + 1) * POOL_GROUP_DIM)
        u = ubuf[HIST_ROWS:HIST_ROWS + tm, sl]
        acc = u
        for j in range(1, w):
            acc = acc + ubuf[HIST_ROWS - j:HIST_ROWS - j + tm, sl]
        cnt = jnp.minimum(pos + 1, w).astype(F32)
        d = (acc / cnt - u).astype(BF16)
        y = jnp.dot(d, wpool_ref[g], preferred_element_type=F32) * pscale_ref[:, sl]
        pooled_ref[:, sl] = y.astype(BF16)
    poolnew_ref[...] = ubuf[tm:tm + HIST_ROWS, :]


def _premix(x, mod, cos_t, sin_t, hist, w, *, tm, pos0):
    nb, nt, _ = x.shape
    grid = (nb, nt // tm)
    tok = lambda width: pl.BlockSpec((None, tm, width), lambda b, t: (b, t, 0))
    full = lambda a: pl.BlockSpec(a.shape, lambda b, t: (0,) * a.ndim)
    tab = pl.BlockSpec((tm, LANES), lambda b, t: (t, 0))
    perb = lambda rows, width: pl.BlockSpec((None, rows, width), lambda b, t: (b, 0, 0))
    outs = [
        jax.ShapeDtypeStruct((nb, nt, HEAD_SLABS), BF16),
        jax.ShapeDtypeStruct((nb, nt, HEAD_SLABS), BF16),
        jax.ShapeDtypeStruct((nb, HEAD_SLABS, nt), BF16),
        jax.ShapeDtypeStruct((nb, nt, KV_LORA), F32),
        jax.ShapeDtypeStruct((nb, nt, LANES), F32),
        jax.ShapeDtypeStruct((nb, nt, POOL_WIDTH), BF16),
        jax.ShapeDtypeStruct((nb, HIST_ROWS, POOL_WIDTH), F32),
    ]
    return pl.pallas_call(
        functools.partial(_premix_kernel, tm=tm, pos0=pos0),
        out_shape=outs,
        grid=grid,
        in_specs=[tok(D_MODEL), perb(N_MOD, D_MODEL), tab, tab, perb(HIST_ROWS, POOL_WIDTH),
                  full(w["g_pre_mix"]), full(w["g_q_a"]), full(w["g_kv_a"]), full(w["pool_scale"]),
                  full(w["w_in"]), full(w["w_q"]), full(w["w_k"]), full(w["w_vt"]), full(w["w_pool"])],
        out_specs=[tok(HEAD_SLABS), tok(HEAD_SLABS),
                   pl.BlockSpec((None, HEAD_SLABS, tm), lambda b, t: (b, 0, t)),
                   tok(KV_LORA), tok(LANES), tok(POOL_WIDTH), perb(HIST_ROWS, POOL_WIDTH)],
        scratch_shapes=[pltpu.VMEM((HIST_ROWS + tm, POOL_WIDTH), F32)],
        compiler_params=_cparams(2),
        name="premix",
    )(x, mod, cos_t, sin_t, hist, w["g_pre_mix"], w["g_q_a"], w["g_kv_a"], w["pool_scale"],
      w["w_in"], w["w_q"], w["w_k"], w["w_vt"], w["w_pool"])


def _kvproj_kernel(kv_ref, kpe_ref, wk_ref, wvt_ref, k_ref, vt_ref):
    _store_kv(kv_ref[...].astype(BF16), kpe_ref[...], wk_ref, wvt_ref, k_ref, vt_ref)


def _kvproj(kv, kpe_slab, w_k, w_vt, *, tm):
    nb, nt, _ = kv.shape
    tok = lambda width: pl.BlockSpec((None, tm, width), lambda b, t: (b, t, 0))
    full = lambda a: pl.BlockSpec(a.shape, lambda b, t: (0,) * a.ndim)
    return pl.pallas_call(
        _kvproj_kernel,
        out_shape=[jax.ShapeDtypeStruct((nb, nt, HEAD_SLABS), BF16),
                   jax.ShapeDtypeStruct((nb, HEAD_SLABS, nt), BF16)],
        grid=(nb, nt // tm),
        in_specs=[tok(KV_LORA), tok(LANES), full(w_k), full(w_vt)],
        out_specs=[tok(HEAD_SLABS), pl.BlockSpec((None, HEAD_SLABS, tm), lambda b, t: (b, 0, t))],
        compiler_params=_cparams(2),
        name="kvproj",
    )(kv, kpe_slab, w_k, w_vt)


def _attn_kernel(q_ref, k_ref, vt_ref, o_ref, m_ref, acc_ref, *, tq, tk, nk, causal, kv_len):
    qi = pl.program_id(1)
    ki = pl.program_id(2)
    last = ((qi + 1) * tq - 1) // tk if causal else nk - 1
    need_len_mask = kv_len < nk * tk

    @pl.when(ki == 0)
    def _():
        m_ref[...] = jnp.full(m_ref.shape, NEG_INF, F32)
        acc_ref[...] = jnp.zeros(acc_ref.shape, F32)

    def step(masked, nkeys=tk):
        if masked:
            kpos = ki * tk + lax.broadcasted_iota(jnp.int32, (nkeys, tq), 0)
            vis = None
            if causal:
                qpos = qi * tq + lax.broadcasted_iota(jnp.int32, (nkeys, tq), 1)
                vis = (kpos // CHUNK) <= (qpos // CHUNK)
            if need_len_mask:
                lm = kpos < kv_len
                vis = lm if vis is None else (vis & lm)

        def scores(hd):
            sl = slice(hd * LANES, (hd + 1) * LANES)
            return lax.dot_general(k_ref[:nkeys, sl], q_ref[:, sl], (((1,), (1,)), ((), ())),
                                   preferred_element_type=F32)

        def accumulate(hd, alpha, p):
            sl = slice(hd * LANES, (hd + 1) * LANES)
            acc_ref[hd] = alpha * acc_ref[hd] + jnp.dot(vt_ref[sl, :nkeys], p, preferred_element_type=F32)

        s_next = scores(0)
        pending = None
        for hd in range(N_HEADS):
            s = s_next
            if hd + 1 < N_HEADS:
                s_next = scores(hd + 1)
            if pending is not None:
                accumulate(*pending)
            if masked:
                s = jnp.where(vis, s, NEG_INF)
            m_prev = m_ref[hd:hd + 1, :]
            m_new = jnp.maximum(m_prev, jnp.max(s, axis=0, keepdims=True))
            m_ref[hd:hd + 1, :] = m_new
            pending = (hd, jnp.exp2(m_prev - m_new), jnp.exp2(s - m_new).astype(BF16))
        accumulate(*pending)

    if causal and tk == 2 * tq and not need_len_mask:
        @pl.when(ki < last)
        def _():
            step(False)

        @pl.when((ki == last) & (qi % 2 == 0))
        def _():
            step(True, tq)

        @pl.when((ki == last) & (qi % 2 == 1))
        def _():
            step(True)
    elif causal or need_len_mask:
        @pl.when(ki < last)
        def _():
            step(False)

        @pl.when(ki == last)
        def _():
            step(True)
    else:
        step(False)

    @pl.when(ki == nk - 1)
    def _():
        for hd in range(N_HEADS):
            acc = acc_ref[hd]
            out_t = acc / acc[ONE_LANE:ONE_LANE + 1, :]
            o_ref[:, hd * LANES:(hd + 1) * LANES] = out_t.T.astype(BF16)


def _attention(q, k, vt, *, tq, tk, causal, kv_len):
    nb, nq_tot, _ = q.shape
    nk = k.shape[1] // tk
    nq = nq_tot // tq
    if causal:
        last = lambda i: ((i + 1) * tq - 1) // tk
        kmap = lambda b, i, j: (b, jnp.minimum(j, last(i)), 0)
        vmap = lambda b, i, j: (b, 0, jnp.minimum(j, last(i)))
    else:
        kmap = lambda b, i, j: (b, j, 0)
        vmap = lambda b, i, j: (b, 0, j)
    return pl.pallas_call(
        functools.partial(_attn_kernel, tq=tq, tk=tk, nk=nk, causal=causal, kv_len=kv_len),
        out_shape=jax.ShapeDtypeStruct((nb, nq_tot, HEAD_SLABS), BF16),
        grid=(nb, nq, nk),
        in_specs=[pl.BlockSpec((None, tq, HEAD_SLABS), lambda b, i, j: (b, i, 0)),
                  pl.BlockSpec((None, tk, HEAD_SLABS), kmap),
                  pl.BlockSpec((None, HEAD_SLABS, tk), vmap)],
        out_specs=pl.BlockSpec((None, tq, HEAD_SLABS), lambda b, i, j: (b, i, 0)),
        scratch_shapes=[pltpu.VMEM((N_HEADS, tq), F32), pltpu.VMEM((N_HEADS, LANES, tq), F32)],
        compiler_params=_cparams(3),
        name="attention",
    )(q, k, vt)


def _postmix_kernel(*refs, tm, cap, first_call):
    (attn_ref, pooled_ref, x_ref, mod_ref, cnt0_ref, gpost_ref, gffn_ref, woa_ref, wop_ref, wr_ref, br_ref) = refs[:11]
    rest = refs[11:] if first_call else refs[12:]
    x1_ref, dk_ref, rg_ref, cnt_ref, xs_ref, carry, ltri, hbuf, dvm, dsm, dsem, csem = rest
    step = pl.program_id(0) * pl.num_programs(1) + pl.program_id(1)
    n_steps = pl.num_programs(0) * pl.num_programs(1)
    slot = step % 2

    @pl.when(step == 0)
    def _():
        carry[...] = cnt0_ref[...]
        r = lax.broadcasted_iota(jnp.int32, (tm, tm), 0)
        c = lax.broadcasted_iota(jnp.int32, (tm, tm), 1)
        ltri[...] = (r > c).astype(BF16)

    n_groups = tm // SUBLANES
    n_chunks = min(8, n_groups)
    per_chunk = n_groups // n_chunks

    def issue_rows(buf, j0, j1):
        def issue(j, c):
            for s in range(SUBLANES):
                r = j * SUBLANES + s
                for kk in range(TOP_K):
                    pltpu.async_copy(hbuf.at[buf, r], xs_ref.at[dsm[kk, r]], dsem.at[buf], priority=1)
            return c

        lax.fori_loop(j0, j1, issue, 0)

    chunks = iter(range(n_chunks))

    def issue_prev_chunk():
        c = next(chunks, None)
        if c is not None:
            @pl.when(step >= 1)
            def _():
                issue_rows(1 - slot, c * per_chunk, (c + 1) * per_chunk)

    issue_prev_chunk()
    mix = (jnp.dot(attn_ref[...], woa_ref[...], preferred_element_type=F32)
           + jnp.dot(pooled_ref[...], wop_ref[...], preferred_element_type=F32))
    issue_prev_chunk()
    x1 = x_ref[...] + mod_ref[2:3, :] * _rms(mix, gpost_ref[...])
    x1_ref[...] = x1
    issue_prev_chunk()
    h2 = _rms(x1, gffn_ref[...]) * (1.0 + mod_ref[4:5, :]) + mod_ref[3:4, :]

    lane = lax.broadcasted_iota(jnp.int32, (tm, LANES), 1).astype(F32)
    logits = jnp.dot(h2.astype(BF16), wr_ref[...], preferred_element_type=F32) + br_ref[...]
    logits = jnp.where(lane < N_EXPERTS, logits, NEG_INF)
    issue_prev_chunk()
    sel = jnp.zeros((tm, LANES), F32)
    ids, vals, hots = [], [], []
    for _ in range(TOP_K):
        mk = jnp.max(logits, axis=1, keepdims=True)
        idx = jnp.min(jnp.where(logits == mk, lane, float(LANES)), axis=1, keepdims=True)
        hot = lane == idx
        logits = jnp.where(hot, NEG_INF, logits)
        sel = sel + hot.astype(F32)
        ids.append(idx)
        vals.append(mk)
        hots.append(hot)
        issue_prev_chunk()
    assert next(chunks, None) is None
    ex = [jnp.exp(vk - vals[0]) for vk in vals]
    denom = ex[0] + ex[1] + ex[2] + ex[3]

    before = jnp.dot(ltri[...], sel.astype(BF16), preferred_element_type=F32) + carry[0:1, :]
    ri = jnp.zeros((tm, LANES), F32)
    rg = jnp.zeros((tm, LANES), F32)
    for kk in range(TOP_K):
        rank = jnp.sum(jnp.where(hots[kk], before, 0.0), axis=1, keepdims=True)
        ri = jnp.where(lane == kk, ids[kk] * float(cap) + rank, ri)
        rg = jnp.where(lane == kk, ex[kk] / denom, rg)
    dest_t = ri.T[0:SUBLANES, :].astype(jnp.int32)
    dk_ref[...] = dest_t
    rg_ref[...] = rg
    carry[0:1, :] = carry[0:1, :] + jnp.sum(sel, axis=0, keepdims=True)
    cnt_ref[...] = carry[...]

    def drain(buf):
        for _ in range(TOP_K):
            pltpu.make_async_copy(xs_ref.at[pl.ds(0, tm)], xs_ref.at[pl.ds(0, tm)], dsem.at[buf]).wait()

    @pl.when(step >= 2)
    def _():
        drain(slot)

    hbuf[slot] = h2.reshape(tm, *ROW_TILE)
    dvm[...] = dest_t
    to_smem = pltpu.make_async_copy(dvm, dsm, csem)
    to_smem.start()
    to_smem.wait()

    @pl.when(step == n_steps - 1)
    def _():
        issue_rows(slot, 0, n_groups)
        drain(slot)

        @pl.when(step >= 1)
        def _():
            drain(1 - slot)


def _postmix(attn, pooled, x, mod, cnt0, w, xs, *, tm, cap):
    nb, nt, _ = x.shape
    first_call = xs is None
    tok = lambda width: pl.BlockSpec((None, tm, width), lambda b, t: (b, t, 0))
    full = lambda a: pl.BlockSpec(a.shape, lambda b, t: (0,) * a.ndim)
    outs = [
        jax.ShapeDtypeStruct((nb, nt, D_MODEL), F32),
        jax.ShapeDtypeStruct((nb, SUBLANES, nt), jnp.int32),
        jax.ShapeDtypeStruct((nb, nt, LANES), F32),
        jax.ShapeDtypeStruct((8, LANES), F32),
        jax.ShapeDtypeStruct((N_EXPERTS * cap,) + ROW_TILE, F32),
    ]
    in_specs = [tok(HEAD_SLABS), tok(POOL_WIDTH), tok(D_MODEL),
                pl.BlockSpec((None, N_MOD, D_MODEL), lambda b, t: (b, 0, 0)),
                full(cnt0), full(w["g_post_mix"]), full(w["g_pre_ffn"]),
                full(w["w_o_attn"]), full(w["w_o_pool"]), full(w["w_router"]), full(w["b_router"])]
    args = [attn, pooled, x, mod, cnt0, w["g_post_mix"], w["g_pre_ffn"],
            w["w_o_attn"], w["w_o_pool"], w["w_router"], w["b_router"]]
    if not first_call:
        in_specs.append(pl.BlockSpec(memory_space=pl.ANY))
        args.append(xs)
    return pl.pallas_call(
        functools.partial(_postmix_kernel, tm=tm, cap=cap, first_call=first_call),
        out_shape=outs,
        grid=(nb, nt // tm),
        in_specs=in_specs,
        out_specs=[tok(D_MODEL), pl.BlockSpec((None, SUBLANES, tm), lambda b, t: (b, 0, t)), tok(LANES),
                   pl.BlockSpec((8, LANES), lambda b, t: (0, 0)), pl.BlockSpec(memory_space=pl.ANY)],
        scratch_shapes=[pltpu.VMEM((8, LANES), F32), pltpu.VMEM((tm, tm), BF16),
                        pltpu.VMEM((2, tm) + ROW_TILE, F32), pltpu.VMEM((SUBLANES, tm), jnp.int32),
                        pltpu.SMEM((SUBLANES, tm), jnp.int32),
                        pltpu.SemaphoreType.DMA((2,)), pltpu.SemaphoreType.DMA(())],
        input_output_aliases={} if first_call else {11: 4},
        compiler_params=_cparams(2),
        name="postmix",
    )(*args)


def _zero_pads_kernel(zfrom_ref, xs_in_ref, xs_ref, zbuf, zsem):
    del xs_in_ref
    zbuf[...] = jnp.zeros(zbuf.shape, F32)
    copies = [pltpu.make_async_copy(zbuf, xs_ref.at[pl.ds(zfrom_ref[e], MOE_BLOCK)], zsem)
              for e in range(N_EXPERTS)]
    for cp in copies:
        cp.start()
    for cp in copies:
        cp.wait()


def _zero_pads(zfrom, xs):
    return pl.pallas_call(
        _zero_pads_kernel,
        out_shape=jax.ShapeDtypeStruct(xs.shape, xs.dtype),
        grid_spec=pltpu.PrefetchScalarGridSpec(
            num_scalar_prefetch=1,
            grid=(1,),
            in_specs=[pl.BlockSpec(memory_space=pl.ANY)],
            out_specs=pl.BlockSpec(memory_space=pl.ANY),
            scratch_shapes=[pltpu.VMEM((MOE_BLOCK,) + ROW_TILE, F32), pltpu.SemaphoreType.DMA(())],
        ),
        input_output_aliases={1: 0},
        compiler_params=_cparams(1),
        name="zero_pads",
    )(zfrom, xs)


def _expert_kernel(be_ref, bi_ref, nu_ref, xs_ref, wgu_ref, bgu_ref, wd_ref, bd_ref, ys_ref, wgu_b, wd_b):
    del bi_ref
    i = pl.program_id(0)

    @pl.when(i < nu_ref[0])
    def _():
        @pl.when((i == 0) | (be_ref[i] != be_ref[jnp.maximum(i - 1, 0)]))
        def _():
            wgu_b[...] = wgu_ref[...].astype(BF16)
            wd_b[...] = wd_ref[...].astype(BF16)

        x = xs_ref[...].reshape(MOE_BLOCK, D_MODEL).astype(BF16)
        gu = jnp.dot(x, wgu_b[...], preferred_element_type=F32) + bgu_ref[...]
        g = jnp.minimum(gu[:, :D_FF], SWIGLU_LIMIT)
        u = jnp.clip(gu[:, D_FF:], -SWIGLU_LIMIT, SWIGLU_LIMIT)
        a = (u + 1.0) * (g * jax.nn.sigmoid(SWIGLU_ALPHA * g))
        y = jnp.dot(a.astype(BF16), wd_b[...], preferred_element_type=F32) + bd_ref[...]
        ys_ref[...] = y.reshape(MOE_BLOCK, *ROW_TILE)


def _experts(block_e, block_idx, n_used, xs, w):
    nblk = block_e.shape[0]
    row = lambda i, be, bi, nu: (bi[i], 0, 0)
    per_e = lambda i, be, bi, nu: (be[i], 0, 0)
    return pl.pallas_call(
        _expert_kernel,
        out_shape=jax.ShapeDtypeStruct(xs.shape, F32),
        grid_spec=pltpu.PrefetchScalarGridSpec(
            num_scalar_prefetch=3,
            grid=(nblk,),
            in_specs=[pl.BlockSpec((MOE_BLOCK,) + ROW_TILE, row),
                      pl.BlockSpec((None, D_MODEL, 2 * D_FF), per_e),
                      pl.BlockSpec((None, 1, 2 * D_FF), per_e),
                      pl.BlockSpec((None, D_FF, D_MODEL), per_e),
                      pl.BlockSpec((None, 1, D_MODEL), per_e)],
            out_specs=pl.BlockSpec((MOE_BLOCK,) + ROW_TILE, row),
            scratch_shapes=[pltpu.VMEM((D_MODEL, 2 * D_FF), BF16), pltpu.VMEM((D_FF, D_MODEL), BF16)],
        ),
        compiler_params=_cparams(1),
        name="experts",
    )(block_e, block_idx, n_used, xs, w["w_gu"], w["b_gu"], w["w_down"], w["b_down"])


def _combine_kernel(dcur_ref, dnext_ref, x1_ref, rg_ref, mod_ref, gpost_ref, ys_ref, o_ref, gbuf, sem, *, tm, n):
    i = pl.program_id(0)
    slot = i % 2

    def gather(dest_ref, sl):
        def issue(j, carry):
            for s in range(SUBLANES):
                r = j * SUBLANES + s
                for kk in range(TOP_K):
                    pltpu.async_copy(ys_ref.at[dest_ref[0, kk, r]], gbuf.at[sl, kk, r], sem.at[sl], priority=kk % 2)
            return carry

        lax.fori_loop(0, tm // SUBLANES, issue, 0)

    @pl.when(i == 0)
    def _():
        gather(dcur_ref, 0)

    @pl.when(i + 1 < n)
    def _():
        gather(dnext_ref, 1 - slot)

    for kk in range(TOP_K):
        pltpu.make_async_copy(gbuf.at[1 - slot, kk], gbuf.at[slot, kk], sem.at[slot]).wait()

    rg = rg_ref[...]
    y = rg[:, 0:1] * gbuf[slot, 0].reshape(tm, D_MODEL)
    for kk in range(1, TOP_K):
        y = y + rg[:, kk:kk + 1] * gbuf[slot, kk].reshape(tm, D_MODEL)
    o_ref[...] = x1_ref[...] + mod_ref[5:6, :] * _rms(y, gpost_ref[...])


def _combine(dest, x1, rg, mod, g_post, ys, *, tm):
    nb, nt, _ = x1.shape
    ntile = nt // tm
    n = nb * ntile
    tok = lambda width: pl.BlockSpec((tm, width), lambda i: (i, 0))
    dspec = lambda f: pl.BlockSpec((1, SUBLANES, tm), lambda i: (f(i) // ntile, 0, f(i) % ntile),
                                   memory_space=pltpu.SMEM)
    out = pl.pallas_call(
        functools.partial(_combine_kernel, tm=tm, n=n),
        out_shape=jax.ShapeDtypeStruct((nb * nt, D_MODEL), F32),
        grid=(n,),
        in_specs=[dspec(lambda i: i), dspec(lambda i: jnp.minimum(i + 1, n - 1)),
                  tok(D_MODEL), tok(LANES),
                  pl.BlockSpec((None, N_MOD, D_MODEL), lambda i: (i // ntile, 0, 0)),
                  pl.BlockSpec(g_post.shape, lambda i: (0, 0)),
                  pl.BlockSpec(memory_space=pl.ANY)],
        out_specs=tok(D_MODEL),
        scratch_shapes=[pltpu.VMEM((2, TOP_K, tm) + ROW_TILE, F32),
                        pltpu.SemaphoreType.DMA((2,))],
        compiler_params=_cparams(1),
        name="combine",
    )(dest, dest, x1.reshape(nb * nt, D_MODEL), rg.reshape(nb * nt, LANES), mod, g_post, ys)
    return out.reshape(x1.shape)


def _rot_swap(w):
    half = QK_ROPE // 2
    return jnp.concatenate([-w[..., half:], w[..., :half]], axis=-1)


def _prep_weights(w_in, g_q_a, w_q_b, g_kv_a, w_uk, w_uv, w_pool, pool_scale, w_o, g_pre_mix, g_post_mix,
                  g_pre_ffn, w_router, b_router, w_gu, b_gu, w_down, b_down, g_post_ffn):
    row = lambda a: a.reshape(1, -1).astype(F32)
    w_kpe = w_in[:, U_OFF:U_OFF + QK_ROPE]
    zeros = lambda *s: jnp.zeros(s, F32)
    d = D_MODEL
    slab = lambda a: jnp.concatenate([zeros(d, QK_NOPE), a, zeros(d, LANES - QK_HEAD)], axis=1)
    w_in_ext = jnp.concatenate([w_in[:, :U_OFF], w_in[:, U_OFF + QK_ROPE:], slab(w_kpe), slab(_rot_swap(w_kpe))],
                               axis=1)
    pad_q = zeros(Q_LORA, N_HEADS, LANES - QK_HEAD)
    wq_plain = jnp.concatenate([w_q_b, pad_q], axis=2).reshape(Q_LORA, HEAD_SLABS)
    wq_swap = jnp.concatenate([zeros(Q_LORA, N_HEADS, QK_NOPE), _rot_swap(w_q_b[..., QK_NOPE:]), pad_q],
                              axis=2).reshape(Q_LORA, HEAD_SLABS)
    pad_kv = zeros(KV_LORA, N_HEADS, LANES - QK_NOPE)
    wk = jnp.concatenate([w_uk, pad_kv], axis=2).reshape(KV_LORA, HEAD_SLABS)
    wv = jnp.concatenate([w_uv, pad_kv], axis=2).reshape(KV_LORA, HEAD_SLABS)
    mla_w = N_HEADS * V_HEAD
    woa = jnp.concatenate([w_o[:mla_w].reshape(N_HEADS, V_HEAD, d), zeros(N_HEADS, LANES - V_HEAD, d)],
                          axis=1).reshape(HEAD_SLABS, d)
    return {
        "g_pre_mix": row(g_pre_mix), "g_q_a": row(g_q_a), "g_kv_a": row(g_kv_a), "pool_scale": row(pool_scale),
        "w_in": w_in_ext.astype(BF16),
        "w_q": jnp.concatenate([wq_plain, wq_swap], axis=1).astype(BF16),
        "w_k": wk.astype(BF16), "w_vt": wv.T.astype(BF16),
        "w_pool": w_pool.astype(BF16),
        "g_post_mix": row(g_post_mix), "g_pre_ffn": row(g_pre_ffn), "g_post_ffn": row(g_post_ffn),
        "w_o_attn": woa.astype(BF16), "w_o_pool": w_o[mla_w:].astype(BF16),
        "w_router": jnp.pad(w_router, ((0, 0), (0, LANES - N_EXPERTS))).astype(BF16),
        "b_router": jnp.pad(b_router, (0, LANES - N_EXPERTS)).reshape(1, LANES).astype(F32),
        "w_gu": w_gu, "b_gu": b_gu.reshape(N_EXPERTS, 1, 2 * D_FF).astype(F32),
        "w_down": w_down, "b_down": b_down.reshape(N_EXPERTS, 1, D_MODEL).astype(F32),
    }


def _rope_tables(pos):
    half = QK_ROPE // 2
    inv = ROPE_THETA ** (-jnp.arange(half, dtype=F32) / half)
    ang = pos.astype(F32)[:, None] * inv[None, :]
    cos, sin = jnp.cos(ang), jnp.sin(ang)
    n = pos.shape[0]
    cos_t = jnp.concatenate([jnp.ones((n, QK_NOPE), F32), cos, cos, jnp.zeros((n, LANES - QK_HEAD), F32)], axis=1)
    sin_t = jnp.concatenate([jnp.zeros((n, QK_NOPE), F32), sin, sin, jnp.zeros((n, LANES - QK_HEAD), F32)], axis=1)
    return cos_t, sin_t


def _tile(n, pref):
    return pref if n % pref == 0 else n


def _mixer_path(x, mod, pos0, hist, cache, w, cnt0, xs, cap):
    nb, nt, _ = x.shape
    tm = _tile(nt, 512)
    cos_t, sin_t = _rope_tables(pos0 + jnp.arange(nt, dtype=jnp.int32))
    q, k, vt, kv_new, kslab, pooled, pool_tail = _premix(x, mod, cos_t, sin_t, hist, w, tm=tm, pos0=pos0)
    if cache is None:
        attn = _attention(q, k, vt, tq=tm, tk=_tile(nt, 2 * tm), causal=True, kv_len=nt)
    else:
        ckv, ckpe = cache
        past = ckv.shape[1]
        kv_len = past + nt
        tk = -(-kv_len // 256) * 256
        ckpe_slab = jnp.pad(ckpe, ((0, 0), (0, 0), (QK_NOPE, LANES - QK_HEAD)))
        lat_all = jnp.concatenate([ckv, kv_new, jnp.zeros((nb, tk - kv_len, KV_LORA), F32)], axis=1)
        kpe_all = jnp.concatenate([ckpe_slab, kslab, jnp.zeros((nb, tk - kv_len, LANES), F32)], axis=1)
        k_all, vt_all = _kvproj(lat_all, kpe_all, w["w_k"], w["w_vt"], tm=_tile(tk, 768))
        tq = -(-nt // LANES) * LANES
        q_pad = jnp.pad(q, ((0, 0), (0, tq - nt), (0, 0)))
        attn = _attention(q_pad, k_all, vt_all, tq=tq, tk=tk, causal=False, kv_len=kv_len)[:, :nt]
    x1, ri, rg, cnt, xs = _postmix(attn, pooled, x, mod, cnt0, w, xs, tm=tm, cap=cap)
    return x1, xs, ri, rg, cnt, kv_new, kslab[..., QK_NOPE:QK_HEAD], pool_tail[:, 1:]


def kernel(x_prompt, x_sample, c_prompt, c_sample, cache_kv_latent, cache_k_rope, state_pool, w_ada, b_ada,
           g_pre_mix, w_in, g_q_a, w_q_b, g_kv_a, w_uk, w_uv, w_pool, pool_scale, w_o, g_post_mix, g_pre_ffn,
           w_router, b_router, w_gu, b_gu, w_down, b_down, g_post_ffn):
    assert w_ada.shape[0] == 1, "single-layer step"
    bp, sp, _ = x_prompt.shape
    bs, ss, _ = x_sample.shape
    past = cache_kv_latent.shape[2]
    w = _prep_weights(w_in[0], g_q_a[0], w_q_b[0], g_kv_a[0], w_uk[0], w_uv[0], w_pool[0], pool_scale[0], w_o[0],
                      g_pre_mix[0], g_post_mix[0], g_pre_ffn[0], w_router[0], b_router[0], w_gu[0], b_gu[0],
                      w_down[0], b_down[0], g_post_ffn[0])

    mod = _ada(jnp.concatenate([c_prompt, c_sample], axis=0), w_ada[0], b_ada[0])
    mod = mod.reshape(bp + bs, N_MOD, D_MODEL)
    mod_p, mod_s = mod[:bp], mod[bp:]

    hist_p = jnp.zeros((bp, HIST_ROWS, POOL_WIDTH), F32)
    hist_s = jnp.pad(state_pool[0], ((0, 0), (1, 0), (0, 0)))
    n_tok = bp * sp + bs * ss
    cap = (n_tok // MOE_BLOCK + 1) * MOE_BLOCK
    cnt0 = jnp.zeros((8, LANES), F32)
    x1p, xs, rip, rgp, cntp, kv_p, kpe_p, pool_p = _mixer_path(x_prompt, mod_p, 0, hist_p, None, w, cnt0, None, cap)
    x1s, xs, ris, rgs, cnts, kv_s, kpe_s, pool_s = _mixer_path(
        x_sample, mod_s, past, hist_s, (cache_kv_latent[0], cache_k_rope[0]), w, cntp, xs, cap)

    counts = cnts[0, :N_EXPERTS].astype(jnp.int32)
    blocks = (counts + MOE_BLOCK - 1) // MOE_BLOCK
    blk_end = jnp.cumsum(blocks)
    n_blocks = -(-(n_tok * TOP_K) // MOE_BLOCK) + N_EXPERTS
    n_used = blk_end[-1:].astype(jnp.int32)
    grid_i = jnp.minimum(jnp.arange(n_blocks, dtype=jnp.int32), n_used[0] - 1)
    block_e = jnp.sum((blk_end[None, :] <= grid_i[:, None]).astype(jnp.int32), axis=1)
    blk_start = blk_end - blocks
    block_idx = (block_e * (cap // MOE_BLOCK) + grid_i
                 - jnp.sum(jnp.where(block_e[:, None] == jnp.arange(N_EXPERTS, dtype=jnp.int32), blk_start, 0), axis=1))
    zfrom = (jnp.arange(N_EXPERTS, dtype=jnp.int32) * cap + counts).astype(jnp.int32)
    xs = _zero_pads(zfrom, xs)
    ys = _experts(block_e, block_idx.astype(jnp.int32), n_used, xs, w)
    y_p = _combine(rip, x1p, rgp, mod_p, w["g_post_ffn"], ys, tm=_tile(sp, 512))
    y_s = _combine(ris, x1s, rgs, mod_s, w["g_post_ffn"], ys, tm=_tile(ss, 256))
    return (y_p, y_s, kv_p[None], kpe_p[None], pool_p[None], kv_s[None], kpe_s[None], pool_s[None])
```
